```python
import math
import jax, jax.numpy as jnp
from jax import lax
import numpy as np

D_MODEL = 1024
BATCH = 8
SEQ = 2048
DEPTH = 4
DEC_BATCH = 128
DEC_SEQ = 8
PAST_LEN = 16384
PAGE_SIZE = 128

N_MIXERS = 3
N_LAYERS_A = (DEPTH + 2) // 3
N_LAYERS_B = (DEPTH + 1) // 3
N_LAYERS_C = DEPTH // 3
D_FF = 4 * D_MODEL
CONV_W = 4
NORM_EPS = 1e-6

D_LRU = D_MODEL
LRU_BLOCKS = 8
LRU_BLOCK = D_LRU // LRU_BLOCKS
LRU_C = 8.0

D_SSM = 2 * D_MODEL
SSM_HEADDIM = 64
SSM_HEADS = D_SSM // SSM_HEADDIM
SSM_GROUPS = 8
SSM_HPG = SSM_HEADS // SSM_GROUPS
SSM_STATE = 128
SSM_CONV_DIM = D_SSM + 2 * SSM_GROUPS * SSM_STATE
SSM_IN = D_SSM + SSM_CONV_DIM + SSM_HEADS
SSM_CHUNK = 128
SSM_NORM_EPS = 1e-5

RWKV_HEAD = 64
RWKV_HEADS = D_MODEL // RWKV_HEAD
D_DECAY_LORA = 64
D_AAA_LORA = 64
D_GATE_LORA = 128
GN_EPS = 64e-5

kernel_name = "hybrid_rglru_ssd_rwkv7_step"


def rmsnorm(x, w):
    xf = x.astype(jnp.float32)
    y = xf * lax.rsqrt(jnp.mean(xf * xf, axis=-1, keepdims=True) + NORM_EPS)
    return (y * w.astype(jnp.float32)).astype(x.dtype)


def causal_conv(x, buf, w, b):
    L = x.shape[1]
    xp = jnp.concatenate([buf.astype(x.dtype), x], axis=1)
    y = b + xp[:, 0:L] * w[0]
    for k in range(1, CONV_W):
        y = y + xp[:, k:k + L] * w[k]
    return y, xp[:, -(CONV_W - 1):]


def _lin_combine(e1, e2):
    a1, b1 = e1
    a2, b2 = e2
    return a1 * a2, a2 * b1 + b2


def rglru_mixer(u, conv_buf, h0, start_pos, w_in, conv_w, conv_b, w_r, b_r, w_i, b_i, lam, w_out):
    bsz, L, _ = u.shape
    f32 = jnp.float32
    xb, gate = jnp.split(u @ w_in, 2, axis=-1)
    xc, new_buf = causal_conv(xb, conv_buf, conv_w, conv_b)
    xblk = xc.reshape(bsz, L, LRU_BLOCKS, LRU_BLOCK)
    r = jax.nn.sigmoid(jnp.einsum("blhi,hij->blhj", xblk, w_r).reshape(bsz, L, D_LRU) + b_r)
    i = jax.nn.sigmoid(jnp.einsum("blhi,hij->blhj", xblk, w_i).reshape(bsz, L, D_LRU) + b_i)
    log_a = -LRU_C * r.astype(f32) * jax.nn.softplus(-lam.astype(f32))
    reset = ((start_pos + jnp.arange(L)) == 0)[None, :, None]
    a = jnp.where(reset, 0.0, jnp.exp(log_a))
    mult = jnp.where(reset, 1.0, jnp.sqrt(-jnp.expm1(2.0 * log_a)))
    bterm = mult * (i * xc).astype(f32)
    bterm = bterm.at[:, 0].add(a[:, 0] * h0.astype(f32))
    _, h = lax.associative_scan(_lin_combine, (a, bterm), axis=1)
    y = (h * jax.nn.gelu(gate.astype(f32))).astype(u.dtype)
    return y @ w_out, new_buf, h[:, -1].astype(h0.dtype)


def ssd_chunked(x, dt, A, B, C, s0):
    f32 = jnp.float32
    bsz, L = x.shape[:2]
    q = SSM_CHUNK if L % SSM_CHUNK == 0 else L
    nc = L // q
    xc = x.reshape(bsz, nc, q, SSM_GROUPS, SSM_HPG, SSM_HEADDIM).astype(f32)
    dtc = dt.reshape(bsz, nc, q, SSM_GROUPS, SSM_HPG).astype(f32)
    Bc = B.reshape(bsz, nc, q, SSM_GROUPS, SSM_STATE).astype(f32)
    Cc = C.reshape(bsz, nc, q, SSM_GROUPS, SSM_STATE).astype(f32)
    acs = jnp.cumsum(dtc * A, axis=2)
    xdt = xc * dtc[..., None]
    seg = acs[:, :, :, None] - acs[:, :, None, :]
    mask = jnp.tril(jnp.ones((q, q), bool))[:, :, None, None]
    Ldec = jnp.exp(jnp.where(mask, seg, -jnp.inf))
    CB = jnp.einsum("bcign,bcjgn->bcijg", Cc, Bc)
    y_diag = jnp.einsum("bcijg,bcijge,bcjgep->bcigep", CB, Ldec, xdt)
    decay_end = jnp.exp(acs[:, :, -1:] - acs)
    st = jnp.einsum("bcjgn,bcjge,bcjgep->bcgepn", Bc, decay_end, xdt)
    chunk_decay = jnp.exp(acs[:, :, -1])

    def step(s, inp):
        dec, st_c = inp
        return s * dec[..., None, None] + st_c, s

    s_fin, s_start = lax.scan(step, s0.astype(f32),
                              (jnp.moveaxis(chunk_decay, 1, 0), jnp.moveaxis(st, 1, 0)))
    s_start = jnp.moveaxis(s_start, 0, 1)
    y_off = jnp.einsum("bcign,bcige,bcgepn->bcigep", Cc, jnp.exp(acs), s_start)
    y = (y_diag + y_off).reshape(bsz, L, SSM_GROUPS, SSM_HPG, SSM_HEADDIM)
    return y, s_fin


def mamba2_mixer(u, conv_buf, ssm0, w_in, conv_w, conv_b, dt_bias, a_log, d_skip, norm_w, w_out):
    bsz, L, _ = u.shape
    f32 = jnp.float32
    z, xbc, dt = jnp.split(u @ w_in, [D_SSM, D_SSM + SSM_CONV_DIM], axis=-1)
    xbc, new_buf = causal_conv(xbc, conv_buf, conv_w, conv_b)
    xbc = jax.nn.silu(xbc)
    xs, Bm, Cm = jnp.split(xbc, [D_SSM, D_SSM + SSM_GROUPS * SSM_STATE], axis=-1)
    xs = xs.reshape(bsz, L, SSM_GROUPS, SSM_HPG, SSM_HEADDIM)
    Bm = Bm.reshape(bsz, L, SSM_GROUPS, SSM_STATE)
    Cm = Cm.reshape(bsz, L, SSM_GROUPS, SSM_STATE)
    dt = jax.nn.softplus(dt.astype(f32) + dt_bias.astype(f32)).reshape(bsz, L, SSM_GROUPS, SSM_HPG)
    A = -jnp.exp(a_log.astype(f32)).reshape(SSM_GROUPS, SSM_HPG)
    s0 = ssm0.reshape(bsz, SSM_GROUPS, SSM_HPG, SSM_HEADDIM, SSM_STATE)
    y, s_fin = ssd_chunked(xs, dt, A, Bm, Cm, s0)
    y = y + d_skip.astype(f32).reshape(SSM_GROUPS, SSM_HPG, 1) * xs.astype(f32)
    yg = (y.reshape(bsz, L, D_SSM) * jax.nn.silu(z.astype(f32))).reshape(bsz, L, SSM_GROUPS, D_SSM // SSM_GROUPS)
    yg = yg * lax.rsqrt(jnp.mean(yg * yg, axis=-1, keepdims=True) + SSM_NORM_EPS)
    yg = (yg.reshape(bsz, L, D_SSM) * norm_w.astype(f32)).astype(u.dtype)
    s_fin = s_fin.reshape(bsz, SSM_HEADS, SSM_HEADDIM, SSM_STATE).astype(ssm0.dtype)
    return yg @ w_out, new_buf, s_fin


def rwkv7_mixer(u, shift0, wkv0, mu, w_rkv, w0, w_w1, w_w2, a0, w_a1, w_a2, w_g1, w_g2,
                k_k, k_a, r_k, lnx_w, lnx_b, w_out):
    bsz, L, _ = u.shape
    f32 = jnp.float32
    prev = jnp.concatenate([shift0[:, None].astype(u.dtype), u[:, :-1]], axis=1)
    xm = u[None] + (prev - u)[None] * mu[:, None, None, :]
    r, k, v = jnp.einsum("sbld,sde->sble", xm[:3], w_rkv)
    xw, xa, xg = xm[3], xm[4], xm[5]
    w_log = -jax.nn.softplus(-(w0 + jnp.tanh(xw @ w_w1) @ w_w2)) - 0.5
    decay = jnp.exp(-jnp.exp(w_log.astype(f32)))
    a = jax.nn.sigmoid(a0 + (xa @ w_a1) @ w_a2)
    g = jax.nn.sigmoid(xg @ w_g1) @ w_g2

    def heads(t):
        return t.reshape(bsz, L, RWKV_HEADS, RWKV_HEAD).astype(f32)

    kk = heads(k * k_k)
    kk = kk / jnp.maximum(jnp.linalg.norm(kk, axis=-1, keepdims=True), 1e-12)
    k = k * (1.0 + (a - 1.0) * k_a)
    r_h, k_h, v_h, a_h, w_h = heads(r), heads(k), heads(v), heads(a), heads(decay)

    def step(S, inp):
        r_t, k_t, v_t, kk_t, a_t, w_t = inp
        sa = jnp.einsum("bhvk,bhk->bhv", S, kk_t)
        S = (S * w_t[:, :, None, :] - sa[..., None] * (kk_t * a_t)[:, :, None, :]
             + v_t[..., None] * k_t[:, :, None, :])
        return S, jnp.einsum("bhvk,bhk->bhv", S, r_t)

    seq = tuple(jnp.moveaxis(t, 1, 0) for t in (r_h, k_h, v_h, kk, a_h, w_h))
    S_fin, o = lax.scan(step, wkv0.astype(f32), seq)
    o = jnp.moveaxis(o, 0, 1)
    mean = jnp.mean(o, axis=-1, keepdims=True)
    var = jnp.mean(jnp.square(o - mean), axis=-1, keepdims=True)
    o = ((o - mean) * lax.rsqrt(var + GN_EPS)).reshape(bsz, L, D_MODEL) * lnx_w + lnx_b
    bonus = (jnp.sum(r_h * k_h * r_k, axis=-1, keepdims=True) * v_h).reshape(bsz, L, D_MODEL)
    out = ((o + bonus) * g).astype(u.dtype)
    return out @ w_out, u[:, -1], S_fin.astype(wkv0.dtype)


def sqrelu_mlp(u, w1, w2):
    return jnp.square(jax.nn.relu(u @ w1)) @ w2


def run_trunk(x, lru_conv, lru_h, ssm_conv, ssm_state, rw_shift, rw_wkv, start_pos, p):
    n_lc, n_lh, n_sc, n_ss, n_rs, n_rw = [], [], [], [], [], []
    ia = ib = ic = 0
    for layer in range(DEPTH):
        u = rmsnorm(x, p["norm_mix"][layer])
        kind = layer % N_MIXERS
        if kind == 0:
            out, cb, h = rglru_mixer(u, lru_conv[ia], lru_h[ia], start_pos,
                                     p["lru_w_in"][ia], p["lru_conv_w"][ia], p["lru_conv_b"][ia],
                                     p["lru_w_r"][ia], p["lru_b_r"][ia], p["lru_w_i"][ia], p["lru_b_i"][ia],
                                     p["lru_lambda"][ia], p["lru_w_out"][ia])
            n_lc.append(cb)
            n_lh.append(h)
            ia += 1
        elif kind == 1:
            out, cb, s = mamba2_mixer(u, ssm_conv[ib], ssm_state[ib],
                                      p["ssm_w_in"][ib], p["ssm_conv_w"][ib], p["ssm_conv_b"][ib],
                                      p["ssm_dt_bias"][ib], p["ssm_a_log"][ib], p["ssm_d"][ib],
                                      p["ssm_norm_w"][ib], p["ssm_w_out"][ib])
            n_sc.append(cb)
            n_ss.append(s)
            ib += 1
        else:
            out, sh, S = rwkv7_mixer(u, rw_shift[ic], rw_wkv[ic],
                                     p["rwkv_mu"][ic], p["rwkv_w_rkv"][ic], p["rwkv_w0"][ic],
                                     p["rwkv_w_w1"][ic], p["rwkv_w_w2"][ic], p["rwkv_a0"][ic],
                                     p["rwkv_w_a1"][ic], p["rwkv_w_a2"][ic], p["rwkv_w_g1"][ic],
                                     p["rwkv_w_g2"][ic], p["rwkv_k_k"][ic], p["rwkv_k_a"][ic],
                                     p["rwkv_r_k"][ic], p["rwkv_lnx_w"][ic], p["rwkv_lnx_b"][ic],
                                     p["rwkv_w_out"][ic])
            n_rs.append(sh)
            n_rw.append(S)
            ic += 1
        x = x + out.astype(x.dtype)
        x = x + sqrelu_mlp(rmsnorm(x, p["norm_ffn"][layer]), p["ffn_w1"][layer], p["ffn_w2"][layer]).astype(x.dtype)
    y = rmsnorm(x, p["norm_final"])
    return (y, jnp.stack(n_lc), jnp.stack(n_lh), jnp.stack(n_sc), jnp.stack(n_ss),
            jnp.stack(n_rs), jnp.stack(n_rw))


def setup_inputs(seed: int = 0) -> dict:
    key = jax.random.key(seed)
    ks = iter(jax.random.split(key, 64))
    f32 = jnp.float32

    def nrm(shape, scale):
        return jax.random.normal(next(ks), shape, f32) * scale

    def uni(shape, lo, hi):
        return jax.random.uniform(next(ks), shape, f32, lo, hi)

    nA, nB, nC = N_LAYERS_A, N_LAYERS_B, N_LAYERS_C
    a_init = uni((nA, D_LRU), 0.9, 0.999)
    s = a_init ** (1.0 / LRU_C)
    lru_lambda = jnp.log(s) - jnp.log1p(-s)
    dt_init = jnp.exp(uni((nB, SSM_HEADS), math.log(1e-3), math.log(1e-1)))
    ssm_dt_bias = dt_init + jnp.log(-jnp.expm1(-dt_init))
    return {
        "x_prompt": nrm((BATCH, SEQ, D_MODEL), 1.0),
        "x_sample": nrm((DEC_BATCH, DEC_SEQ, D_MODEL), 1.0),
        "state_lru_conv": nrm((nA, DEC_BATCH, CONV_W - 1, D_LRU), 1.0),
        "state_lru_h": nrm((nA, DEC_BATCH, D_LRU), 0.5),
        "state_ssm_conv": nrm((nB, DEC_BATCH, CONV_W - 1, SSM_CONV_DIM), 1.0),
        "state_ssm": nrm((nB, DEC_BATCH, SSM_HEADS, SSM_HEADDIM, SSM_STATE), 0.1),
        "state_rwkv_shift": nrm((nC, DEC_BATCH, D_MODEL), 1.0),
        "state_rwkv_wkv": nrm((nC, DEC_BATCH, RWKV_HEADS, RWKV_HEAD, RWKV_HEAD), 0.1),
        "norm_mix": 1.0 + nrm((DEPTH, D_MODEL), 0.01),
        "norm_ffn": 1.0 + nrm((DEPTH, D_MODEL), 0.01),
        "norm_final": 1.0 + nrm((D_MODEL,), 0.01),
        "lru_w_in": nrm((nA, D_MODEL, 2 * D_LRU), D_MODEL ** -0.5),
        "lru_conv_w": nrm((nA, CONV_W, D_LRU), CONV_W ** -0.5),
        "lru_conv_b": nrm((nA, D_LRU), 0.01),
        "lru_w_r": nrm((nA, LRU_BLOCKS, LRU_BLOCK, LRU_BLOCK), LRU_BLOCK ** -0.5),
        "lru_b_r": nrm((nA, D_LRU), 0.01),
        "lru_w_i": nrm((nA, LRU_BLOCKS, LRU_BLOCK, LRU_BLOCK), LRU_BLOCK ** -0.5),
        "lru_b_i": nrm((nA, D_LRU), 0.01),
        "lru_lambda": lru_lambda,
        "lru_w_out": nrm((nA, D_LRU, D_MODEL), D_LRU ** -0.5),
        "ssm_w_in": nrm((nB, D_MODEL, SSM_IN), D_MODEL ** -0.5),
        "ssm_conv_w": nrm((nB, CONV_W, SSM_CONV_DIM), CONV_W ** -0.5),
        "ssm_conv_b": nrm((nB, SSM_CONV_DIM), 0.01),
        "ssm_dt_bias": ssm_dt_bias,
        "ssm_a_log": jnp.log(uni((nB, SSM_HEADS), 1.0, 16.0)),
        "ssm_d": 1.0 + nrm((nB, SSM_HEADS), 0.01),
        "ssm_norm_w": 1.0 + nrm((nB, D_SSM), 0.01),
        "ssm_w_out": nrm((nB, D_SSM, D_MODEL), D_SSM ** -0.5),
        "rwkv_mu": uni((nC, 6, D_MODEL), 0.0, 1.0),
        "rwkv_w_rkv": nrm((nC, 3, D_MODEL, D_MODEL), D_MODEL ** -0.5),
        "rwkv_w0": -1.0 + nrm((nC, D_MODEL), 0.5),
        "rwkv_w_w1": nrm((nC, D_MODEL, D_DECAY_LORA), D_MODEL ** -0.5),
        "rwkv_w_w2": nrm((nC, D_DECAY_LORA, D_MODEL), 0.1 * D_DECAY_LORA ** -0.5),
        "rwkv_a0": nrm((nC, D_MODEL), 0.1),
        "rwkv_w_a1": nrm((nC, D_MODEL, D_AAA_LORA), D_MODEL ** -0.5),
        "rwkv_w_a2": nrm((nC, D_AAA_LORA, D_MODEL), 0.1 * D_AAA_LORA ** -0.5),
        "rwkv_w_g1": nrm((nC, D_MODEL, D_GATE_LORA), D_MODEL ** -0.5),
        "rwkv_w_g2": nrm((nC, D_GATE_LORA, D_MODEL), D_GATE_LORA ** -0.5),
        "rwkv_k_k": 1.0 + nrm((nC, D_MODEL), 0.1),
        "rwkv_k_a": 1.0 + nrm((nC, D_MODEL), 0.1),
        "rwkv_r_k": nrm((nC, RWKV_HEADS, RWKV_HEAD), 0.1),
        "rwkv_lnx_w": 1.0 + nrm((nC, D_MODEL), 0.01),
        "rwkv_lnx_b": nrm((nC, D_MODEL), 0.01),
        "rwkv_w_out": nrm((nC, D_MODEL, D_MODEL), D_MODEL ** -0.5),
        "ffn_w1": nrm((DEPTH, D_MODEL, D_FF), D_MODEL ** -0.5),
        "ffn_w2": nrm((DEPTH, D_FF, D_MODEL), 0.5 * D_FF ** -0.5),
    }


def reference(x_prompt, x_sample, state_lru_conv, state_lru_h, state_ssm_conv, state_ssm,
              state_rwkv_shift, state_rwkv_wkv, norm_mix, norm_ffn, norm_final,
              lru_w_in, lru_conv_w, lru_conv_b, lru_w_r, lru_b_r, lru_w_i, lru_b_i, lru_lambda, lru_w_out,
              ssm_w_in, ssm_conv_w, ssm_conv_b, ssm_dt_bias, ssm_a_log, ssm_d, ssm_norm_w, ssm_w_out,
              rwkv_mu, rwkv_w_rkv, rwkv_w0, rwkv_w_w1, rwkv_w_w2, rwkv_a0, rwkv_w_a1, rwkv_w_a2,
              rwkv_w_g1, rwkv_w_g2, rwkv_k_k, rwkv_k_a, rwkv_r_k, rwkv_lnx_w, rwkv_lnx_b, rwkv_w_out,
              ffn_w1, ffn_w2):
    p = dict(norm_mix=norm_mix, norm_ffn=norm_ffn, norm_final=norm_final,
             lru_w_in=lru_w_in, lru_conv_w=lru_conv_w, lru_conv_b=lru_conv_b, lru_w_r=lru_w_r,
             lru_b_r=lru_b_r, lru_w_i=lru_w_i, lru_b_i=lru_b_i, lru_lambda=lru_lambda, lru_w_out=lru_w_out,
             ssm_w_in=ssm_w_in, ssm_conv_w=ssm_conv_w, ssm_conv_b=ssm_conv_b, ssm_dt_bias=ssm_dt_bias,
             ssm_a_log=ssm_a_log, ssm_d=ssm_d, ssm_norm_w=ssm_norm_w, ssm_w_out=ssm_w_out,
             rwkv_mu=rwkv_mu, rwkv_w_rkv=rwkv_w_rkv, rwkv_w0=rwkv_w0, rwkv_w_w1=rwkv_w_w1,
             rwkv_w_w2=rwkv_w_w2, rwkv_a0=rwkv_a0, rwkv_w_a1=rwkv_w_a1, rwkv_w_a2=rwkv_w_a2,
             rwkv_w_g1=rwkv_w_g1, rwkv_w_g2=rwkv_w_g2, rwkv_k_k=rwkv_k_k, rwkv_k_a=rwkv_k_a,
             rwkv_r_k=rwkv_r_k, rwkv_lnx_w=rwkv_lnx_w, rwkv_lnx_b=rwkv_lnx_b, rwkv_w_out=rwkv_w_out,
             ffn_w1=ffn_w1, ffn_w2=ffn_w2)
    bp = x_prompt.shape[0]
    sdt = state_lru_h.dtype
    y_prompt, lc_p, lh_p, sc_p, ss_p, rs_p, rw_p = run_trunk(
        x_prompt,
        jnp.zeros((N_LAYERS_A, bp, CONV_W - 1, D_LRU), sdt),
        jnp.zeros((N_LAYERS_A, bp, D_LRU), sdt),
        jnp.zeros((N_LAYERS_B, bp, CONV_W - 1, SSM_CONV_DIM), sdt),
        jnp.zeros((N_LAYERS_B, bp, SSM_HEADS, SSM_HEADDIM, SSM_STATE), sdt),
        jnp.zeros((N_LAYERS_C, bp, D_MODEL), sdt),
        jnp.zeros((N_LAYERS_C, bp, RWKV_HEADS, RWKV_HEAD, RWKV_HEAD), sdt),
        0, p)
    y_sample, lc_s, lh_s, sc_s, ss_s, rs_s, rw_s = run_trunk(
        x_sample, state_lru_conv, state_lru_h, state_ssm_conv, state_ssm,
        state_rwkv_shift, state_rwkv_wkv, PAST_LEN, p)
    return (y_prompt, y_sample, lc_p, lc_s, lh_p, lh_s, sc_p, sc_s, ss_p, ss_s, rs_p, rs_s, rw_p, rw_s)
```

```python
import functools
import math

import jax
import jax.numpy as jnp
from jax import lax
from jax.experimental import pallas as pl
from jax.experimental.pallas import tpu as pltpu

F32 = jnp.float32
BF16 = jnp.bfloat16

NORM_EPS = 1e-6
CONV_W = 4
LRU_BLOCKS = 8
LRU_C = 8.0
SSM_HEADDIM = 64
SSM_GROUPS = 8
SSM_STATE = 128
SSM_CHUNK = 128
SSM_NORM_EPS = 1e-5
RWKV_HEAD = 64
GN_EPS = 64e-5
PAST_LEN = 16384

V7X_LANES = 128
V7X_SUBLANES = 8
V7X_VMEM_BYTES = 64 * 1024 * 1024
VMEM_LIMIT = V7X_VMEM_BYTES - 8 * 1024 * 1024

RWKV_CHUNK = 64
CONV_PAD = V7X_SUBLANES


def _bdot(a, b):
    return jnp.dot(a.astype(BF16), b.astype(BF16), preferred_element_type=F32)


def _bdot_nt(a, b):
    return lax.dot_general(a.astype(BF16), b.astype(BF16), (((1,), (1,)), ((), ())),
                           preferred_element_type=F32)


def _bdot_tn(a, b):
    return lax.dot_general(a.astype(BF16), b.astype(BF16), (((0,), (0,)), ((), ())),
                           preferred_element_type=F32)


def _rms(x, g):
    return x * lax.rsqrt(jnp.mean(x * x, axis=-1, keepdims=True) + NORM_EPS) * g


def _sigmoid(x):
    return 1.0 / (1.0 + jnp.exp(-x))


def _softplus(x):
    return jnp.maximum(x, 0.0) + jnp.log1p(jnp.exp(-jnp.abs(x)))


def _silu(x):
    return x * _sigmoid(x)


def _gelu_tanh(x):
    c = math.sqrt(2.0 / math.pi)
    return 0.5 * x * (1.0 + jnp.tanh(c * (x + 0.044715 * (x * x * x))))


def _row_index(rows, period):
    return lax.broadcasted_iota(jnp.int32, (rows, 1), 0) & (period - 1)


def _cumsum_rows(x, t_in, period):
    d = 1
    while d < period:
        x = x + jnp.where(t_in >= d, pltpu.roll(x, d, axis=0), 0.0)
        d *= 2
    return x


def _head_sum(x, head):
    rows, width = x.shape
    lo = lax.broadcasted_iota(jnp.int32, (1, V7X_LANES), 1) < head
    parts = []
    for j in range(width // V7X_LANES):
        seg = x[:, j * V7X_LANES:(j + 1) * V7X_LANES]
        s_lo = jnp.sum(jnp.where(lo, seg, 0.0), axis=-1, keepdims=True)
        s_hi = jnp.sum(jnp.where(lo, 0.0, seg), axis=-1, keepdims=True)
        parts.append(jnp.where(lo, s_lo, s_hi))
    return jnp.concatenate(parts, axis=1)


def _const_spec(shape):
    nd = len(shape)
    return pl.BlockSpec(shape, lambda *_: (0,) * nd, pipeline_mode=pl.Buffered(1))


def _params(sem):
    return pltpu.CompilerParams(dimension_semantics=sem, vmem_limit_bytes=VMEM_LIMIT)


def _proj_kernel(x_ref, g_ref, *refs, n_out, col_chunk):
    w_refs, o_refs = refs[:n_out], refs[n_out:]
    u = _rms(x_ref[...], g_ref[...]).astype(BF16)
    for w_ref, o_ref in zip(w_refs, o_refs):
        n = w_ref.shape[1]
        step = min(col_chunk, n)
        for c0 in range(0, n, step):
            o_ref[:, c0:c0 + step] = jnp.dot(u, w_ref[:, c0:c0 + step], preferred_element_type=F32)


def _norm_proj(x, g, weights, *, tm):
    t, d = x.shape
    assert t % tm == 0
    n_out = len(weights)
    return pl.pallas_call(
        functools.partial(_proj_kernel, n_out=n_out, col_chunk=1024),
        out_shape=[jax.ShapeDtypeStruct((t, w.shape[1]), F32) for w in weights],
        grid=(t // tm,),
        in_specs=[pl.BlockSpec((tm, d), lambda i: (i, 0)), _const_spec((1, d))]
        + [_const_spec(w.shape) for w in weights],
        out_specs=[pl.BlockSpec((tm, w.shape[1]), lambda i: (i, 0)) for w in weights],
        compiler_params=_params(("parallel",)),
        name="norm_proj",
    )(x, g, *weights)


def _ffn_kernel(x_ref, g_ref, w1_ref, w2_ref, gf_ref, o_ref, *, f_chunk, final_norm):
    x = x_ref[...]
    u = _rms(x, g_ref[...]).astype(BF16)
    acc = x
    for c0 in range(0, w1_ref.shape[1], f_chunk):
        h = jnp.dot(u, w1_ref[:, c0:c0 + f_chunk], preferred_element_type=F32)
        h = jnp.square(jnp.maximum(h, 0.0)).astype(BF16)
        acc = acc + jnp.dot(h, w2_ref[c0:c0 + f_chunk, :], preferred_element_type=F32)
    if final_norm:
        acc = _rms(acc, gf_ref[...])
    o_ref[...] = acc


def _ffn(x, g, w1, w2, g_final, *, tm, final_norm):
    t, d = x.shape
    f = w1.shape[1]
    assert t % tm == 0
    return pl.pallas_call(
        functools.partial(_ffn_kernel, f_chunk=1024, final_norm=final_norm),
        out_shape=jax.ShapeDtypeStruct((t, d), F32),
        grid=(t // tm,),
        in_specs=[pl.BlockSpec((tm, d), lambda i: (i, 0)), _const_spec((1, d)),
                  _const_spec((d, f)), _const_spec((f, d)), _const_spec((1, d))],
        out_specs=pl.BlockSpec((tm, d), lambda i: (i, 0)),
        compiler_params=_params(("parallel",)),
        name="ffn",
    )(x, g, w1, w2, g_final)


def _conv_taps(xe_ref, cw, cb, tl):
    nd = len(xe_ref.shape)

    def win(off):
        if nd == 3:
            return xe_ref[:, off:off + tl, :]
        return xe_ref[off:off + tl, :]

    base = CONV_PAD - (CONV_W - 1)
    y = cb + win(base) * cw[0:1, :]
    for k in range(1, CONV_W):
        y = y + win(base + k) * cw[k:k + 1, :]
    return y


def _lru_kernel(xg_ref, x_ref, cs_ref, h0_ref, cw_ref, cb_ref, wr_ref, br_ref, wi_ref, bi_ref, lam_ref, wo_ref,
                o_ref, cso_ref, ho_ref, xe_ref, hc_ref, *, nb, tl, start_pos):
    c = pl.program_id(1)
    rows = nb * tl
    d = x_ref.shape[1]
    blk = d // LRU_BLOCKS

    @pl.when(c == 0)
    def _():
        xe_ref[:, 0:CONV_PAD, :] = cs_ref[...]
        hc_ref[...] = h0_ref[...]

    xe_ref[:, CONV_PAD:CONV_PAD + tl, :] = xg_ref[:, 0:d].reshape(nb, tl, d)
    xc = _conv_taps(xe_ref, cw_ref[...], cb_ref[...], tl).reshape(rows, d)
    tail = xe_ref[:, tl:tl + CONV_PAD, :]
    xe_ref[:, 0:CONV_PAD, :] = tail
    cso_ref[...] = tail

    xcb = xc.astype(BF16)
    r_pre = jnp.concatenate(
        [jnp.dot(xcb[:, j * blk:(j + 1) * blk], wr_ref[j], preferred_element_type=F32) for j in range(LRU_BLOCKS)],
        axis=1) + br_ref[...]
    i_pre = jnp.concatenate(
        [jnp.dot(xcb[:, j * blk:(j + 1) * blk], wi_ref[j], preferred_element_type=F32) for j in range(LRU_BLOCKS)],
        axis=1) + bi_ref[...]
    r = _sigmoid(r_pre)
    gi = _sigmoid(i_pre)
    log_a = (-LRU_C) * r * _softplus(-lam_ref[...])
    a = jnp.exp(log_a)
    mult = jnp.sqrt(-jnp.tanh(log_a) * (a * a + 1.0))
    t_in = _row_index(rows, tl)
    if start_pos == 0:
        reset = (c * tl + t_in) == 0
        a = jnp.where(reset, 0.0, a)
        mult = jnp.where(reset, 1.0, mult)
    b = mult * (gi * xc)
    h_prev = jnp.broadcast_to(hc_ref[...], (nb, tl, d)).reshape(rows, d)
    b = b + jnp.where(t_in == 0, a * h_prev, 0.0)

    step = 1
    while step < tl:
        m = t_in >= step
        b = jnp.where(m, a * pltpu.roll(b, step, axis=0) + b, b)
        if 2 * step < tl:
            a = jnp.where(m, a * pltpu.roll(a, step, axis=0), a)
        step *= 2
    h_last = b.reshape(nb, tl, d)[:, tl - 1:tl, :]
    hc_ref[...] = h_last
    ho_ref[...] = h_last

    y = (b * _gelu_tanh(xg_ref[:, d:2 * d])).astype(BF16)
    o_ref[...] = x_ref[...] + jnp.dot(y, wo_ref[...], preferred_element_type=F32)


def _lru_mixer(x, xg, conv_state, h0, cw, cb, wr, br, wi, bi, lam, wo, *, batch, seq, nb, tl, start_pos):
    d = x.shape[1]
    nchunk = seq // tl
    assert batch % nb == 0 and seq % tl == 0 and (nb == 1 or nchunk == 1)
    rows = nb * tl
    row_map = lambda i, c: (i * nchunk + c, 0)
    seq_map = lambda i, c: (i, 0, 0)
    return pl.pallas_call(
        functools.partial(_lru_kernel, nb=nb, tl=tl, start_pos=start_pos),
        out_shape=[jax.ShapeDtypeStruct((batch * seq, d), F32),
                   jax.ShapeDtypeStruct((batch, CONV_PAD, d), F32),
                   jax.ShapeDtypeStruct((batch, 1, d), F32)],
        grid=(batch // nb, nchunk),
        in_specs=[pl.BlockSpec((rows, 2 * d), row_map), pl.BlockSpec((rows, d), row_map),
                  pl.BlockSpec((nb, CONV_PAD, d), seq_map), pl.BlockSpec((nb, 1, d), seq_map),
                  _const_spec(cw.shape), _const_spec(cb.shape), _const_spec(wr.shape), _const_spec(br.shape),
                  _const_spec(wi.shape), _const_spec(bi.shape), _const_spec(lam.shape), _const_spec(wo.shape)],
        out_specs=[pl.BlockSpec((rows, d), row_map), pl.BlockSpec((nb, CONV_PAD, d), seq_map),
                   pl.BlockSpec((nb, 1, d), seq_map)],
        scratch_shapes=[pltpu.VMEM((nb, CONV_PAD + tl, d), F32), pltpu.VMEM((nb, 1, d), F32)],
        compiler_params=_params(("parallel", "arbitrary")),
        name="lru_mixer",
    )(xg, x, conv_state, h0, cw, cb, wr, br, wi, bi, lam, wo)


def _ssd_kernel(z_ref, xbc_ref, dt_ref, x_ref, cs_ref, s0_ref, cw_ref, cb_ref, dtb_ref, alog_ref, dsk_ref, nw_ref,
                wo_ref, o_ref, cso_ref, so_ref, xe_ref, s_ref, *, q, zero_state):
    c = pl.program_id(1)
    d_ssm = z_ref.shape[1]
    n = SSM_STATE
    p = SSM_HEADDIM
    hpg = d_ssm // p // SSM_GROUPS
    gw = hpg * p

    @pl.when(c == 0)
    def _():
        xe_ref[0:CONV_PAD, :] = cs_ref[0]
        if zero_state:
            s_ref[...] = jnp.zeros(s_ref.shape, F32)
        else:
            s_ref[...] = s0_ref[0]

    xe_ref[CONV_PAD:CONV_PAD + q, :] = xbc_ref[...]
    xbc = _silu(_conv_taps(xe_ref, cw_ref[...], cb_ref[...], q))
    tail = xe_ref[q:q + CONV_PAD, :]
    xe_ref[0:CONV_PAD, :] = tail
    cso_ref[0] = tail

    xs = xbc[:, 0:d_ssm]
    bm = xbc[:, d_ssm:d_ssm + SSM_GROUPS * n]
    cm = xbc[:, d_ssm + SSM_GROUPS * n:]
    dt = _softplus(dt_ref[...] + dtb_ref[...])
    da = dt * (-jnp.exp(alog_ref[...]))
    t_in = _row_index(q, q)
    acs = _cumsum_rows(da, t_in, q)
    if q < V7X_LANES:
        acs_sq = jnp.concatenate([acs, jnp.zeros((V7X_LANES - q, V7X_LANES), F32)], axis=0)
    else:
        acs_sq = acs
    acs_t = acs_sq.T
    acs_last = acs[q - 1:q, :]
    dec_end = jnp.exp(acs_last - acs)
    e_acs = jnp.exp(acs)
    e_last = jnp.exp(acs_last)
    tri = lax.broadcasted_iota(jnp.int32, (q, q), 0) >= lax.broadcasted_iota(jnp.int32, (q, q), 1)
    dsk = dsk_ref[...]

    y_parts = []
    for g in range(SSM_GROUPS):
        bg = bm[:, g * n:(g + 1) * n].astype(BF16)
        cg = cm[:, g * n:(g + 1) * n].astype(BF16)
        cb_mat = _bdot_nt(cg, bg)
        xdtd = []
        dec_rows = []
        for e in range(hpg):
            h = g * hpg + e
            xs_h = xs[:, h * p:(h + 1) * p]
            xdt_h = xs_h * dt[:, h:h + 1]
            seg = acs[:, h:h + 1] - acs_t[h:h + 1, 0:q]
            l_mat = jnp.where(tri, jnp.exp(jnp.minimum(seg, 0.0)), 0.0)
            y_h = _bdot(cb_mat * l_mat, xdt_h)
            s_h = s_ref[h * p:(h + 1) * p, :]
            y_h = y_h + _bdot_nt(cg, s_h) * e_acs[:, h:h + 1]
            y_h = y_h + dsk[:, h:h + 1] * xs_h
            y_parts.append(y_h)
            xdtd.append(xdt_h * dec_end[:, h:h + 1])
            dec_rows.append(jnp.broadcast_to(e_last[:, h:h + 1], (p, n)))
        st_g = _bdot_tn(jnp.concatenate(xdtd, axis=1), bg)
        rows = slice(g * gw, (g + 1) * gw)
        s_ref[rows, :] = s_ref[rows, :] * jnp.concatenate(dec_rows, axis=0) + st_g
    so_ref[0] = s_ref[...]

    y = jnp.concatenate(y_parts, axis=1) * _silu(z_ref[...])
    norm_parts = []
    for g in range(SSM_GROUPS):
        yg = y[:, g * gw:(g + 1) * gw]
        norm_parts.append(yg * lax.rsqrt(jnp.mean(yg * yg, axis=-1, keepdims=True) + SSM_NORM_EPS))
    yn = (jnp.concatenate(norm_parts, axis=1) * nw_ref[...]).astype(BF16)
    o_ref[...] = x_ref[...] + jnp.dot(yn, wo_ref[...], preferred_element_type=F32)


def _ssd_mixer(x, z, xbc, dt, conv_state, s0, cw, cb, dtb, alog, dsk, nw, wo, *, batch, seq, q):
    d = x.shape[1]
    d_ssm = z.shape[1]
    cdim = xbc.shape[1]
    nchunk = seq // q
    assert seq % q == 0
    zero_state = s0 is None
    if zero_state:
        s0 = jnp.zeros((1, d_ssm, SSM_STATE), F32)
    row_map = lambda b, c: (b * nchunk + c, 0)
    seq_map = lambda b, c: (b, 0, 0)
    s0_map = (lambda b, c: (0, 0, 0)) if zero_state else seq_map
    return pl.pallas_call(
        functools.partial(_ssd_kernel, q=q, zero_state=zero_state),
        out_shape=[jax.ShapeDtypeStruct((batch * seq, d), F32),
                   jax.ShapeDtypeStruct((batch, CONV_PAD, cdim), F32),
                   jax.ShapeDtypeStruct((batch, d_ssm, SSM_STATE), F32)],
        grid=(batch, nchunk),
        in_specs=[pl.BlockSpec((q, d_ssm), row_map), pl.BlockSpec((q, cdim), row_map),
                  pl.BlockSpec((q, V7X_LANES), row_map), pl.BlockSpec((q, d), row_map),
                  pl.BlockSpec((1, CONV_PAD, cdim), seq_map), pl.BlockSpec((1, d_ssm, SSM_STATE), s0_map),
                  _const_spec(cw.shape), _const_spec(cb.shape), _const_spec(dtb.shape), _const_spec(alog.shape),
                  _const_spec(dsk.shape), _const_spec(nw.shape), _const_spec(wo.shape)],
        out_specs=[pl.BlockSpec((q, d), row_map), pl.BlockSpec((1, CONV_PAD, cdim), seq_map),
                   pl.BlockSpec((1, d_ssm, SSM_STATE), seq_map)],
        scratch_shapes=[pltpu.VMEM((CONV_PAD + q, cdim), F32), pltpu.VMEM((d_ssm, SSM_STATE), F32)],
        compiler_params=_params(("parallel", "arbitrary")),
        name="ssd_mixer",
    )(z, xbc, dt, x, conv_state, s0, cw, cb, dtb, alog, dsk, nw, wo)


def _rwkv_proj_kernel(x_ref, sh_ref, g_ref, mu_ref, wrkv_ref, w0_ref, w1_ref, w2_ref, a0_ref, a1_ref, a2_ref,
                      g1_ref, g2_ref, kkw_ref, kaw_ref,
                      r_o, k_o, v_o, kk_o, bb_o, lw_o, g_o, sho_ref, carry_ref, *, nb, tl):
    c = pl.program_id(1)
    rows = nb * tl
    d = x_ref.shape[1]

    @pl.when(c == 0)
    def _():
        carry_ref[...] = sh_ref[...]

    u = _rms(x_ref[...], g_ref[...])
    t_in = _row_index(rows, tl)
    prev = jnp.where(t_in == 0, jnp.broadcast_to(carry_ref[...], (nb, tl, d)).reshape(rows, d),
                     pltpu.roll(u, 1, axis=0))
    last = u.reshape(nb, tl, d)[:, tl - 1:tl, :]
    carry_ref[...] = last
    sho_ref[...] = last

    diff = prev - u
    mu = mu_ref[...]

    def mixed(s):
        return (u + diff * mu[s:s + 1, :]).astype(BF16)

    r = jnp.dot(mixed(0), wrkv_ref[0], preferred_element_type=F32)
    k = jnp.dot(mixed(1), wrkv_ref[1], preferred_element_type=F32)
    v_o[...] = jnp.dot(mixed(2), wrkv_ref[2], preferred_element_type=F32)
    w_pre = w0_ref[...] + _bdot(jnp.tanh(jnp.dot(mixed(3), w1_ref[...], preferred_element_type=F32)), w2_ref[...])
    lw_o[...] = -jnp.exp(-_softplus(-w_pre) - 0.5)
    a = _sigmoid(a0_ref[...] + _bdot(jnp.dot(mixed(4), a1_ref[...], preferred_element_type=F32), a2_ref[...]))
    g_o[...] = _bdot(_sigmoid(jnp.dot(mixed(5), g1_ref[...], preferred_element_type=F32)), g2_ref[...])
    kk = k * kkw_ref[...]
    kk = kk / jnp.maximum(jnp.sqrt(_head_sum(kk * kk, RWKV_HEAD)), 1e-12)
    r_o[...] = r
    k_o[...] = k * (1.0 + (a - 1.0) * kaw_ref[...])
    kk_o[...] = kk
    bb_o[...] = kk * a


def _rwkv_proj(x, shift0, g, mu, wrkv, w0, w1, w2, a0, a1, a2, g1, g2, kkw, kaw, *, batch, seq, nb, tl):
    d = x.shape[1]
    nchunk = seq // tl
    assert batch % nb == 0 and seq % tl == 0 and (nb == 1 or nchunk == 1)
    rows = nb * tl
    row_map = lambda i, c: (i * nchunk + c, 0)
    seq_map = lambda i, c: (i, 0, 0)
    consts = (g, mu, wrkv, w0, w1, w2, a0, a1, a2, g1, g2, kkw, kaw)
    return pl.pallas_call(
        functools.partial(_rwkv_proj_kernel, nb=nb, tl=tl),
        out_shape=[jax.ShapeDtypeStruct((batch * seq, d), F32)] * 7 + [jax.ShapeDtypeStruct((batch, 1, d), F32)],
        grid=(batch // nb, nchunk),
        in_specs=[pl.BlockSpec((rows, d), row_map), pl.BlockSpec((nb, 1, d), seq_map)]
        + [_const_spec(w.shape) for w in consts],
        out_specs=[pl.BlockSpec((rows, d), row_map)] * 7 + [pl.BlockSpec((nb, 1, d), seq_map)],
        scratch_shapes=[pltpu.VMEM((nb, 1, d), F32)],
        compiler_params=_params(("parallel", "arbitrary")),
        name="rwkv_proj",
    )(x, shift0, *consts)


def _unit_lower_solve(a_low, w, size):
    nmat = -a_low
    u = w + _bdot(nmat, w)
    m = 2
    while m < size:
        nmat = _bdot(nmat, nmat)
        u = u + _bdot(nmat, u)
        m *= 2
    return u


def _rwkv_chunk_kernel(r_ref, k_ref, v_ref, kk_ref, bb_ref, lw_ref, g_ref, x_ref, s0_ref, rk_ref, lnw_ref, lnb_ref,
                       wo_ref, o_ref, so_ref, s_ref, og_ref, *, nb, tl, chunk, zero_state):
    c = pl.program_id(1)
    d = x_ref.shape[1]
    hd = RWKV_HEAD
    heads = d // hd
    cpb = tl // chunk
    n_iter = nb * cpb

    @pl.when(c == 0)
    def _():
        if zero_state:
            s_ref[...] = jnp.zeros(s_ref.shape, F32)
        else:
            s_ref[...] = s0_ref[...]

    t_in = _row_index(chunk, chunk)
    ri = lax.broadcasted_iota(jnp.int32, (chunk, chunk), 0)
    ci = lax.broadcasted_iota(jnp.int32, (chunk, chunk), 1)
    strict = ri > ci
    incl = ri >= ci

    def body(it, carry):
        seq_i = it if cpb == 1 else 0
        rows = pl.ds(pl.multiple_of(it * chunk, chunk), chunk)
        r = r_ref[rows, :]
        k = k_ref[rows, :]
        v = v_ref[rows, :]
        kk = kk_ref[rows, :]
        bb = bb_ref[rows, :]
        lw = lw_ref[rows, :]
        cum = _cumsum_rows(lw, t_in, chunk)
        c_last = cum[chunk - 1:chunk, :]
        g_last = jnp.exp(c_last)
        e_inv = jnp.exp(-cum)
        e_end = jnp.exp(c_last - cum)
        kkd = (kk * jnp.exp(cum - lw)).astype(BF16)
        rd = (r * jnp.exp(cum)).astype(BF16)
        ki = (k * e_inv).astype(BF16)
        bi = (bb * e_inv).astype(BF16)
        kg = (k * e_end).astype(BF16)
        bg = (bb * e_end).astype(BF16)
        o_parts = []
        for h in range(heads):
            sl = slice(h * hd, (h + 1) * hd)
            s_h = s_ref[seq_i, sl, :]
            s_hb = s_h.astype(BF16)
            v_h = v[:, sl]
            a_k = jnp.where(strict, _bdot_nt(kkd[:, sl], ki[:, sl]), 0.0)
            a_b = jnp.where(strict, _bdot_nt(kkd[:, sl], bi[:, sl]), 0.0)
            w_h = _bdot_nt(kkd[:, sl], s_hb) + _bdot(a_k, v_h)
            u_h = _unit_lower_solve(a_b, w_h, chunk)
            r_k = jnp.where(incl, _bdot_nt(rd[:, sl], ki[:, sl]), 0.0)
            r_b = jnp.where(incl, _bdot_nt(rd[:, sl], bi[:, sl]), 0.0)
            o_parts.append(_bdot_nt(rd[:, sl], s_hb) + _bdot(r_k, v_h) - _bdot(r_b, u_h))
            s_ref[seq_i, sl, :] = s_h * g_last[:, sl] + _bdot_tn(v_h, kg[:, sl]) - _bdot_tn(u_h, bg[:, sl])
        o = jnp.concatenate(o_parts, axis=1)
        mean = _head_sum(o, hd) * (1.0 / hd)
        oc = o - mean
        var = _head_sum(oc * oc, hd) * (1.0 / hd)
        on = oc * lax.rsqrt(var + GN_EPS) * lnw_ref[...] + lnb_ref[...]
        bonus = _head_sum(r * k * rk_ref[...], hd) * v
        og_ref[rows, :] = (on + bonus) * g_ref[rows, :]
        return carry

    lax.fori_loop(0, n_iter, body, 0)
    so_ref[...] = s_ref[...]
    o_ref[...] = x_ref[...] + _bdot(og_ref[...], wo_ref[...])


def _rwkv_mixer(x, r, k, v, kk, bb, lw, g, s0, rk, lnw, lnb, wo, *, batch, seq, nb, tl, chunk):
    d = x.shape[1]
    nchunk = seq // tl
    assert batch % nb == 0 and seq % tl == 0 and tl % chunk == 0 and (nb == 1 or tl == chunk)
    rows = nb * tl
    zero_state = s0 is None
    if zero_state:
        s0 = jnp.zeros((nb, d, RWKV_HEAD), F32)
    row_map = lambda i, c: (i * nchunk + c, 0)
    seq_map = lambda i, c: (i, 0, 0)
    s0_map = (lambda i, c: (0, 0, 0)) if zero_state else seq_map
    row_spec = pl.BlockSpec((rows, d), row_map)
    return pl.pallas_call(
        functools.partial(_rwkv_chunk_kernel, nb=nb, tl=tl, chunk=chunk, zero_state=zero_state),
        out_shape=[jax.ShapeDtypeStruct((batch * seq, d), F32), jax.ShapeDtypeStruct((batch, d, RWKV_HEAD), F32)],
        grid=(batch // nb, nchunk),
        in_specs=[row_spec] * 8 + [pl.BlockSpec((nb, d, RWKV_HEAD), s0_map),
                                   _const_spec(rk.shape), _const_spec(lnw.shape), _const_spec(lnb.shape),
                                   _const_spec(wo.shape)],
        out_specs=[row_spec, pl.BlockSpec((nb, d, RWKV_HEAD), seq_map)],
        scratch_shapes=[pltpu.VMEM((nb, d, RWKV_HEAD), F32), pltpu.VMEM((rows, d), F32)],
        compiler_params=_params(("parallel", "arbitrary")),
        name="rwkv_chunk",
    )(r, k, v, kk, bb, lw, g, x, s0, rk, lnw, lnb, wo)


def _pad_conv_state(state):
    return jnp.pad(state, ((0, 0), (CONV_PAD - (CONV_W - 1), 0), (0, 0)))


def _trunk(x3, states, start_pos, p, cfg):
    batch, seq, d = x3.shape
    x = x3.reshape(batch * seq, d)
    depth = p["norm_mix"].shape[0]
    new_lc, new_lh, new_sc, new_ss, new_rs, new_rw = [], [], [], [], [], []
    ia = ib = ic = 0
    for layer in range(depth):
        g_mix = p["norm_mix"][layer][None, :]
        kind = layer % 3
        if kind == 0:
            (xg,) = _norm_proj(x, g_mix, [p["lru_w_in"][ia]], tm=cfg["tm_proj"])
            cs = states["lru_conv"][ia] if states else jnp.zeros((batch, CONV_W - 1, d), F32)
            h0 = states["lru_h"][ia] if states else jnp.zeros((batch, d), F32)
            x, cso, ho = _lru_mixer(
                x, xg, _pad_conv_state(cs), h0[:, None, :],
                p["lru_conv_w"][ia], p["lru_conv_b"][ia][None, :], p["lru_w_r"][ia], p["lru_b_r"][ia][None, :],
                p["lru_w_i"][ia], p["lru_b_i"][ia][None, :], p["lru_lambda"][ia][None, :], p["lru_w_out"][ia],
                batch=batch, seq=seq, nb=cfg["lru_nb"], tl=cfg["lru_tl"], start_pos=start_pos)
            new_lc.append(cso[:, CONV_PAD - (CONV_W - 1):, :])
            new_lh.append(ho[:, 0, :])
            ia += 1
        elif kind == 1:
            z, xbc, dt = _norm_proj(x, g_mix, [p["ssm_w_z"][ib], p["ssm_w_xbc"][ib], p["ssm_w_dt"][ib]],
                                    tm=cfg["tm_proj"])
            cdim = xbc.shape[1]
            cs = states["ssm_conv"][ib] if states else jnp.zeros((batch, CONV_W - 1, cdim), F32)
            s0 = states["ssm"][ib].reshape(batch, -1, SSM_STATE) if states else None
            x, cso, so = _ssd_mixer(
                x, z, xbc, dt, _pad_conv_state(cs), s0,
                p["ssm_conv_w"][ib], p["ssm_conv_b"][ib][None, :], p["ssm_dt_bias"][ib], p["ssm_a_log"][ib],
                p["ssm_d"][ib], p["ssm_norm_w"][ib][None, :], p["ssm_w_out"][ib],
                batch=batch, seq=seq, q=cfg["ssd_q"])
            new_sc.append(cso[:, CONV_PAD - (CONV_W - 1):, :])
            new_ss.append(so.reshape(batch, -1, SSM_HEADDIM, SSM_STATE))
            ib += 1
        else:
            sh0 = states["rwkv_shift"][ic] if states else jnp.zeros((batch, d), F32)
            s0 = states["rwkv_wkv"][ic].reshape(batch, d, RWKV_HEAD) if states else None
            r, k, v, kk, bb, lw, g, sho = _rwkv_proj(
                x, sh0[:, None, :], g_mix, p["rwkv_mu"][ic], p["rwkv_w_rkv"][ic], p["rwkv_w0"][ic][None, :],
                p["rwkv_w_w1"][ic], p["rwkv_w_w2"][ic], p["rwkv_a0"][ic][None, :], p["rwkv_w_a1"][ic],
                p["rwkv_w_a2"][ic], p["rwkv_w_g1"][ic], p["rwkv_w_g2"][ic], p["rwkv_k_k"][ic][None, :],
                p["rwkv_k_a"][ic][None, :], batch=batch, seq=seq, nb=cfg["rwkv_proj_nb"], tl=cfg["rwkv_tl"])
            x, so = _rwkv_mixer(
                x, r, k, v, kk, bb, lw, g, s0, p["rwkv_r_k"][ic].reshape(1, d), p["rwkv_lnx_w"][ic][None, :],
                p["rwkv_lnx_b"][ic][None, :], p["rwkv_w_out"][ic],
                batch=batch, seq=seq, nb=cfg["rwkv_nb"], tl=cfg["rwkv_tl"], chunk=cfg["rwkv_chunk"])
            new_rs.append(sho[:, 0, :])
            new_rw.append(so.reshape(batch, d // RWKV_HEAD, RWKV_HEAD, RWKV_HEAD))
            ic += 1
        x = _ffn(x, p["norm_ffn"][layer][None, :], p["ffn_w1"][layer], p["ffn_w2"][layer],
                 p["norm_final"][None, :], tm=cfg["tm_ffn"], final_norm=(layer == depth - 1))
    return (x.reshape(batch, seq, d), jnp.stack(new_lc), jnp.stack(new_lh), jnp.stack(new_sc), jnp.stack(new_ss),
            jnp.stack(new_rs), jnp.stack(new_rw))


def _prep_params(p):
    q = dict(p)
    for name in ("lru_w_in", "lru_w_r", "lru_w_i", "lru_w_out", "ssm_w_out", "rwkv_w_rkv", "rwkv_w_w1", "rwkv_w_w2",
                 "rwkv_w_a1", "rwkv_w_a2", "rwkv_w_g1", "rwkv_w_g2", "rwkv_w_out", "ffn_w1", "ffn_w2"):
        q[name] = p[name].astype(BF16)
    d_ssm = p["ssm_w_out"].shape[1]
    heads = p["ssm_dt_bias"].shape[1]
    w_in = p["ssm_w_in"]
    cdim = w_in.shape[2] - d_ssm - heads
    pad = V7X_LANES - heads
    q["ssm_w_z"] = w_in[:, :, :d_ssm].astype(BF16)
    q["ssm_w_xbc"] = w_in[:, :, d_ssm:d_ssm + cdim].astype(BF16)
    q["ssm_w_dt"] = jnp.pad(w_in[:, :, d_ssm + cdim:], ((0, 0), (0, 0), (0, pad))).astype(BF16)
    for name in ("ssm_dt_bias", "ssm_a_log", "ssm_d"):
        q[name] = jnp.pad(p[name], ((0, 0), (0, pad)))[:, None, :]
    return q


def kernel(x_prompt, x_sample, state_lru_conv, state_lru_h, state_ssm_conv, state_ssm, state_rwkv_shift, state_rwkv_wkv, norm_mix, norm_ffn, norm_final, lru_w_in, lru_conv_w, lru_conv_b, lru_w_r, lru_b_r, lru_w_i, lru_b_i, lru_lambda, lru_w_out, ssm_w_in, ssm_conv_w, ssm_conv_b, ssm_dt_bias, ssm_a_log, ssm_d, ssm_norm_w, ssm_w_out, rwkv_mu, rwkv_w_rkv, rwkv_w0, rwkv_w_w1, rwkv_w_w2, rwkv_a0, rwkv_w_a1, rwkv_w_a2, rwkv_w_g1, rwkv_w_g2, rwkv_k_k, rwkv_k_a, rwkv_r_k, rwkv_lnx_w, rwkv_lnx_b, rwkv_w_out, ffn_w1, ffn_w2):
    p = _prep_params(dict(
        norm_mix=norm_mix, norm_ffn=norm_ffn, norm_final=norm_final,
        lru_w_in=lru_w_in, lru_conv_w=lru_conv_w, lru_conv_b=lru_conv_b, lru_w_r=lru_w_r, lru_b_r=lru_b_r,
        lru_w_i=lru_w_i, lru_b_i=lru_b_i, lru_lambda=lru_lambda, lru_w_out=lru_w_out,
        ssm_w_in=ssm_w_in, ssm_conv_w=ssm_conv_w, ssm_conv_b=ssm_conv_b, ssm_dt_bias=ssm_dt_bias,
        ssm_a_log=ssm_a_log, ssm_d=ssm_d, ssm_norm_w=ssm_norm_w, ssm_w_out=ssm_w_out,
        rwkv_mu=rwkv_mu, rwkv_w_rkv=rwkv_w_rkv, rwkv_w0=rwkv_w0, rwkv_w_w1=rwkv_w_w1, rwkv_w_w2=rwkv_w_w2,
        rwkv_a0=rwkv_a0, rwkv_w_a1=rwkv_w_a1, rwkv_w_a2=rwkv_w_a2, rwkv_w_g1=rwkv_w_g1, rwkv_w_g2=rwkv_w_g2,
        rwkv_k_k=rwkv_k_k, rwkv_k_a=rwkv_k_a, rwkv_r_k=rwkv_r_k, rwkv_lnx_w=rwkv_lnx_w, rwkv_lnx_b=rwkv_lnx_b,
        rwkv_w_out=rwkv_w_out, ffn_w1=ffn_w1, ffn_w2=ffn_w2))
    seq_p = x_prompt.shape[1]
    seq_s = x_sample.shape[1]
    cfg_p = dict(tm_proj=256, tm_ffn=512, lru_nb=1, lru_tl=256, ssd_q=min(SSM_CHUNK, seq_p),
                 rwkv_proj_nb=1, rwkv_nb=1, rwkv_tl=256, rwkv_chunk=RWKV_CHUNK)
    cfg_s = dict(tm_proj=256, tm_ffn=512, lru_nb=32, lru_tl=seq_s, ssd_q=seq_s,
                 rwkv_proj_nb=32, rwkv_nb=8, rwkv_tl=seq_s, rwkv_chunk=seq_s)
    y_p, lc_p, lh_p, sc_p, ss_p, rs_p, rw_p = _trunk(x_prompt, None, 0, p, cfg_p)
    states = dict(lru_conv=state_lru_conv, lru_h=state_lru_h, ssm_conv=state_ssm_conv, ssm=state_ssm,
                  rwkv_shift=state_rwkv_shift, rwkv_wkv=state_rwkv_wkv)
    y_s, lc_s, lh_s, sc_s, ss_s, rs_s, rw_s = _trunk(x_sample, states, PAST_LEN, p, cfg_s)
    return (y_p, y_s, lc_p, lc_s, lh_p, lh_s, sc_p, sc_s, ss_p, ss_s, rs_p, rs_s, rw_p, rw_s)
```

```python
import functools
import math

import jax
import jax.numpy as jnp
from jax import lax
from jax.experimental import pallas as pl
from jax.experimental.pallas import tpu as pltpu

F32 = jnp.float32
BF16 = jnp.bfloat16

NORM_EPS = 1e-6
CONV_W = 4
LRU_BLOCKS = 8
LRU_C = 8.0
SSM_HEADDIM = 64
SSM_GROUPS = 8
SSM_STATE = 128
SSM_CHUNK = 128
SSM_NORM_EPS = 1e-5
RWKV_HEAD = 64
GN_EPS = 64e-5
PAST_LEN = 16384

V7X_LANES = 128
V7X_SUBLANES = 8
V7X_VMEM_BYTES = 64 * 1024 * 1024
VMEM_LIMIT = V7X_VMEM_BYTES - 8 * 1024 * 1024

RWKV_CHUNK = 64
CONV_PAD = V7X_SUBLANES


def _bdot(a, b):
    return jnp.dot(a.astype(BF16), b.astype(BF16), preferred_element_type=F32)


def _bdot_nt(a, b):
    return lax.dot_general(a.astype(BF16), b.astype(BF16), (((1,), (1,)), ((), ())),
                           preferred_element_type=F32)


def _bdot_tn(a, b):
    return lax.dot_general(a.astype(BF16), b.astype(BF16), (((0,), (0,)), ((), ())),
                           preferred_element_type=F32)


def _rms(x, g):
    return x * lax.rsqrt(jnp.mean(x * x, axis=-1, keepdims=True) + NORM_EPS) * g


def _sigmoid(x):
    return 1.0 / (1.0 + jnp.exp(-x))


def _softplus(x):
    return jnp.maximum(x, 0.0) + jnp.log1p(jnp.exp(-jnp.abs(x)))


def _silu(x):
    return x * _sigmoid(x)


def _gelu_tanh(x):
    c = math.sqrt(2.0 / math.pi)
    return 0.5 * x * (1.0 + jnp.tanh(c * (x + 0.044715 * (x * x * x))))


def _row_index(rows, period):
    return lax.broadcasted_iota(jnp.int32, (rows, 1), 0) & (period - 1)


def _cumsum_rows(x, t_in, period):
    d = 1
    while d < period:
        x = x + jnp.where(t_in >= d, pltpu.roll(x, d, axis=0), 0.0)
        d *= 2
    return x


def _head_sum(x, head):
    rows, width = x.shape
    lo = lax.broadcasted_iota(jnp.int32, (1, V7X_LANES), 1) < head
    parts = []
    for j in range(width // V7X_LANES):
        seg = x[:, j * V7X_LANES:(j + 1) * V7X_LANES]
        s_lo = jnp.sum(jnp.where(lo, seg, 0.0), axis=-1, keepdims=True)
        s_hi = jnp.sum(jnp.where(lo, 0.0, seg), axis=-1, keepdims=True)
        parts.append(jnp.where(lo, s_lo, s_hi))
    return jnp.concatenate(parts, axis=1)


def _const_spec(shape):
    nd = len(shape)
    return pl.BlockSpec(shape, lambda *_: (0,) * nd, pipeline_mode=pl.Buffered(1))


def _params(sem):
    return pltpu.CompilerParams(dimension_semantics=sem, vmem_limit_bytes=VMEM_LIMIT)


def _proj_kernel(x_ref, g_ref, *refs, n_out, col_chunk):
    w_refs, o_refs = refs[:n_out], refs[n_out:]
    u = _rms(x_ref[...], g_ref[...]).astype(BF16)
    for w_ref, o_ref in zip(w_refs, o_refs):
        n = w_ref.shape[1]
        step = min(col_chunk, n)
        for c0 in range(0, n, step):
            o_ref[:, c0:c0 + step] = jnp.dot(u, w_ref[:, c0:c0 + step], preferred_element_type=F32)


def _norm_proj(x, g, weights, *, tm):
    t, d = x.shape
    assert t % tm == 0
    n_out = len(weights)
    return pl.pallas_call(
        functools.partial(_proj_kernel, n_out=n_out, col_chunk=1024),
        out_shape=[jax.ShapeDtypeStruct((t, w.shape[1]), F32) for w in weights],
        grid=(t // tm,),
        in_specs=[pl.BlockSpec((tm, d), lambda i: (i, 0)), _const_spec((1, d))]
        + [_const_spec(w.shape) for w in weights],
        out_specs=[pl.BlockSpec((tm, w.shape[1]), lambda i: (i, 0)) for w in weights],
        compiler_params=_params(("parallel",)),
        name="norm_proj",
    )(x, g, *weights)


def _ffn_kernel(x_ref, g_ref, w1_ref, w2_ref, gf_ref, o_ref, *, f_chunk, final_norm):
    x = x_ref[...]
    u = _rms(x, g_ref[...]).astype(BF16)
    acc = x
    for c0 in range(0, w1_ref.shape[1], f_chunk):
        h = jnp.dot(u, w1_ref[:, c0:c0 + f_chunk], preferred_element_type=F32)
        h = jnp.square(jnp.maximum(h, 0.0)).astype(BF16)
        acc = acc + jnp.dot(h, w2_ref[c0:c0 + f_chunk, :], preferred_element_type=F32)
    if final_norm:
        acc = _rms(acc, gf_ref[...])
    o_ref[...] = acc


def _ffn(x, g, w1, w2, g_final, *, tm, final_norm):
    t, d = x.shape
    f = w1.shape[1]
    assert t % tm == 0
    return pl.pallas_call(
        functools.partial(_ffn_kernel, f_chunk=1024, final_norm=final_norm),
        out_shape=jax.ShapeDtypeStruct((t, d), F32),
        grid=(t // tm,),
        in_specs=[pl.BlockSpec((tm, d), lambda i: (i, 0)), _const_spec((1, d)),
                  _const_spec((d, f)), _const_spec((f, d)), _const_spec((1, d))],
        out_specs=pl.BlockSpec((tm, d), lambda i: (i, 0)),
        compiler_params=_params(("parallel",)),
        name="ffn",
    )(x, g, w1, w2, g_final)


def _conv_taps(xe_ref, cw, cb, tl):
    nd = len(xe_ref.shape)

    def win(off):
        if nd == 3:
            return xe_ref[:, off:off + tl, :]
        return xe_ref[off:off + tl, :]

    base = CONV_PAD - (CONV_W - 1)
    y = cb + win(base) * cw[0:1, :]
    for k in range(1, CONV_W):
        y = y + win(base + k) * cw[k:k + 1, :]
    return y


def _lru_kernel(xg_ref, x_ref, cs_ref, h0_ref, cw_ref, cb_ref, wr_ref, br_ref, wi_ref, bi_ref, lam_ref, wo_ref,
                o_ref, cso_ref, ho_ref, xe_ref, hc_ref, *, nb, tl, start_pos):
    c = pl.program_id(1)
    rows = nb * tl
    d = x_ref.shape[1]
    blk = d // LRU_BLOCKS

    @pl.when(c == 0)
    def _():
        xe_ref[:, 0:CONV_PAD, :] = cs_ref[...]
        hc_ref[...] = h0_ref[...]

    xe_ref[:, CONV_PAD:CONV_PAD + tl, :] = xg_ref[:, 0:d].reshape(nb, tl, d)
    xc = _conv_taps(xe_ref, cw_ref[...], cb_ref[...], tl).reshape(rows, d)
    tail = xe_ref[:, tl:tl + CONV_PAD, :]
    xe_ref[:, 0:CONV_PAD, :] = tail
    cso_ref[...] = tail

    xcb = xc.astype(BF16)
    r_pre = jnp.concatenate(
        [jnp.dot(xcb[:, j * blk:(j + 1) * blk], wr_ref[j], preferred_element_type=F32) for j in range(LRU_BLOCKS)],
        axis=1) + br_ref[...]
    i_pre = jnp.concatenate(
        [jnp.dot(xcb[:, j * blk:(j + 1) * blk], wi_ref[j], preferred_element_type=F32) for j in range(LRU_BLOCKS)],
        axis=1) + bi_ref[...]
    r = _sigmoid(r_pre)
    gi = _sigmoid(i_pre)
    log_a = (-LRU_C) * r * _softplus(-lam_ref[...])
    a = jnp.exp(log_a)
    mult = jnp.sqrt(-jnp.tanh(log_a) * (a * a + 1.0))
    t_in = _row_index(rows, tl)
    if start_pos == 0:
        reset = (c * tl + t_in) == 0
        a = jnp.where(reset, 0.0, a)
        mult = jnp.where(reset, 1.0, mult)
    b = mult * (gi * xc)
    h_prev = jnp.broadcast_to(hc_ref[...], (nb, tl, d)).reshape(rows, d)
    b = b + jnp.where(t_in == 0, a * h_prev, 0.0)

    step = 1
    while step < tl:
        m = t_in >= step
        b = jnp.where(m, a * pltpu.roll(b, step, axis=0) + b, b)
        if 2 * step < tl:
            a = jnp.where(m, a * pltpu.roll(a, step, axis=0), a)
        step *= 2
    h_last = b.reshape(nb, tl, d)[:, tl - 1:tl, :]
    hc_ref[...] = h_last
    ho_ref[...] = h_last

    y = (b * _gelu_tanh(xg_ref[:, d:2 * d])).astype(BF16)
    o_ref[...] = x_ref[...] + jnp.dot(y, wo_ref[...], preferred_element_type=F32)


def _lru_mixer(x, xg, conv_state, h0, cw, cb, wr, br, wi, bi, lam, wo, *, batch, seq, nb, tl, start_pos):
    d = x.shape[1]
    nchunk = seq // tl
    assert batch % nb == 0 and seq % tl == 0 and (nb == 1 or nchunk == 1)
    rows = nb * tl
    row_map = lambda i, c: (i * nchunk + c, 0)
    seq_map = lambda i, c: (i, 0, 0)
    return pl.pallas_call(
        functools.partial(_lru_kernel, nb=nb, tl=tl, start_pos=start_pos),
        out_shape=[jax.ShapeDtypeStruct((batch * seq, d), F32),
                   jax.ShapeDtypeStruct((batch, CONV_PAD, d), F32),
                   jax.ShapeDtypeStruct((batch, 1, d), F32)],
        grid=(batch // nb, nchunk),
        in_specs=[pl.BlockSpec((rows, 2 * d), row_map), pl.BlockSpec((rows, d), row_map),
                  pl.BlockSpec((nb, CONV_PAD, d), seq_map), pl.BlockSpec((nb, 1, d), seq_map),
                  _const_spec(cw.shape), _const_spec(cb.shape), _const_spec(wr.shape), _const_spec(br.shape),
                  _const_spec(wi.shape), _const_spec(bi.shape), _const_spec(lam.shape), _const_spec(wo.shape)],
        out_specs=[pl.BlockSpec((rows, d), row_map), pl.BlockSpec((nb, CONV_PAD, d), seq_map),
                   pl.BlockSpec((nb, 1, d), seq_map)],
        scratch_shapes=[pltpu.VMEM((nb, CONV_PAD + tl, d), F32), pltpu.VMEM((nb, 1, d), F32)],
        compiler_params=_params(("parallel", "arbitrary")),
        name="lru_mixer",
    )(xg, x, conv_state, h0, cw, cb, wr, br, wi, bi, lam, wo)


def _ssd_kernel(z_ref, xbc_ref, dt_ref, x_ref, cs_ref, s0_ref, cw_ref, cb_ref, dtb_ref, alog_ref, dsk_ref, nw_ref,
                wo_ref, o_ref, cso_ref, so_ref, xe_ref, s_ref, *, q, zero_state):
    c = pl.program_id(1)
    d_ssm = z_ref.shape[1]
    n = SSM_STATE
    p = SSM_HEADDIM
    hpg = d_ssm // p // SSM_GROUPS
    gw = hpg * p

    @pl.when(c == 0)
    def _():
        xe_ref[0:CONV_PAD, :] = cs_ref[0]
        if zero_state:
            s_ref[...] = jnp.zeros(s_ref.shape, F32)
        else:
            s_ref[...] = s0_ref[0]

    xe_ref[CONV_PAD:CONV_PAD + q, :] = xbc_ref[...]
    xbc = _silu(_conv_taps(xe_ref, cw_ref[...], cb_ref[...], q))
    tail = xe_ref[q:q + CONV_PAD, :]
    xe_ref[0:CONV_PAD, :] = tail
    cso_ref[0] = tail

    xs = xbc[:, 0:d_ssm]
    bm = xbc[:, d_ssm:d_ssm + SSM_GROUPS * n]
    cm = xbc[:, d_ssm + SSM_GROUPS * n:]
    dt = _softplus(dt_ref[...] + dtb_ref[...])
    da = dt * (-jnp.exp(alog_ref[...]))
    t_in = _row_index(q, q)
    acs = _cumsum_rows(da, t_in, q)
    if q < V7X_LANES:
        acs_sq = jnp.concatenate([acs, jnp.zeros((V7X_LANES - q, V7X_LANES), F32)], axis=0)
    else:
        acs_sq = acs
    acs_t = acs_sq.T
    acs_last = acs[q - 1:q, :]
    dec_end = jnp.exp(acs_last - acs)
    e_acs = jnp.exp(acs)
    e_last = jnp.exp(acs_last)
    tri = lax.broadcasted_iota(jnp.int32, (q, q), 0) >= lax.broadcasted_iota(jnp.int32, (q, q), 1)
    dsk = dsk_ref[...]

    y_parts = []
    for g in range(SSM_GROUPS):
        bg = bm[:, g * n:(g + 1) * n].astype(BF16)
        cg = cm[:, g * n:(g + 1) * n].astype(BF16)
        cb_mat = _bdot_nt(cg, bg)
        xdtd = []
        dec_rows = []
        for e in range(hpg):
            h = g * hpg + e
            xs_h = xs[:, h * p:(h + 1) * p]
            xdt_h = xs_h * dt[:, h:h + 1]
            seg = acs[:, h:h + 1] - acs_t[h:h + 1, 0:q]
            l_mat = jnp.where(tri, jnp.exp(jnp.minimum(seg, 0.0)), 0.0)
            y_h = _bdot(cb_mat * l_mat, xdt_h)
            s_h = s_ref[h * p:(h + 1) * p, :]
            y_h = y_h + _bdot_nt(cg, s_h) * e_acs[:, h:h + 1]
            y_h = y_h + dsk[:, h:h + 1] * xs_h
            y_parts.append(y_h)
            xdtd.append(xdt_h * dec_end[:, h:h + 1])
            dec_rows.append(jnp.broadcast_to(e_last[:, h:h + 1], (p, n)))
        st_g = _bdot_tn(jnp.concatenate(xdtd, axis=1), bg)
        rows = slice(g * gw, (g + 1) * gw)
        s_ref[rows, :] = s_ref[rows, :] * jnp.concatenate(dec_rows, axis=0) + st_g
    so_ref[0] = s_ref[...]

    y = jnp.concatenate(y_parts, axis=1) * _silu(z_ref[...])
    norm_parts = []
    for g in range(SSM_GROUPS):
        yg = y[:, g * gw:(g + 1) * gw]
        norm_parts.append(yg * lax.rsqrt(jnp.mean(yg * yg, axis=-1, keepdims=True) + SSM_NORM_EPS))
    yn = (jnp.concatenate(norm_parts, axis=1) * nw_ref[...]).astype(BF16)
    o_ref[...] = x_ref[...] + jnp.dot(yn, wo_ref[...], preferred_element_type=F32)


def _ssd_mixer(x, z, xbc, dt, conv_state, s0, cw, cb, dtb, alog, dsk, nw, wo, *, batch, seq, q):
    d = x.shape[1]
    d_ssm = z.shape[1]
    cdim = xbc.shape[1]
    nchunk = seq // q
    assert seq % q == 0
    zero_state = s0 is None
    if zero_state:
        s0 = jnp.zeros((1, d_ssm, SSM_STATE), F32)
    row_map = lambda b, c: (b * nchunk + c, 0)
    seq_map = lambda b, c: (b, 0, 0)
    s0_map = (lambda b, c: (0, 0, 0)) if zero_state else seq_map
    return pl.pallas_call(
        functools.partial(_ssd_kernel, q=q, zero_state=zero_state),
        out_shape=[jax.ShapeDtypeStruct((batch * seq, d), F32),
                   jax.ShapeDtypeStruct((batch, CONV_PAD, cdim), F32),
                   jax.ShapeDtypeStruct((batch, d_ssm, SSM_STATE), F32)],
        grid=(batch, nchunk),
        in_specs=[pl.BlockSpec((q, d_ssm), row_map), pl.BlockSpec((q, cdim), row_map),
                  pl.BlockSpec((q, V7X_LANES), row_map), pl.BlockSpec((q, d), row_map),
                  pl.BlockSpec((1, CONV_PAD, cdim), seq_map), pl.BlockSpec((1, d_ssm, SSM_STATE), s0_map),
                  _const_spec(cw.shape), _const_spec(cb.shape), _const_spec(dtb.shape), _const_spec(alog.shape),
                  _const_spec(dsk.shape), _const_spec(nw.shape), _const_spec(wo.shape)],
        out_specs=[pl.BlockSpec((q, d), row_map), pl.BlockSpec((1, CONV_PAD, cdim), seq_map),
                   pl.BlockSpec((1, d_ssm, SSM_STATE), seq_map)],
        scratch_shapes=[pltpu.VMEM((CONV_PAD + q, cdim), F32), pltpu.VMEM((d_ssm, SSM_STATE), F32)],
        compiler_params=_params(("parallel", "arbitrary")),
        name="ssd_mixer",
    )(z, xbc, dt, x, conv_state, s0, cw, cb, dtb, alog, dsk, nw, wo)


def _rwkv_proj_kernel(x_ref, sh_ref, g_ref, mu_ref, wrkv_ref, w0_ref, w1_ref, w2_ref, a0_ref, a1_ref, a2_ref,
                      g1_ref, g2_ref, kkw_ref, kaw_ref,
                      r_o, k_o, v_o, kk_o, bb_o, lw_o, g_o, sho_ref, carry_ref, *, nb, tl):
    c = pl.program_id(1)
    rows = nb * tl
    d = x_ref.shape[1]

    @pl.when(c == 0)
    def _():
        carry_ref[...] = sh_ref[...]

    u = _rms(x_ref[...], g_ref[...])
    t_in = _row_index(rows, tl)
    prev = jnp.where(t_in == 0, jnp.broadcast_to(carry_ref[...], (nb, tl, d)).reshape(rows, d),
                     pltpu.roll(u, 1, axis=0))
    last = u.reshape(nb, tl, d)[:, tl - 1:tl, :]
    carry_ref[...] = last
    sho_ref[...] = last

    diff = prev - u
    mu = mu_ref[...]

    def mixed(s):
        return (u + diff * mu[s:s + 1, :]).astype(BF16)

    r = jnp.dot(mixed(0), wrkv_ref[0], preferred_element_type=F32)
    k = jnp.dot(mixed(1), wrkv_ref[1], preferred_element_type=F32)
    v_o[...] = jnp.dot(mixed(2), wrkv_ref[2], preferred_element_type=F32)
    w_pre = w0_ref[...] + _bdot(jnp.tanh(jnp.dot(mixed(3), w1_ref[...], preferred_element_type=F32)), w2_ref[...])
    lw_o[...] = -jnp.exp(-_softplus(-w_pre) - 0.5)
    a = _sigmoid(a0_ref[...] + _bdot(jnp.dot(mixed(4), a1_ref[...], preferred_element_type=F32), a2_ref[...]))
    g_o[...] = _bdot(_sigmoid(jnp.dot(mixed(5), g1_ref[...], preferred_element_type=F32)), g2_ref[...])
    kk = k * kkw_ref[...]
    kk = kk / jnp.maximum(jnp.sqrt(_head_sum(kk * kk, RWKV_HEAD)), 1e-12)
    r_o[...] = r
    k_o[...] = k * (1.0 + (a - 1.0) * kaw_ref[...])
    kk_o[...] = kk
    bb_o[...] = kk * a


def _rwkv_proj(x, shift0, g, mu, wrkv, w0, w1, w2, a0, a1, a2, g1, g2, kkw, kaw, *, batch, seq, nb, tl):
    d = x.shape[1]
    nchunk = seq // tl
    assert batch % nb == 0 and seq % tl == 0 and (nb == 1 or nchunk == 1)
    rows = nb * tl
    row_map = lambda i, c: (i * nchunk + c, 0)
    seq_map = lambda i, c: (i, 0, 0)
    consts = (g, mu, wrkv, w0, w1, w2, a0, a1, a2, g1, g2, kkw, kaw)
    return pl.pallas_call(
        functools.partial(_rwkv_proj_kernel, nb=nb, tl=tl),
        out_shape=[jax.ShapeDtypeStruct((batch * seq, d), F32)] * 7 + [jax.ShapeDtypeStruct((batch, 1, d), F32)],
        grid=(batch // nb, nchunk),
        in_specs=[pl.BlockSpec((rows, d), row_map), pl.BlockSpec((nb, 1, d), seq_map)]
        + [_const_spec(w.shape) for w in consts],
        out_specs=[pl.BlockSpec((rows, d), row_map)] * 7 + [pl.BlockSpec((nb, 1, d), seq_map)],
        scratch_shapes=[pltpu.VMEM((nb, 1, d), F32)],
        compiler_params=_params(("parallel", "arbitrary")),
        name="rwkv_proj",
    )(x, shift0, *consts)


def _unit_lower_solve(a_low, w, size):
    nmat = -a_low
    u = w + _bdot(nmat, w)
    m = 2
    while m < size:
        nmat = _bdot(nmat, nmat)
        u = u + _bdot(nmat, u)
        m *= 2
    return u


def _rwkv_chunk_kernel(r_ref, k_ref, v_ref, kk_ref, bb_ref, lw_ref, g_ref, x_ref, s0_ref, rk_ref, lnw_ref, lnb_ref,
                       wo_ref, o_ref, so_ref, s_ref, og_ref, *, nb, tl, chunk, zero_state):
    c = pl.program_id(1)
    d = x_ref.shape[1]
    hd = RWKV_HEAD
    heads = d // hd
    cpb = tl // chunk
    n_iter = nb * cpb

    @pl.when(c == 0)
    def _():
        if zero_state:
            s_ref[...] = jnp.zeros(s_ref.shape, F32)
        else:
            s_ref[...] = s0_ref[...]

    t_in = _row_index(chunk, chunk)
    ri = lax.broadcasted_iota(jnp.int32, (chunk, chunk), 0)
    ci = lax.broadcasted_iota(jnp.int32, (chunk, chunk), 1)
    strict = ri > ci
    incl = ri >= ci

    def body(it, carry):
        seq_i = it if cpb == 1 else 0
        rows = pl.ds(pl.multiple_of(it * chunk, chunk), chunk)
        r = r_ref[rows, :]
        k = k_ref[rows, :]
        v = v_ref[rows, :]
        kk = kk_ref[rows, :]
        bb = bb_ref[rows, :]
        lw = lw_ref[rows, :]
        cum = _cumsum_rows(lw, t_in, chunk)
        c_last = cum[chunk - 1:chunk, :]
        g_last = jnp.exp(c_last)
        e_inv = jnp.exp(-cum)
        e_end = jnp.exp(c_last - cum)
        kkd = (kk * jnp.exp(cum - lw)).astype(BF16)
        rd = (r * jnp.exp(cum)).astype(BF16)
        ki = (k * e_inv).astype(BF16)
        bi = (bb * e_inv).astype(BF16)
        kg = (k * e_end).astype(BF16)
        bg = (bb * e_end).astype(BF16)
        hs = range(heads)
        sls = [slice(h * hd, (h + 1) * hd) for h in hs]
        s_l = [s_ref[seq_i, sl, :] for sl in sls]
        s_b = [s.astype(BF16) for s in s_l]
        v_l = [v[:, sl].astype(BF16) for sl in sls]
        a_k = [jnp.where(strict, _bdot_nt(kkd[:, sl], ki[:, sl]), 0.0) for sl in sls]
        nmat = [jnp.where(strict, -_bdot_nt(kkd[:, sl], bi[:, sl]), 0.0) for sl in sls]
        w_l = [_bdot_nt(kkd[:, sls[h]], s_b[h]) + _bdot(a_k[h], v_l[h]) for h in hs]
        u_l = [w_l[h] + _bdot(nmat[h], w_l[h]) for h in hs]
        m = 2
        while m < chunk:
            nmat = [_bdot(x, x) for x in nmat]
            u_l = [u_l[h] + _bdot(nmat[h], u_l[h]) for h in hs]
            m *= 2
        r_k = [jnp.where(incl, _bdot_nt(rd[:, sl], ki[:, sl]), 0.0) for sl in sls]
        r_b = [jnp.where(incl, _bdot_nt(rd[:, sl], bi[:, sl]), 0.0) for sl in sls]
        o_parts = [_bdot_nt(rd[:, sls[h]], s_b[h]) + _bdot(r_k[h], v_l[h]) - _bdot(r_b[h], u_l[h]) for h in hs]
        for h in hs:
            s_ref[seq_i, sls[h], :] = (s_l[h] * g_last[:, sls[h]] + _bdot_tn(v_l[h], kg[:, sls[h]])
                                       - _bdot_tn(u_l[h], bg[:, sls[h]]))
        o = jnp.concatenate(o_parts, axis=1)
        mean = _head_sum(o, hd) * (1.0 / hd)
        oc = o - mean
        var = _head_sum(oc * oc, hd) * (1.0 / hd)
        on = oc * lax.rsqrt(var + GN_EPS) * lnw_ref[...] + lnb_ref[...]
        bonus = _head_sum(r * k * rk_ref[...], hd) * v
        og_ref[rows, :] = (on + bonus) * g_ref[rows, :]
        return carry

    lax.fori_loop(0, n_iter, body, 0)
    so_ref[...] = s_ref[...]
    o_ref[...] = x_ref[...] + _bdot(og_ref[...], wo_ref[...])


def _rwkv_mixer(x, r, k, v, kk, bb, lw, g, s0, rk, lnw, lnb, wo, *, batch, seq, nb, tl, chunk):
    d = x.shape[1]
    nchunk = seq // tl
    assert batch % nb == 0 and seq % tl == 0 and tl % chunk == 0 and (nb == 1 or tl == chunk)
    rows = nb * tl
    zero_state = s0 is None
    if zero_state:
        s0 = jnp.zeros((nb, d, RWKV_HEAD), F32)
    row_map = lambda i, c: (i * nchunk + c, 0)
    seq_map = lambda i, c: (i, 0, 0)
    s0_map = (lambda i, c: (0, 0, 0)) if zero_state else seq_map
    row_spec = pl.BlockSpec((rows, d), row_map)
    return pl.pallas_call(
        functools.partial(_rwkv_chunk_kernel, nb=nb, tl=tl, chunk=chunk, zero_state=zero_state),
        out_shape=[jax.ShapeDtypeStruct((batch * seq, d), F32), jax.ShapeDtypeStruct((batch, d, RWKV_HEAD), F32)],
        grid=(batch // nb, nchunk),
        in_specs=[row_spec] * 8 + [pl.BlockSpec((nb, d, RWKV_HEAD), s0_map),
                                   _const_spec(rk.shape), _const_spec(lnw.shape), _const_spec(lnb.shape),
                                   _const_spec(wo.shape)],
        out_specs=[row_spec, pl.BlockSpec((nb, d, RWKV_HEAD), seq_map)],
        scratch_shapes=[pltpu.VMEM((nb, d, RWKV_HEAD), F32), pltpu.VMEM((rows, d), F32)],
        compiler_params=_params(("parallel", "arbitrary")),
        name="rwkv_chunk",
    )(r, k, v, kk, bb, lw, g, x, s0, rk, lnw, lnb, wo)


def _pad_conv_state(state):
    return jnp.pad(state, ((0, 0), (CONV_PAD - (CONV_W - 1), 0), (0, 0)))


def _trunk(x3, states, start_pos, p, cfg):
    batch, seq, d = x3.shape
    x = x3.reshape(batch * seq, d)
    depth = p["norm_mix"].shape[0]
    new_lc, new_lh, new_sc, new_ss, new_rs, new_rw = [], [], [], [], [], []
    ia = ib = ic = 0
    for layer in range(depth):
        g_mix = p["norm_mix"][layer][None, :]
        kind = layer % 3
        if kind == 0:
            (xg,) = _norm_proj(x, g_mix, [p["lru_w_in"][ia]], tm=cfg["tm_proj"])
            cs = states["lru_conv"][ia] if states else jnp.zeros((batch, CONV_W - 1, d), F32)
            h0 = states["lru_h"][ia] if states else jnp.zeros((batch, d), F32)
            x, cso, ho = _lru_mixer(
                x, xg, _pad_conv_state(cs), h0[:, None, :],
                p["lru_conv_w"][ia], p["lru_conv_b"][ia][None, :], p["lru_w_r"][ia], p["lru_b_r"][ia][None, :],
                p["lru_w_i"][ia], p["lru_b_i"][ia][None, :], p["lru_lambda"][ia][None, :], p["lru_w_out"][ia],
                batch=batch, seq=seq, nb=cfg["lru_nb"], tl=cfg["lru_tl"], start_pos=start_pos)
            new_lc.append(cso[:, CONV_PAD - (CONV_W - 1):, :])
            new_lh.append(ho[:, 0, :])
            ia += 1
        elif kind == 1:
            z, xbc, dt = _norm_proj(x, g_mix, [p["ssm_w_z"][ib], p["ssm_w_xbc"][ib], p["ssm_w_dt"][ib]],
                                    tm=cfg["tm_proj"])
            cdim = xbc.shape[1]
            cs = states["ssm_conv"][ib] if states else jnp.zeros((batch, CONV_W - 1, cdim), F32)
            s0 = states["ssm"][ib].reshape(batch, -1, SSM_STATE) if states else None
            x, cso, so = _ssd_mixer(
                x, z, xbc, dt, _pad_conv_state(cs), s0,
                p["ssm_conv_w"][ib], p["ssm_conv_b"][ib][None, :], p["ssm_dt_bias"][ib], p["ssm_a_log"][ib],
                p["ssm_d"][ib], p["ssm_norm_w"][ib][None, :], p["ssm_w_out"][ib],
                batch=batch, seq=seq, q=cfg["ssd_q"])
            new_sc.append(cso[:, CONV_PAD - (CONV_W - 1):, :])
            new_ss.append(so.reshape(batch, -1, SSM_HEADDIM, SSM_STATE))
            ib += 1
        else:
            sh0 = states["rwkv_shift"][ic] if states else jnp.zeros((batch, d), F32)
            s0 = states["rwkv_wkv"][ic].reshape(batch, d, RWKV_HEAD) if states else None
            r, k, v, kk, bb, lw, g, sho = _rwkv_proj(
                x, sh0[:, None, :], g_mix, p["rwkv_mu"][ic], p["rwkv_w_rkv"][ic], p["rwkv_w0"][ic][None, :],
                p["rwkv_w_w1"][ic], p["rwkv_w_w2"][ic], p["rwkv_a0"][ic][None, :], p["rwkv_w_a1"][ic],
                p["rwkv_w_a2"][ic], p["rwkv_w_g1"][ic], p["rwkv_w_g2"][ic], p["rwkv_k_k"][ic][None, :],
                p["rwkv_k_a"][ic][None, :], batch=batch, seq=seq, nb=cfg["rwkv_proj_nb"], tl=cfg["rwkv_tl"])
            x, so = _rwkv_mixer(
                x, r, k, v, kk, bb, lw, g, s0, p["rwkv_r_k"][ic].reshape(1, d), p["rwkv_lnx_w"][ic][None, :],
                p["rwkv_lnx_b"][ic][None, :], p["rwkv_w_out"][ic],
                batch=batch, seq=seq, nb=cfg["rwkv_nb"], tl=cfg["rwkv_tl"], chunk=cfg["rwkv_chunk"])
            new_rs.append(sho[:, 0, :])
            new_rw.append(so.reshape(batch, d // RWKV_HEAD, RWKV_HEAD, RWKV_HEAD))
            ic += 1
        x = _ffn(x, p["norm_ffn"][layer][None, :], p["ffn_w1"][layer], p["ffn_w2"][layer],
                 p["norm_final"][None, :], tm=cfg["tm_ffn"], final_norm=(layer == depth - 1))
    return (x.reshape(batch, seq, d), jnp.stack(new_lc), jnp.stack(new_lh), jnp.stack(new_sc), jnp.stack(new_ss),
            jnp.stack(new_rs), jnp.stack(new_rw))


def _prep_params(p):
    q = dict(p)
    for name in ("lru_w_in", "lru_w_r", "lru_w_i", "lru_w_out", "ssm_w_out", "rwkv_w_rkv", "rwkv_w_w1", "rwkv_w_w2",
                 "rwkv_w_a1", "rwkv_w_a2", "rwkv_w_g1", "rwkv_w_g2", "rwkv_w_out", "ffn_w1", "ffn_w2"):
        q[name] = p[name].astype(BF16)
    d_ssm = p["ssm_w_out"].shape[1]
    heads = p["ssm_dt_bias"].shape[1]
    w_in = p["ssm_w_in"]
    cdim = w_in.shape[2] - d_ssm - heads
    pad = V7X_LANES - heads
    q["ssm_w_z"] = w_in[:, :, :d_ssm].astype(BF16)
    q["ssm_w_xbc"] = w_in[:, :, d_ssm:d_ssm + cdim].astype(BF16)
    q["ssm_w_dt"] = jnp.pad(w_in[:, :, d_ssm + cdim:], ((0, 0), (0, 0), (0, pad))).astype(BF16)
    for name in ("ssm_dt_bias", "ssm_a_log", "ssm_d"):
        q[name] = jnp.pad(p[name], ((0, 0), (0, pad)))[:, None, :]
    return q


def kernel(x_prompt, x_sample, state_lru_conv, state_lru_h, state_ssm_conv, state_ssm, state_rwkv_shift, state_rwkv_wkv, norm_mix, norm_ffn, norm_final, lru_w_in, lru_conv_w, lru_conv_b, lru_w_r, lru_b_r, lru_w_i, lru_b_i, lru_lambda, lru_w_out, ssm_w_in, ssm_conv_w, ssm_conv_b, ssm_dt_bias, ssm_a_log, ssm_d, ssm_norm_w, ssm_w_out, rwkv_mu, rwkv_w_rkv, rwkv_w0, rwkv_w_w1, rwkv_w_w2, rwkv_a0, rwkv_w_a1, rwkv_w_a2, rwkv_w_g1, rwkv_w_g2, rwkv_k_k, rwkv_k_a, rwkv_r_k, rwkv_lnx_w, rwkv_lnx_b, rwkv_w_out, ffn_w1, ffn_w2):
    p = _prep_params(dict(
        norm_mix=norm_mix, norm_ffn=norm_ffn, norm_final=norm_final,
        lru_w_in=lru_w_in, lru_conv_w=lru_conv_w, lru_conv_b=lru_conv_b, lru_w_r=lru_w_r, lru_b_r=lru_b_r,
        lru_w_i=lru_w_i, lru_b_i=lru_b_i, lru_lambda=lru_lambda, lru_w_out=lru_w_out,
        ssm_w_in=ssm_w_in, ssm_conv_w=ssm_conv_w, ssm_conv_b=ssm_conv_b, ssm_dt_bias=ssm_dt_bias,
        ssm_a_log=ssm_a_log, ssm_d=ssm_d, ssm_norm_w=ssm_norm_w, ssm_w_out=ssm_w_out,
        rwkv_mu=rwkv_mu, rwkv_w_rkv=rwkv_w_rkv, rwkv_w0=rwkv_w0, rwkv_w_w1=rwkv_w_w1, rwkv_w_w2=rwkv_w_w2,
        rwkv_a0=rwkv_a0, rwkv_w_a1=rwkv_w_a1, rwkv_w_a2=rwkv_w_a2, rwkv_w_g1=rwkv_w_g1, rwkv_w_g2=rwkv_w_g2,
        rwkv_k_k=rwkv_k_k, rwkv_k_a=rwkv_k_a, rwkv_r_k=rwkv_r_k, rwkv_lnx_w=rwkv_lnx_w, rwkv_lnx_b=rwkv_lnx_b,
        rwkv_w_out=rwkv_w_out, ffn_w1=ffn_w1, ffn_w2=ffn_w2))
    seq_p = x_prompt.shape[1]
    seq_s = x_sample.shape[1]
    cfg_p = dict(tm_proj=256, tm_ffn=512, lru_nb=1, lru_tl=256, ssd_q=min(SSM_CHUNK, seq_p),
                 rwkv_proj_nb=1, rwkv_nb=1, rwkv_tl=256, rwkv_chunk=RWKV_CHUNK)
    cfg_s = dict(tm_proj=256, tm_ffn=512, lru_nb=32, lru_tl=seq_s, ssd_q=seq_s,
                 rwkv_proj_nb=32, rwkv_nb=8, rwkv_tl=seq_s, rwkv_chunk=seq_s)
    y_p, lc_p, lh_p, sc_p, ss_p, rs_p, rw_p = _trunk(x_prompt, None, 0, p, cfg_p)
    states = dict(lru_conv=state_lru_conv, lru_h=state_lru_h, ssm_conv=state_ssm_conv, ssm=state_ssm,
                  rwkv_shift=state_rwkv_shift, rwkv_wkv=state_rwkv_wkv)
    y_s, lc_s, lh_s, sc_s, ss_s, rs_s, rw_s = _trunk(x_sample, states, PAST_LEN, p, cfg_s)
    return (y_p, y_s, lc_p, lc_s, lh_p, lh_s, sc_p, sc_s, ss_p, ss_s, rs_p, rs_s, rw_p, rw_s)
```

```python
import functools
import math

import jax
import jax.numpy as jnp
from jax import lax
from jax.experimental import pallas as pl
from jax.experimental.pallas import tpu as pltpu

F32 = jnp.float32
BF16 = jnp.bfloat16

NORM_EPS = 1e-6
CONV_W = 4
LRU_BLOCKS = 8
LRU_C = 8.0
SSM_HEADDIM = 64
SSM_GROUPS = 8
SSM_STATE = 128
SSM_CHUNK = 128
SSM_NORM_EPS = 1e-5
RWKV_HEAD = 64
GN_EPS = 64e-5
PAST_LEN = 16384

V7X_LANES = 128
V7X_SUBLANES = 8
V7X_VMEM_BYTES = 64 * 1024 * 1024
VMEM_LIMIT = V7X_VMEM_BYTES - 8 * 1024 * 1024

RWKV_CHUNK = 64
CONV_PAD = V7X_SUBLANES


def _bdot(a, b):
    return jnp.dot(a.astype(BF16), b.astype(BF16), preferred_element_type=F32)


def _bdot_nt(a, b):
    return lax.dot_general(a.astype(BF16), b.astype(BF16), (((1,), (1,)), ((), ())),
                           preferred_element_type=F32)


def _bdot_tn(a, b):
    return lax.dot_general(a.astype(BF16), b.astype(BF16), (((0,), (0,)), ((), ())),
                           preferred_element_type=F32)


def _rms(x, g):
    return x * lax.rsqrt(jnp.mean(x * x, axis=-1, keepdims=True) + NORM_EPS) * g


def _sigmoid(x):
    return 0.5 * jnp.tanh(0.5 * x) + 0.5


def _softplus(x):
    return jnp.maximum(x, 0.0) + jnp.log1p(jnp.exp(-jnp.abs(x)))


def _silu(x):
    return x * _sigmoid(x)


def _gelu_tanh(x):
    c = math.sqrt(2.0 / math.pi)
    return 0.5 * x * (1.0 + jnp.tanh(c * (x + 0.044715 * (x * x * x))))


def _row_index(rows, period):
    return lax.broadcasted_iota(jnp.int32, (rows, 1), 0) & (period - 1)


def _cumsum_rows(x, t_in, period):
    d = 1
    while d < period:
        x = x + jnp.where(t_in >= d, pltpu.roll(x, d, axis=0), 0.0)
        d *= 2
    return x


def _head_sum(x, head):
    rows, width = x.shape
    lo = lax.broadcasted_iota(jnp.int32, (1, V7X_LANES), 1) < head
    parts = []
    for j in range(width // V7X_LANES):
        seg = x[:, j * V7X_LANES:(j + 1) * V7X_LANES]
        s_lo = jnp.sum(jnp.where(lo, seg, 0.0), axis=-1, keepdims=True)
        s_hi = jnp.sum(jnp.where(lo, 0.0, seg), axis=-1, keepdims=True)
        parts.append(jnp.where(lo, s_lo, s_hi))
    return jnp.concatenate(parts, axis=1)


def _const_spec(shape):
    nd = len(shape)
    return pl.BlockSpec(shape, lambda *_: (0,) * nd, pipeline_mode=pl.Buffered(1))


def _params(sem):
    return pltpu.CompilerParams(dimension_semantics=sem, vmem_limit_bytes=VMEM_LIMIT)


def _proj_kernel(x_ref, g_ref, *refs, n_out, col_chunk):
    w_refs, o_refs = refs[:n_out], refs[n_out:]
    u = _rms(x_ref[...], g_ref[...]).astype(BF16)
    for w_ref, o_ref in zip(w_refs, o_refs):
        n = w_ref.shape[1]
        step = min(col_chunk, n)
        for c0 in range(0, n, step):
            o_ref[:, c0:c0 + step] = jnp.dot(u, w_ref[:, c0:c0 + step], preferred_element_type=F32)


def _norm_proj(x, g, weights, *, tm):
    t, d = x.shape
    assert t % tm == 0
    n_out = len(weights)
    return pl.pallas_call(
        functools.partial(_proj_kernel, n_out=n_out, col_chunk=1024),
        out_shape=[jax.ShapeDtypeStruct((t, w.shape[1]), F32) for w in weights],
        grid=(t // tm,),
        in_specs=[pl.BlockSpec((tm, d), lambda i: (i, 0)), _const_spec((1, d))]
        + [_const_spec(w.shape) for w in weights],
        out_specs=[pl.BlockSpec((tm, w.shape[1]), lambda i: (i, 0)) for w in weights],
        compiler_params=_params(("parallel",)),
        name="norm_proj",
    )(x, g, *weights)


def _ffn_kernel(x_ref, g_ref, w1_ref, w2_ref, gf_ref, o_ref, *, f_chunk, final_norm):
    x = x_ref[...]
    u = _rms(x, g_ref[...]).astype(BF16)
    acc = x
    for c0 in range(0, w1_ref.shape[1], f_chunk):
        h = jnp.dot(u, w1_ref[:, c0:c0 + f_chunk], preferred_element_type=F32)
        h = jnp.square(jnp.maximum(h, 0.0)).astype(BF16)
        acc = acc + jnp.dot(h, w2_ref[c0:c0 + f_chunk, :], preferred_element_type=F32)
    if final_norm:
        acc = _rms(acc, gf_ref[...])
    o_ref[...] = acc


def _ffn(x, g, w1, w2, g_final, *, tm, final_norm):
    t, d = x.shape
    f = w1.shape[1]
    assert t % tm == 0
    return pl.pallas_call(
        functools.partial(_ffn_kernel, f_chunk=1024, final_norm=final_norm),
        out_shape=jax.ShapeDtypeStruct((t, d), F32),
        grid=(t // tm,),
        in_specs=[pl.BlockSpec((tm, d), lambda i: (i, 0)), _const_spec((1, d)),
                  _const_spec((d, f)), _const_spec((f, d)), _const_spec((1, d))],
        out_specs=pl.BlockSpec((tm, d), lambda i: (i, 0)),
        compiler_params=_params(("parallel",)),
        name="ffn",
    )(x, g, w1, w2, g_final)


def _conv_taps(xe_ref, cw, cb, tl):
    nd = len(xe_ref.shape)

    def win(off):
        if nd == 3:
            return xe_ref[:, off:off + tl, :]
        return xe_ref[off:off + tl, :]

    base = CONV_PAD - (CONV_W - 1)
    y = cb + win(base) * cw[0:1, :]
    for k in range(1, CONV_W):
        y = y + win(base + k) * cw[k:k + 1, :]
    return y


def _lru_kernel(xg_ref, x_ref, cs_ref, h0_ref, cw_ref, cb_ref, wr_ref, br_ref, wi_ref, bi_ref, lam_ref, wo_ref,
                o_ref, cso_ref, ho_ref, xe_ref, hc_ref, *, nb, tl, start_pos):
    c = pl.program_id(1)
    rows = nb * tl
    d = x_ref.shape[1]
    blk = d // LRU_BLOCKS

    @pl.when(c == 0)
    def _():
        xe_ref[:, 0:CONV_PAD, :] = cs_ref[...]
        hc_ref[...] = h0_ref[...]

    xe_ref[:, CONV_PAD:CONV_PAD + tl, :] = xg_ref[:, 0:d].reshape(nb, tl, d)
    xc = _conv_taps(xe_ref, cw_ref[...], cb_ref[...], tl).reshape(rows, d)
    tail = xe_ref[:, tl:tl + CONV_PAD, :]
    xe_ref[:, 0:CONV_PAD, :] = tail
    cso_ref[...] = tail

    xcb = xc.astype(BF16)
    r_pre = jnp.concatenate(
        [jnp.dot(xcb[:, j * blk:(j + 1) * blk], wr_ref[j], preferred_element_type=F32) for j in range(LRU_BLOCKS)],
        axis=1) + br_ref[...]
    i_pre = jnp.concatenate(
        [jnp.dot(xcb[:, j * blk:(j + 1) * blk], wi_ref[j], preferred_element_type=F32) for j in range(LRU_BLOCKS)],
        axis=1) + bi_ref[...]
    r = _sigmoid(r_pre)
    gi = _sigmoid(i_pre)
    log_a = (-LRU_C) * r * _softplus(-lam_ref[...])
    a = jnp.exp(log_a)
    m2 = -jnp.tanh(log_a) * (a * a + 1.0)
    mult = jnp.where(m2 > 0.0, m2 * lax.rsqrt(m2), 0.0)
    t_in = _row_index(rows, tl)
    if start_pos == 0:
        reset = (c * tl + t_in) == 0
        a = jnp.where(reset, 0.0, a)
        mult = jnp.where(reset, 1.0, mult)
    b = mult * (gi * xc)
    h_prev = jnp.broadcast_to(hc_ref[...], (nb, tl, d)).reshape(rows, d)
    b = b + jnp.where(t_in == 0, a * h_prev, 0.0)

    step = 1
    while step < tl:
        m = t_in >= step
        b = jnp.where(m, a * pltpu.roll(b, step, axis=0) + b, b)
        if 2 * step < tl:
            a = jnp.where(m, a * pltpu.roll(a, step, axis=0), a)
        step *= 2
    h_last = b.reshape(nb, tl, d)[:, tl - 1:tl, :]
    hc_ref[...] = h_last
    ho_ref[...] = h_last

    y = (b * _gelu_tanh(xg_ref[:, d:2 * d])).astype(BF16)
    o_ref[...] = x_ref[...] + jnp.dot(y, wo_ref[...], preferred_element_type=F32)


def _lru_mixer(x, xg, conv_state, h0, cw, cb, wr, br, wi, bi, lam, wo, *, batch, seq, nb, tl, start_pos):
    d = x.shape[1]
    nchunk = seq // tl
    assert batch % nb == 0 and seq % tl == 0 and (nb == 1 or nchunk == 1)
    rows = nb * tl
    row_map = lambda i, c: (i * nchunk + c, 0)
    seq_map = lambda i, c: (i, 0, 0)
    return pl.pallas_call(
        functools.partial(_lru_kernel, nb=nb, tl=tl, start_pos=start_pos),
        out_shape=[jax.ShapeDtypeStruct((batch * seq, d), F32),
                   jax.ShapeDtypeStruct((batch, CONV_PAD, d), F32),
                   jax.ShapeDtypeStruct((batch, 1, d), F32)],
        grid=(batch // nb, nchunk),
        in_specs=[pl.BlockSpec((rows, 2 * d), row_map), pl.BlockSpec((rows, d), row_map),
                  pl.BlockSpec((nb, CONV_PAD, d), seq_map), pl.BlockSpec((nb, 1, d), seq_map),
                  _const_spec(cw.shape), _const_spec(cb.shape), _const_spec(wr.shape), _const_spec(br.shape),
                  _const_spec(wi.shape), _const_spec(bi.shape), _const_spec(lam.shape), _const_spec(wo.shape)],
        out_specs=[pl.BlockSpec((rows, d), row_map), pl.BlockSpec((nb, CONV_PAD, d), seq_map),
                   pl.BlockSpec((nb, 1, d), seq_map)],
        scratch_shapes=[pltpu.VMEM((nb, CONV_PAD + tl, d), F32), pltpu.VMEM((nb, 1, d), F32)],
        compiler_params=_params(("parallel", "arbitrary")),
        name="lru_mixer",
    )(xg, x, conv_state, h0, cw, cb, wr, br, wi, bi, lam, wo)


def _split_bf16(x, terms):
    parts = []
    for _ in range(terms - 1):
        hi = x.astype(BF16).astype(F32)
        parts.append(hi)
        x = x - hi
    parts.append(x)
    return parts


def _ssd_kernel(z_ref, xbc_ref, dt_ref, x_ref, cs_ref, s0_ref, cw_ref, cb_ref, dtb_ref, alog_ref, dskx_ref, nw_ref,
                ex_ref, wo_ref, o_ref, cso_ref, so_ref, xe_ref, st_ref, *, q, zero_state):
    c = pl.program_id(1)
    d_ssm = z_ref.shape[1]
    n = SSM_STATE
    hpg = d_ssm // SSM_HEADDIM // SSM_GROUPS
    gw = hpg * SSM_HEADDIM
    nblk = d_ssm // n

    @pl.when(c == 0)
    def _():
        xe_ref[0:CONV_PAD, :] = cs_ref[0]
        if zero_state:
            st_ref[...] = jnp.zeros(st_ref.shape, F32)
        else:
            for j in range(nblk):
                st_ref[:, j * n:(j + 1) * n] = s0_ref[0, j * n:(j + 1) * n, :].T

    xe_ref[CONV_PAD:CONV_PAD + q, :] = xbc_ref[...]
    xbc = _silu(_conv_taps(xe_ref, cw_ref[...], cb_ref[...], q))
    tail = xe_ref[q:q + CONV_PAD, :]
    xe_ref[0:CONV_PAD, :] = tail
    cso_ref[0] = tail

    xs = xbc[:, 0:d_ssm]
    bm = xbc[:, d_ssm:d_ssm + SSM_GROUPS * n]
    cm = xbc[:, d_ssm + SSM_GROUPS * n:]
    dt = _softplus(dt_ref[...] + dtb_ref[...])
    da = dt * (-jnp.exp(alog_ref[...]))
    t_in = _row_index(q, q)
    acs = _cumsum_rows(da, t_in, q)
    if q < V7X_LANES:
        acs_sq = jnp.concatenate([acs, jnp.zeros((V7X_LANES - q, V7X_LANES), F32)], axis=0)
    else:
        acs_sq = acs
    acs_t = acs_sq.T

    lhs = jnp.concatenate(_split_bf16(dt, 2) + _split_bf16(acs, 3), axis=0).astype(BF16)
    ex = jnp.dot(lhs, ex_ref[...], preferred_element_type=F32)
    dt_x = ex[0:q] + ex[q:2 * q]
    acs_x = ex[2 * q:3 * q] + ex[3 * q:4 * q] + ex[4 * q:5 * q]
    last_x = acs_x[q - 1:q, :]
    xdt = (xs * dt_x).astype(BF16)
    xdtd = (xs * (dt_x * jnp.exp(last_x - acs_x))).astype(BF16)

    tri = lax.broadcasted_iota(jnp.int32, (q, q), 0) >= lax.broadcasted_iota(jnp.int32, (q, q), 1)
    lane_head = lax.broadcasted_iota(jnp.int32, (1, gw), 1) // SSM_HEADDIM
    groups = range(SSM_GROUPS)
    bgs = [bm[:, g * n:(g + 1) * n].astype(BF16) for g in groups]
    cgs = [cm[:, g * n:(g + 1) * n].astype(BF16) for g in groups]
    cb_mats = [_bdot_nt(cgs[g], bgs[g]) for g in groups]
    y_off = [_bdot(cgs[g], st_ref[:, g * gw:(g + 1) * gw]) for g in groups]
    y_diag = []
    for g in groups:
        xdt_g = xdt[:, g * gw:(g + 1) * gw]
        acc = None
        for e in range(hpg):
            h = g * hpg + e
            seg = acs[:, h:h + 1] - acs_t[h:h + 1, 0:q]
            l_mat = jnp.where(tri, jnp.exp(jnp.minimum(seg, 0.0)), 0.0)
            part = jnp.dot((cb_mats[g] * l_mat).astype(BF16), jnp.where(lane_head == e, xdt_g, jnp.zeros_like(xdt_g)),
                           preferred_element_type=F32)
            acc = part if acc is None else acc + part
        y_diag.append(acc)
    st_new = [_bdot_tn(bgs[g], xdtd[:, g * gw:(g + 1) * gw]) for g in groups]
    e_last_x = jnp.exp(last_x)
    for g in groups:
        cols = slice(g * gw, (g + 1) * gw)
        st_ref[:, cols] = st_ref[:, cols] * e_last_x[:, cols] + st_new[g]

    @pl.when(c == pl.num_programs(1) - 1)
    def _():
        for j in range(nblk):
            so_ref[0, j * n:(j + 1) * n, :] = st_ref[:, j * n:(j + 1) * n].T

    y = jnp.concatenate(y_diag, axis=1) + jnp.concatenate(y_off, axis=1) * jnp.exp(acs_x) + dskx_ref[...] * xs
    y = y * _silu(z_ref[...])
    norm_parts = []
    for g in groups:
        yg = y[:, g * gw:(g + 1) * gw]
        norm_parts.append(yg * lax.rsqrt(jnp.mean(yg * yg, axis=-1, keepdims=True) + SSM_NORM_EPS))
    yn = (jnp.concatenate(norm_parts, axis=1) * nw_ref[...]).astype(BF16)
    o_ref[...] = x_ref[...] + jnp.dot(yn, wo_ref[...], preferred_element_type=F32)


def _ssd_mixer(x, z, xbc, dt, conv_state, s0, cw, cb, dtb, alog, dskx, nw, expand, wo, *, batch, seq, q):
    d = x.shape[1]
    d_ssm = z.shape[1]
    cdim = xbc.shape[1]
    nchunk = seq // q
    assert seq % q == 0
    zero_state = s0 is None
    if zero_state:
        s0 = jnp.zeros((1, d_ssm, SSM_STATE), F32)
    row_map = lambda b, c: (b * nchunk + c, 0)
    seq_map = lambda b, c: (b, 0, 0)
    s0_map = (lambda b, c: (0, 0, 0)) if zero_state else seq_map
    return pl.pallas_call(
        functools.partial(_ssd_kernel, q=q, zero_state=zero_state),
        out_shape=[jax.ShapeDtypeStruct((batch * seq, d), F32),
                   jax.ShapeDtypeStruct((batch, CONV_PAD, cdim), F32),
                   jax.ShapeDtypeStruct((batch, d_ssm, SSM_STATE), F32)],
        grid=(batch, nchunk),
        in_specs=[pl.BlockSpec((q, d_ssm), row_map), pl.BlockSpec((q, cdim), row_map),
                  pl.BlockSpec((q, V7X_LANES), row_map), pl.BlockSpec((q, d), row_map),
                  pl.BlockSpec((1, CONV_PAD, cdim), seq_map), pl.BlockSpec((1, d_ssm, SSM_STATE), s0_map),
                  _const_spec(cw.shape), _const_spec(cb.shape), _const_spec(dtb.shape), _const_spec(alog.shape),
                  _const_spec(dskx.shape), _const_spec(nw.shape), _const_spec(expand.shape), _const_spec(wo.shape)],
        out_specs=[pl.BlockSpec((q, d), row_map), pl.BlockSpec((1, CONV_PAD, cdim), seq_map),
                   pl.BlockSpec((1, d_ssm, SSM_STATE), seq_map)],
        scratch_shapes=[pltpu.VMEM((CONV_PAD + q, cdim), F32), pltpu.VMEM((SSM_STATE, d_ssm), F32)],
        compiler_params=_params(("parallel", "arbitrary")),
        name="ssd_mixer",
    )(z, xbc, dt, x, conv_state, s0, cw, cb, dtb, alog, dskx, nw, expand, wo)


def _rwkv_proj_kernel(x_ref, sh_ref, g_ref, mu_ref, wrkv_ref, w0_ref, w1_ref, w2_ref, a0_ref, a1_ref, a2_ref,
                      g1_ref, g2_ref, kkw_ref, kaw_ref,
                      r_o, k_o, v_o, kk_o, bb_o, lw_o, g_o, sho_ref, carry_ref, *, nb, tl):
    c = pl.program_id(1)
    rows = nb * tl
    d = x_ref.shape[1]

    @pl.when(c == 0)
    def _():
        carry_ref[...] = sh_ref[...]

    u = _rms(x_ref[...], g_ref[...])
    t_in = _row_index(rows, tl)
    prev = jnp.where(t_in == 0, jnp.broadcast_to(carry_ref[...], (nb, tl, d)).reshape(rows, d),
                     pltpu.roll(u, 1, axis=0))
    last = u.reshape(nb, tl, d)[:, tl - 1:tl, :]
    carry_ref[...] = last
    sho_ref[...] = last

    diff = prev - u
    mu = mu_ref[...]

    def mixed(s):
        return (u + diff * mu[s:s + 1, :]).astype(BF16)

    r = jnp.dot(mixed(0), wrkv_ref[0], preferred_element_type=F32)
    k = jnp.dot(mixed(1), wrkv_ref[1], preferred_element_type=F32)
    v_o[...] = jnp.dot(mixed(2), wrkv_ref[2], preferred_element_type=F32)
    w_pre = w0_ref[...] + _bdot(jnp.tanh(jnp.dot(mixed(3), w1_ref[...], preferred_element_type=F32)), w2_ref[...])
    sp = jnp.maximum(-w_pre, 0.0) + jnp.log(1.0 + jnp.exp(-jnp.abs(w_pre)))
    lw_o[...] = -jnp.exp(-sp - 0.5)
    a = _sigmoid(a0_ref[...] + _bdot(jnp.dot(mixed(4), a1_ref[...], preferred_element_type=F32), a2_ref[...]))
    g_o[...] = _bdot(_sigmoid(jnp.dot(mixed(5), g1_ref[...], preferred_element_type=F32)), g2_ref[...])
    kk = k * kkw_ref[...]
    kk = kk * jnp.minimum(lax.rsqrt(_head_sum(kk * kk, RWKV_HEAD)), 1e12)
    r_o[...] = r
    k_o[...] = k * (1.0 + (a - 1.0) * kaw_ref[...])
    kk_o[...] = kk
    bb_o[...] = kk * a


def _rwkv_proj(x, shift0, g, mu, wrkv, w0, w1, w2, a0, a1, a2, g1, g2, kkw, kaw, *, batch, seq, nb, tl):
    d = x.shape[1]
    nchunk = seq // tl
    assert batch % nb == 0 and seq % tl == 0 and (nb == 1 or nchunk == 1)
    rows = nb * tl
    row_map = lambda i, c: (i * nchunk + c, 0)
    seq_map = lambda i, c: (i, 0, 0)
    consts = (g, mu, wrkv, w0, w1, w2, a0, a1, a2, g1, g2, kkw, kaw)
    return pl.pallas_call(
        functools.partial(_rwkv_proj_kernel, nb=nb, tl=tl),
        out_shape=[jax.ShapeDtypeStruct((batch * seq, d), F32)] * 7 + [jax.ShapeDtypeStruct((batch, 1, d), F32)],
        grid=(batch // nb, nchunk),
        in_specs=[pl.BlockSpec((rows, d), row_map), pl.BlockSpec((nb, 1, d), seq_map)]
        + [_const_spec(w.shape) for w in consts],
        out_specs=[pl.BlockSpec((rows, d), row_map)] * 7 + [pl.BlockSpec((nb, 1, d), seq_map)],
        scratch_shapes=[pltpu.VMEM((nb, 1, d), F32)],
        compiler_params=_params(("parallel", "arbitrary")),
        name="rwkv_proj",
    )(x, shift0, *consts)


def _unit_lower_solve(a_low, w, size):
    nmat = -a_low
    u = w + _bdot(nmat, w)
    m = 2
    while m < size:
        nmat = _bdot(nmat, nmat)
        u = u + _bdot(nmat, u)
        m *= 2
    return u


def _rwkv_chunk_kernel(r_ref, k_ref, v_ref, kk_ref, bb_ref, lw_ref, g_ref, x_ref, s0_ref, rk_ref, lnw_ref, lnb_ref,
                       wo_ref, o_ref, so_ref, s_ref, og_ref, *, nb, tl, chunk, zero_state):
    c = pl.program_id(1)
    d = x_ref.shape[1]
    hd = RWKV_HEAD
    heads = d // hd
    cpb = tl // chunk
    n_iter = nb * cpb

    @pl.when(c == 0)
    def _():
        if zero_state:
            s_ref[...] = jnp.zeros(s_ref.shape, F32)
        else:
            s_ref[...] = s0_ref[...]

    t_in = _row_index(chunk, chunk)
    ri = lax.broadcasted_iota(jnp.int32, (chunk, chunk), 0)
    ci = lax.broadcasted_iota(jnp.int32, (chunk, chunk), 1)
    strict = ri > ci
    incl = ri >= ci

    def body(it, carry):
        seq_i = it if cpb == 1 else 0
        rows = pl.ds(pl.multiple_of(it * chunk, chunk), chunk)
        r = r_ref[rows, :]
        k = k_ref[rows, :]
        v = v_ref[rows, :]
        kk = kk_ref[rows, :]
        bb = bb_ref[rows, :]
        lw = lw_ref[rows, :]
        cum = _cumsum_rows(lw, t_in, chunk)
        c_last = cum[chunk - 1:chunk, :]
        g_last = jnp.exp(c_last)
        e_inv = jnp.exp(-cum)
        e_end = jnp.exp(c_last - cum)
        kkd = (kk * jnp.exp(cum - lw)).astype(BF16)
        rd = (r * jnp.exp(cum)).astype(BF16)
        ki = (k * e_inv).astype(BF16)
        bi = (bb * e_inv).astype(BF16)
        kg = (k * e_end).astype(BF16)
        bg = (bb * e_end).astype(BF16)
        hs = range(heads)
        sls = [slice(h * hd, (h + 1) * hd) for h in hs]
        s_l = [s_ref[seq_i, sl, :] for sl in sls]
        s_b = [s.astype(BF16) for s in s_l]
        v_l = [v[:, sl].astype(BF16) for sl in sls]
        a_k = [jnp.where(strict, _bdot_nt(kkd[:, sl], ki[:, sl]), 0.0) for sl in sls]
        nmat = [jnp.where(strict, -_bdot_nt(kkd[:, sl], bi[:, sl]), 0.0) for sl in sls]
        w_l = [_bdot_nt(kkd[:, sls[h]], s_b[h]) + _bdot(a_k[h], v_l[h]) for h in hs]
        u_l = [w_l[h] + _bdot(nmat[h], w_l[h]) for h in hs]
        m = 2
        while m < chunk:
            nmat = [_bdot(x, x) for x in nmat]
            u_l = [u_l[h] + _bdot(nmat[h], u_l[h]) for h in hs]
            m *= 2
        r_k = [jnp.where(incl, _bdot_nt(rd[:, sl], ki[:, sl]), 0.0) for sl in sls]
        r_b = [jnp.where(incl, _bdot_nt(rd[:, sl], bi[:, sl]), 0.0) for sl in sls]
        o_parts = [_bdot_nt(rd[:, sls[h]], s_b[h]) + _bdot(r_k[h], v_l[h]) - _bdot(r_b[h], u_l[h]) for h in hs]
        for h in hs:
            s_ref[seq_i, sls[h], :] = (s_l[h] * g_last[:, sls[h]] + _bdot_tn(v_l[h], kg[:, sls[h]])
                                       - _bdot_tn(u_l[h], bg[:, sls[h]]))
        o = jnp.concatenate(o_parts, axis=1)
        mean = _head_sum(o, hd) * (1.0 / hd)
        oc = o - mean
        var = _head_sum(oc * oc, hd) * (1.0 / hd)
        on = oc * lax.rsqrt(var + GN_EPS) * lnw_ref[...] + lnb_ref[...]
        bonus = _head_sum(r * k * rk_ref[...], hd) * v
        og_ref[rows, :] = (on + bonus) * g_ref[rows, :]
        return carry

    lax.fori_loop(0, n_iter, body, 0)

    @pl.when(c == pl.num_programs(1) - 1)
    def _():
        so_ref[...] = s_ref[...]

    o_ref[...] = x_ref[...] + _bdot(og_ref[...], wo_ref[...])


def _rwkv_mixer(x, r, k, v, kk, bb, lw, g, s0, rk, lnw, lnb, wo, *, batch, seq, nb, tl, chunk):
    d = x.shape[1]
    nchunk = seq // tl
    assert batch % nb == 0 and seq % tl == 0 and tl % chunk == 0 and (nb == 1 or tl == chunk)
    rows = nb * tl
    zero_state = s0 is None
    if zero_state:
        s0 = jnp.zeros((nb, d, RWKV_HEAD), F32)
    row_map = lambda i, c: (i * nchunk + c, 0)
    seq_map = lambda i, c: (i, 0, 0)
    s0_map = (lambda i, c: (0, 0, 0)) if zero_state else seq_map
    row_spec = pl.BlockSpec((rows, d), row_map)
    return pl.pallas_call(
        functools.partial(_rwkv_chunk_kernel, nb=nb, tl=tl, chunk=chunk, zero_state=zero_state),
        out_shape=[jax.ShapeDtypeStruct((batch * seq, d), F32), jax.ShapeDtypeStruct((batch, d, RWKV_HEAD), F32)],
        grid=(batch // nb, nchunk),
        in_specs=[row_spec] * 8 + [pl.BlockSpec((nb, d, RWKV_HEAD), s0_map),
                                   _const_spec(rk.shape), _const_spec(lnw.shape), _const_spec(lnb.shape),
                                   _const_spec(wo.shape)],
        out_specs=[row_spec, pl.BlockSpec((nb, d, RWKV_HEAD), seq_map)],
        scratch_shapes=[pltpu.VMEM((nb, d, RWKV_HEAD), F32), pltpu.VMEM((rows, d), F32)],
        compiler_params=_params(("parallel", "arbitrary")),
        name="rwkv_chunk",
    )(r, k, v, kk, bb, lw, g, x, s0, rk, lnw, lnb, wo)


def _pad_conv_state(state):
    return jnp.pad(state, ((0, 0), (CONV_PAD - (CONV_W - 1), 0), (0, 0)))


def _trunk(x3, states, start_pos, p, cfg):
    batch, seq, d = x3.shape
    x = x3.reshape(batch * seq, d)
    depth = p["norm_mix"].shape[0]
    new_lc, new_lh, new_sc, new_ss, new_rs, new_rw = [], [], [], [], [], []
    ia = ib = ic = 0
    for layer in range(depth):
        g_mix = p["norm_mix"][layer][None, :]
        kind = layer % 3
        if kind == 0:
            (xg,) = _norm_proj(x, g_mix, [p["lru_w_in"][ia]], tm=cfg["tm_proj"])
            cs = states["lru_conv"][ia] if states else jnp.zeros((batch, CONV_W - 1, d), F32)
            h0 = states["lru_h"][ia] if states else jnp.zeros((batch, d), F32)
            x, cso, ho = _lru_mixer(
                x, xg, _pad_conv_state(cs), h0[:, None, :],
                p["lru_conv_w"][ia], p["lru_conv_b"][ia][None, :], p["lru_w_r"][ia], p["lru_b_r"][ia][None, :],
                p["lru_w_i"][ia], p["lru_b_i"][ia][None, :], p["lru_lambda"][ia][None, :], p["lru_w_out"][ia],
                batch=batch, seq=seq, nb=cfg["lru_nb"], tl=cfg["lru_tl"], start_pos=start_pos)
            new_lc.append(cso[:, CONV_PAD - (CONV_W - 1):, :])
            new_lh.append(ho[:, 0, :])
            ia += 1
        elif kind == 1:
            z, xbc, dt = _norm_proj(x, g_mix, [p["ssm_w_z"][ib], p["ssm_w_xbc"][ib], p["ssm_w_dt"][ib]],
                                    tm=cfg["tm_proj"])
            cdim = xbc.shape[1]
            cs = states["ssm_conv"][ib] if states else jnp.zeros((batch, CONV_W - 1, cdim), F32)
            s0 = states["ssm"][ib].reshape(batch, -1, SSM_STATE) if states else None
            x, cso, so = _ssd_mixer(
                x, z, xbc, dt, _pad_conv_state(cs), s0,
                p["ssm_conv_w"][ib], p["ssm_conv_b"][ib][None, :], p["ssm_dt_bias"][ib], p["ssm_a_log"][ib],
                p["ssm_d_x"][ib], p["ssm_norm_w"][ib][None, :], p["ssm_expand"], p["ssm_w_out"][ib],
                batch=batch, seq=seq, q=cfg["ssd_q"])
            new_sc.append(cso[:, CONV_PAD - (CONV_W - 1):, :])
            new_ss.append(so.reshape(batch, -1, SSM_HEADDIM, SSM_STATE))
            ib += 1
        else:
            sh0 = states["rwkv_shift"][ic] if states else jnp.zeros((batch, d), F32)
            s0 = states["rwkv_wkv"][ic].reshape(batch, d, RWKV_HEAD) if states else None
            r, k, v, kk, bb, lw, g, sho = _rwkv_proj(
                x, sh0[:, None, :], g_mix, p["rwkv_mu"][ic], p["rwkv_w_rkv"][ic], p["rwkv_w0"][ic][None, :],
                p["rwkv_w_w1"][ic], p["rwkv_w_w2"][ic], p["rwkv_a0"][ic][None, :], p["rwkv_w_a1"][ic],
                p["rwkv_w_a2"][ic], p["rwkv_w_g1"][ic], p["rwkv_w_g2"][ic], p["rwkv_k_k"][ic][None, :],
                p["rwkv_k_a"][ic][None, :], batch=batch, seq=seq, nb=cfg["rwkv_proj_nb"], tl=cfg["rwkv_tl"])
            x, so = _rwkv_mixer(
                x, r, k, v, kk, bb, lw, g, s0, p["rwkv_r_k"][ic].reshape(1, d), p["rwkv_lnx_w"][ic][None, :],
                p["rwkv_lnx_b"][ic][None, :], p["rwkv_w_out"][ic],
                batch=batch, seq=seq, nb=cfg["rwkv_nb"], tl=cfg["rwkv_tl"], chunk=cfg["rwkv_chunk"])
            new_rs.append(sho[:, 0, :])
            new_rw.append(so.reshape(batch, d // RWKV_HEAD, RWKV_HEAD, RWKV_HEAD))
            ic += 1
        x = _ffn(x, p["norm_ffn"][layer][None, :], p["ffn_w1"][layer], p["ffn_w2"][layer],
                 p["norm_final"][None, :], tm=cfg["tm_ffn"], final_norm=(layer == depth - 1))
    return (x.reshape(batch, seq, d), jnp.stack(new_lc), jnp.stack(new_lh), jnp.stack(new_sc), jnp.stack(new_ss),
            jnp.stack(new_rs), jnp.stack(new_rw))


def _prep_params(p):
    q = dict(p)
    for name in ("lru_w_in", "lru_w_r", "lru_w_i", "lru_w_out", "ssm_w_out", "rwkv_w_rkv", "rwkv_w_w1", "rwkv_w_w2",
                 "rwkv_w_a1", "rwkv_w_a2", "rwkv_w_g1", "rwkv_w_g2", "rwkv_w_out", "ffn_w1", "ffn_w2"):
        q[name] = p[name].astype(BF16)
    d_ssm = p["ssm_w_out"].shape[1]
    heads = p["ssm_dt_bias"].shape[1]
    w_in = p["ssm_w_in"]
    cdim = w_in.shape[2] - d_ssm - heads
    pad = V7X_LANES - heads
    q["ssm_w_z"] = w_in[:, :, :d_ssm].astype(BF16)
    q["ssm_w_xbc"] = w_in[:, :, d_ssm:d_ssm + cdim].astype(BF16)
    q["ssm_w_dt"] = jnp.pad(w_in[:, :, d_ssm + cdim:], ((0, 0), (0, 0), (0, pad))).astype(BF16)
    for name in ("ssm_dt_bias", "ssm_a_log"):
        q[name] = jnp.pad(p[name], ((0, 0), (0, pad)))[:, None, :]
    q["ssm_d_x"] = jnp.repeat(p["ssm_d"], SSM_HEADDIM, axis=1)[:, None, :]
    q["ssm_expand"] = (jnp.arange(V7X_LANES)[:, None] == jnp.arange(d_ssm)[None, :] // SSM_HEADDIM).astype(BF16)
    return q


def kernel(x_prompt, x_sample, state_lru_conv, state_lru_h, state_ssm_conv, state_ssm, state_rwkv_shift, state_rwkv_wkv, norm_mix, norm_ffn, norm_final, lru_w_in, lru_conv_w, lru_conv_b, lru_w_r, lru_b_r, lru_w_i, lru_b_i, lru_lambda, lru_w_out, ssm_w_in, ssm_conv_w, ssm_conv_b, ssm_dt_bias, ssm_a_log, ssm_d, ssm_norm_w, ssm_w_out, rwkv_mu, rwkv_w_rkv, rwkv_w0, rwkv_w_w1, rwkv_w_w2, rwkv_a0, rwkv_w_a1, rwkv_w_a2, rwkv_w_g1, rwkv_w_g2, rwkv_k_k, rwkv_k_a, rwkv_r_k, rwkv_lnx_w, rwkv_lnx_b, rwkv_w_out, ffn_w1, ffn_w2):
    p = _prep_params(dict(
        norm_mix=norm_mix, norm_ffn=norm_ffn, norm_final=norm_final,
        lru_w_in=lru_w_in, lru_conv_w=lru_conv_w, lru_conv_b=lru_conv_b, lru_w_r=lru_w_r, lru_b_r=lru_b_r,
        lru_w_i=lru_w_i, lru_b_i=lru_b_i, lru_lambda=lru_lambda, lru_w_out=lru_w_out,
        ssm_w_in=ssm_w_in, ssm_conv_w=ssm_conv_w, ssm_conv_b=ssm_conv_b, ssm_dt_bias=ssm_dt_bias,
        ssm_a_log=ssm_a_log, ssm_d=ssm_d, ssm_norm_w=ssm_norm_w, ssm_w_out=ssm_w_out,
        rwkv_mu=rwkv_mu, rwkv_w_rkv=rwkv_w_rkv, rwkv_w0=rwkv_w0, rwkv_w_w1=rwkv_w_w1, rwkv_w_w2=rwkv_w_w2,
        rwkv_a0=rwkv_a0, rwkv_w_a1=rwkv_w_a1, rwkv_w_a2=rwkv_w_a2, rwkv_w_g1=rwkv_w_g1, rwkv_w_g2=rwkv_w_g2,
        rwkv_k_k=rwkv_k_k, rwkv_k_a=rwkv_k_a, rwkv_r_k=rwkv_r_k, rwkv_lnx_w=rwkv_lnx_w, rwkv_lnx_b=rwkv_lnx_b,
        rwkv_w_out=rwkv_w_out, ffn_w1=ffn_w1, ffn_w2=ffn_w2))
    seq_p = x_prompt.shape[1]
    seq_s = x_sample.shape[1]
    cfg_p = dict(tm_proj=256, tm_ffn=512, lru_nb=1, lru_tl=256, ssd_q=min(SSM_CHUNK, seq_p),
                 rwkv_proj_nb=1, rwkv_nb=1, rwkv_tl=256, rwkv_chunk=RWKV_CHUNK)
    cfg_s = dict(tm_proj=256, tm_ffn=512, lru_nb=32, lru_tl=seq_s, ssd_q=seq_s,
                 rwkv_proj_nb=32, rwkv_nb=8, rwkv_tl=seq_s, rwkv_chunk=seq_s)
    y_p, lc_p, lh_p, sc_p, ss_p, rs_p, rw_p = _trunk(x_prompt, None, 0, p, cfg_p)
    states = dict(lru_conv=state_lru_conv, lru_h=state_lru_h, ssm_conv=state_ssm_conv, ssm=state_ssm,
                  rwkv_shift=state_rwkv_shift, rwkv_wkv=state_rwkv_wkv)
    y_s, lc_s, lh_s, sc_s, ss_s, rs_s, rw_s = _trunk(x_sample, states, PAST_LEN, p, cfg_s)
    return (y_p, y_s, lc_p, lc_s, lh_p, lh_s, sc_p, sc_s, ss_p, ss_s, rs_p, rs_s, rw_p, rw_s)
```

```python
import functools
import math

import jax
import jax.numpy as jnp
from jax import lax
from jax.experimental import pallas as pl
from jax.experimental.pallas import tpu as pltpu

F32 = jnp.float32
BF16 = jnp.bfloat16

NORM_EPS = 1e-6
CONV_W = 4
LRU_BLOCKS = 8
LRU_C = 8.0
SSM_HEADDIM = 64
SSM_GROUPS = 8
SSM_STATE = 128
SSM_CHUNK = 128
SSM_NORM_EPS = 1e-5
RWKV_HEAD = 64
GN_EPS = 64e-5
PAST_LEN = 16384

V7X_LANES = 128
V7X_SUBLANES = 8
V7X_VMEM_BYTES = 64 * 1024 * 1024
VMEM_LIMIT = V7X_VMEM_BYTES - 8 * 1024 * 1024

RWKV_CHUNK = 64
CONV_PAD = V7X_SUBLANES


def _bdot(a, b):
    return jnp.dot(a.astype(BF16), b.astype(BF16), preferred_element_type=F32)


def _bdot_nt(a, b):
    return lax.dot_general(a.astype(BF16), b.astype(BF16), (((1,), (1,)), ((), ())),
                           preferred_element_type=F32)


def _bdot_tn(a, b):
    return lax.dot_general(a.astype(BF16), b.astype(BF16), (((0,), (0,)), ((), ())),
                           preferred_element_type=F32)


def _rms(x, g):
    return x * lax.rsqrt(jnp.mean(x * x, axis=-1, keepdims=True) + NORM_EPS) * g


def _sigmoid(x):
    return 0.5 * jnp.tanh(0.5 * x) + 0.5


def _softplus(x):
    return jnp.maximum(x, 0.0) + jnp.log1p(jnp.exp(-jnp.abs(x)))


def _silu(x):
    return x * _sigmoid(x)


def _gelu_tanh(x):
    c = math.sqrt(2.0 / math.pi)
    return 0.5 * x * (1.0 + jnp.tanh(c * (x + 0.044715 * (x * x * x))))


def _row_index(rows, period):
    return lax.broadcasted_iota(jnp.int32, (rows, 1), 0) & (period - 1)


def _cumsum_rows(x, t_in, period):
    d = 1
    while d < period:
        x = x + jnp.where(t_in >= d, pltpu.roll(x, d, axis=0), 0.0)
        d *= 2
    return x


def _head_sum(x, head):
    rows, width = x.shape
    lo = lax.broadcasted_iota(jnp.int32, (1, V7X_LANES), 1) < head
    parts = []
    for j in range(width // V7X_LANES):
        seg = x[:, j * V7X_LANES:(j + 1) * V7X_LANES]
        s_lo = jnp.sum(jnp.where(lo, seg, 0.0), axis=-1, keepdims=True)
        s_hi = jnp.sum(jnp.where(lo, 0.0, seg), axis=-1, keepdims=True)
        parts.append(jnp.where(lo, s_lo, s_hi))
    return jnp.concatenate(parts, axis=1)


def _const_spec(shape):
    nd = len(shape)
    return pl.BlockSpec(shape, lambda *_: (0,) * nd, pipeline_mode=pl.Buffered(1))


def _params(sem):
    return pltpu.CompilerParams(dimension_semantics=sem, vmem_limit_bytes=VMEM_LIMIT)


def _proj_kernel(x_ref, g_ref, *refs, n_out, col_chunk):
    w_refs, o_refs = refs[:n_out], refs[n_out:]
    u = _rms(x_ref[...], g_ref[...]).astype(BF16)
    for w_ref, o_ref in zip(w_refs, o_refs):
        n = w_ref.shape[1]
        step = min(col_chunk, n)
        for c0 in range(0, n, step):
            o_ref[:, c0:c0 + step] = jnp.dot(u, w_ref[:, c0:c0 + step], preferred_element_type=F32)


def _norm_proj(x, g, weights, *, tm):
    t, d = x.shape
    assert t % tm == 0
    n_out = len(weights)
    return pl.pallas_call(
        functools.partial(_proj_kernel, n_out=n_out, col_chunk=1024),
        out_shape=[jax.ShapeDtypeStruct((t, w.shape[1]), F32) for w in weights],
        grid=(t // tm,),
        in_specs=[pl.BlockSpec((tm, d), lambda i: (i, 0)), _const_spec((1, d))]
        + [_const_spec(w.shape) for w in weights],
        out_specs=[pl.BlockSpec((tm, w.shape[1]), lambda i: (i, 0)) for w in weights],
        compiler_params=_params(("parallel",)),
        name="norm_proj",
    )(x, g, *weights)


def _ffn_kernel(x_ref, g_ref, w1_ref, w2_ref, gf_ref, o_ref, *, f_chunk, final_norm):
    x = x_ref[...]
    u = _rms(x, g_ref[...]).astype(BF16)
    acc = x
    for c0 in range(0, w1_ref.shape[1], f_chunk):
        h = jnp.dot(u, w1_ref[:, c0:c0 + f_chunk], preferred_element_type=F32)
        h = jnp.square(jnp.maximum(h, 0.0)).astype(BF16)
        acc = acc + jnp.dot(h, w2_ref[c0:c0 + f_chunk, :], preferred_element_type=F32)
    if final_norm:
        acc = _rms(acc, gf_ref[...])
    o_ref[...] = acc


def _ffn(x, g, w1, w2, g_final, *, tm, final_norm):
    t, d = x.shape
    f = w1.shape[1]
    assert t % tm == 0
    return pl.pallas_call(
        functools.partial(_ffn_kernel, f_chunk=1024, final_norm=final_norm),
        out_shape=jax.ShapeDtypeStruct((t, d), F32),
        grid=(t // tm,),
        in_specs=[pl.BlockSpec((tm, d), lambda i: (i, 0)), _const_spec((1, d)),
                  _const_spec((d, f)), _const_spec((f, d)), _const_spec((1, d))],
        out_specs=pl.BlockSpec((tm, d), lambda i: (i, 0)),
        compiler_params=_params(("parallel",)),
        name="ffn",
    )(x, g, w1, w2, g_final)


def _conv_taps(xe_ref, cw, cb, tl):
    nd = len(xe_ref.shape)

    def win(off):
        if nd == 3:
            return xe_ref[:, off:off + tl, :]
        return xe_ref[off:off + tl, :]

    base = CONV_PAD - (CONV_W - 1)
    y = cb + win(base) * cw[0:1, :]
    for k in range(1, CONV_W):
        y = y + win(base + k) * cw[k:k + 1, :]
    return y


def _lru_kernel(x_ref, g_ref, win_ref, cs_ref, h0_ref, cw_ref, cb_ref, wr_ref, br_ref, wi_ref, bi_ref, lam_ref, wo_ref,
                o_ref, cso_ref, ho_ref, xe_ref, hc_ref, *, nb, tl, start_pos):
    c = pl.program_id(1)
    rows = nb * tl
    d = x_ref.shape[1]
    blk = d // LRU_BLOCKS

    @pl.when(c == 0)
    def _():
        xe_ref[:, 0:CONV_PAD, :] = cs_ref[...]
        hc_ref[...] = h0_ref[...]

    x = x_ref[...]
    u = _rms(x, g_ref[...]).astype(BF16)
    xe_ref[:, CONV_PAD:CONV_PAD + tl, :] = jnp.dot(u, win_ref[:, 0:d], preferred_element_type=F32).reshape(nb, tl, d)
    xc = _conv_taps(xe_ref, cw_ref[...], cb_ref[...], tl).reshape(rows, d)
    tail = xe_ref[:, tl:tl + CONV_PAD, :]
    xe_ref[:, 0:CONV_PAD, :] = tail
    cso_ref[...] = tail

    xcb = xc.astype(BF16)
    r_pre = jnp.concatenate(
        [jnp.dot(xcb[:, j * blk:(j + 1) * blk], wr_ref[j], preferred_element_type=F32) for j in range(LRU_BLOCKS)],
        axis=1) + br_ref[...]
    i_pre = jnp.concatenate(
        [jnp.dot(xcb[:, j * blk:(j + 1) * blk], wi_ref[j], preferred_element_type=F32) for j in range(LRU_BLOCKS)],
        axis=1) + bi_ref[...]
    r = _sigmoid(r_pre)
    gi = _sigmoid(i_pre)
    log_a = (-LRU_C) * r * _softplus(-lam_ref[...])
    a = jnp.exp(log_a)
    m2 = -jnp.tanh(log_a) * (a * a + 1.0)
    mult = jnp.where(m2 > 0.0, m2 * lax.rsqrt(m2), 0.0)
    t_in = _row_index(rows, tl)
    if start_pos == 0:
        reset = (c * tl + t_in) == 0
        a = jnp.where(reset, 0.0, a)
        mult = jnp.where(reset, 1.0, mult)
    b = mult * (gi * xc)
    h_prev = jnp.broadcast_to(hc_ref[...], (nb, tl, d)).reshape(rows, d)
    b = b + jnp.where(t_in == 0, a * h_prev, 0.0)

    step = 1
    while step < tl:
        m = t_in >= step
        b = jnp.where(m, a * pltpu.roll(b, step, axis=0) + b, b)
        if 2 * step < tl:
            a = jnp.where(m, a * pltpu.roll(a, step, axis=0), a)
        step *= 2
    h_last = b.reshape(nb, tl, d)[:, tl - 1:tl, :]
    hc_ref[...] = h_last
    ho_ref[...] = h_last

    gate = jnp.dot(u, win_ref[:, d:2 * d], preferred_element_type=F32)
    y = (b * _gelu_tanh(gate)).astype(BF16)
    o_ref[...] = x + jnp.dot(y, wo_ref[...], preferred_element_type=F32)


def _lru_mixer(x, g, w_in, conv_state, h0, cw, cb, wr, br, wi, bi, lam, wo, *, batch, seq, nb, tl, start_pos):
    d = x.shape[1]
    nchunk = seq // tl
    assert batch % nb == 0 and seq % tl == 0 and (nb == 1 or nchunk == 1)
    rows = nb * tl
    row_map = lambda i, c: (i * nchunk + c, 0)
    seq_map = lambda i, c: (i, 0, 0)
    return pl.pallas_call(
        functools.partial(_lru_kernel, nb=nb, tl=tl, start_pos=start_pos),
        out_shape=[jax.ShapeDtypeStruct((batch * seq, d), F32),
                   jax.ShapeDtypeStruct((batch, CONV_PAD, d), F32),
                   jax.ShapeDtypeStruct((batch, 1, d), F32)],
        grid=(batch // nb, nchunk),
        in_specs=[pl.BlockSpec((rows, d), row_map), _const_spec(g.shape), _const_spec(w_in.shape),
                  pl.BlockSpec((nb, CONV_PAD, d), seq_map), pl.BlockSpec((nb, 1, d), seq_map),
                  _const_spec(cw.shape), _const_spec(cb.shape), _const_spec(wr.shape), _const_spec(br.shape),
                  _const_spec(wi.shape), _const_spec(bi.shape), _const_spec(lam.shape), _const_spec(wo.shape)],
        out_specs=[pl.BlockSpec((rows, d), row_map), pl.BlockSpec((nb, CONV_PAD, d), seq_map),
                   pl.BlockSpec((nb, 1, d), seq_map)],
        scratch_shapes=[pltpu.VMEM((nb, CONV_PAD + tl, d), F32), pltpu.VMEM((nb, 1, d), F32)],
        compiler_params=_params(("parallel", "arbitrary")),
        name="lru_mixer",
    )(x, g, w_in, conv_state, h0, cw, cb, wr, br, wi, bi, lam, wo)


def _split_bf16(x, terms):
    parts = []
    for _ in range(terms - 1):
        hi = x.astype(BF16).astype(F32)
        parts.append(hi)
        x = x - hi
    parts.append(x)
    return parts


def _ssd_kernel(x_ref, g_ref, pz_ref, pxbc_ref, pdt_ref, cs_ref, s0_ref, cw_ref, cb_ref, dtb_ref, alog_ref, dskx_ref, nw_ref,
                ex_ref, wo_ref, o_ref, cso_ref, so_ref, xe_ref, st_ref, *, q, zero_state, fused_proj):
    c = pl.program_id(1)
    d_ssm = pz_ref.shape[1]
    n = SSM_STATE
    hpg = d_ssm // SSM_HEADDIM // SSM_GROUPS
    gw = hpg * SSM_HEADDIM
    nblk = d_ssm // n

    @pl.when(c == 0)
    def _():
        xe_ref[0:CONV_PAD, :] = cs_ref[0]
        if zero_state:
            st_ref[...] = jnp.zeros(st_ref.shape, F32)
        else:
            for j in range(nblk):
                st_ref[:, j * n:(j + 1) * n] = s0_ref[0, j * n:(j + 1) * n, :].T

    x = x_ref[...]
    if fused_proj:
        u = _rms(x, g_ref[...]).astype(BF16)
        for c0 in range(0, pxbc_ref.shape[1], 1024):
            xe_ref[CONV_PAD:CONV_PAD + q, c0:c0 + 1024] = jnp.dot(u, pxbc_ref[:, c0:c0 + 1024],
                                                                  preferred_element_type=F32)
        dt_pre = jnp.dot(u, pdt_ref[...], preferred_element_type=F32)
    else:
        xe_ref[CONV_PAD:CONV_PAD + q, :] = pxbc_ref[...]
        dt_pre = pdt_ref[...]
    xbc = _silu(_conv_taps(xe_ref, cw_ref[...], cb_ref[...], q))
    tail = xe_ref[q:q + CONV_PAD, :]
    xe_ref[0:CONV_PAD, :] = tail
    cso_ref[0] = tail

    xs = xbc[:, 0:d_ssm]
    bm = xbc[:, d_ssm:d_ssm + SSM_GROUPS * n]
    cm = xbc[:, d_ssm + SSM_GROUPS * n:]
    dt = _softplus(dt_pre + dtb_ref[...])
    da = dt * (-jnp.exp(alog_ref[...]))
    t_in = _row_index(q, q)
    acs = _cumsum_rows(da, t_in, q)
    if q < V7X_LANES:
        acs_sq = jnp.concatenate([acs, jnp.zeros((V7X_LANES - q, V7X_LANES), F32)], axis=0)
    else:
        acs_sq = acs
    acs_t = acs_sq.T

    lhs = jnp.concatenate(_split_bf16(dt, 2) + _split_bf16(acs, 3), axis=0).astype(BF16)
    ex = jnp.dot(lhs, ex_ref[...], preferred_element_type=F32)
    dt_x = ex[0:q] + ex[q:2 * q]
    acs_x = ex[2 * q:3 * q] + ex[3 * q:4 * q] + ex[4 * q:5 * q]
    last_x = acs_x[q - 1:q, :]
    xdt = (xs * dt_x).astype(BF16)
    xdtd = (xs * (dt_x * jnp.exp(last_x - acs_x))).astype(BF16)

    tri = lax.broadcasted_iota(jnp.int32, (q, q), 0) >= lax.broadcasted_iota(jnp.int32, (q, q), 1)
    lane_head = lax.broadcasted_iota(jnp.int32, (1, gw), 1) // SSM_HEADDIM
    groups = range(SSM_GROUPS)
    bgs = [bm[:, g * n:(g + 1) * n].astype(BF16) for g in groups]
    cgs = [cm[:, g * n:(g + 1) * n].astype(BF16) for g in groups]
    cb_mats = [_bdot_nt(cgs[g], bgs[g]) for g in groups]
    y_off = [_bdot(cgs[g], st_ref[:, g * gw:(g + 1) * gw]) for g in groups]
    y_diag = []
    for g in groups:
        xdt_g = xdt[:, g * gw:(g + 1) * gw]
        acc = None
        for e in range(hpg):
            h = g * hpg + e
            seg = acs[:, h:h + 1] - acs_t[h:h + 1, 0:q]
            l_mat = jnp.where(tri, jnp.exp(jnp.minimum(seg, 0.0)), 0.0)
            part = jnp.dot((cb_mats[g] * l_mat).astype(BF16), jnp.where(lane_head == e, xdt_g, jnp.zeros_like(xdt_g)),
                           preferred_element_type=F32)
            acc = part if acc is None else acc + part
        y_diag.append(acc)
    st_new = [_bdot_tn(bgs[g], xdtd[:, g * gw:(g + 1) * gw]) for g in groups]
    e_last_x = jnp.exp(last_x)
    for g in groups:
        cols = slice(g * gw, (g + 1) * gw)
        st_ref[:, cols] = st_ref[:, cols] * e_last_x[:, cols] + st_new[g]

    @pl.when(c == pl.num_programs(1) - 1)
    def _():
        for j in range(nblk):
            so_ref[0, j * n:(j + 1) * n, :] = st_ref[:, j * n:(j + 1) * n].T

    y = jnp.concatenate(y_diag, axis=1) + jnp.concatenate(y_off, axis=1) * jnp.exp(acs_x) + dskx_ref[...] * xs
    y = y * _silu(jnp.dot(u, pz_ref[...], preferred_element_type=F32) if fused_proj else pz_ref[...])
    norm_parts = []
    for g in groups:
        yg = y[:, g * gw:(g + 1) * gw]
        norm_parts.append(yg * lax.rsqrt(jnp.mean(yg * yg, axis=-1, keepdims=True) + SSM_NORM_EPS))
    yn = (jnp.concatenate(norm_parts, axis=1) * nw_ref[...]).astype(BF16)
    o_ref[...] = x + jnp.dot(yn, wo_ref[...], preferred_element_type=F32)


def _ssd_mixer(x, g, w_z, w_xbc, w_dt, conv_state, s0, cw, cb, dtb, alog, dskx, nw, expand, wo, *, batch, seq, q,
               fused_proj):
    d = x.shape[1]
    d_ssm = w_z.shape[1]
    cdim = w_xbc.shape[1]
    nchunk = seq // q
    assert seq % q == 0
    zero_state = s0 is None
    if zero_state:
        s0 = jnp.zeros((1, d_ssm, SSM_STATE), F32)
    row_map = lambda b, c: (b * nchunk + c, 0)
    seq_map = lambda b, c: (b, 0, 0)
    s0_map = (lambda b, c: (0, 0, 0)) if zero_state else seq_map
    if fused_proj:
        proj = (w_z, w_xbc, w_dt)
        proj_specs = [_const_spec(w.shape) for w in proj]
    else:
        proj = _norm_proj(x, g, [w_z, w_xbc, w_dt], tm=min(256, batch * seq))
        proj_specs = [pl.BlockSpec((q, a.shape[1]), row_map) for a in proj]
    return pl.pallas_call(
        functools.partial(_ssd_kernel, q=q, zero_state=zero_state, fused_proj=fused_proj),
        out_shape=[jax.ShapeDtypeStruct((batch * seq, d), F32),
                   jax.ShapeDtypeStruct((batch, CONV_PAD, cdim), F32),
                   jax.ShapeDtypeStruct((batch, d_ssm, SSM_STATE), F32)],
        grid=(batch, nchunk),
        in_specs=[pl.BlockSpec((q, d), row_map), _const_spec(g.shape)] + proj_specs
        + [pl.BlockSpec((1, CONV_PAD, cdim), seq_map), pl.BlockSpec((1, d_ssm, SSM_STATE), s0_map),
                  _const_spec(cw.shape), _const_spec(cb.shape), _const_spec(dtb.shape), _const_spec(alog.shape),
                  _const_spec(dskx.shape), _const_spec(nw.shape), _const_spec(expand.shape), _const_spec(wo.shape)],
        out_specs=[pl.BlockSpec((q, d), row_map), pl.BlockSpec((1, CONV_PAD, cdim), seq_map),
                   pl.BlockSpec((1, d_ssm, SSM_STATE), seq_map)],
        scratch_shapes=[pltpu.VMEM((CONV_PAD + q, cdim), F32), pltpu.VMEM((SSM_STATE, d_ssm), F32)],
        compiler_params=_params(("parallel", "arbitrary")),
        name="ssd_mixer",
    )(x, g, *proj, conv_state, s0, cw, cb, dtb, alog, dskx, nw, expand, wo)


def _rwkv_proj_kernel(x_ref, sh_ref, g_ref, mu_ref, wrkv_ref, w0_ref, w1_ref, w2_ref, a0_ref, a1_ref, a2_ref,
                      g1_ref, g2_ref, kkw_ref, kaw_ref,
                      r_o, k_o, v_o, kk_o, bb_o, lw_o, g_o, sho_ref, carry_ref, *, nb, tl):
    c = pl.program_id(1)
    rows = nb * tl
    d = x_ref.shape[1]

    @pl.when(c == 0)
    def _():
        carry_ref[...] = sh_ref[...]

    u = _rms(x_ref[...], g_ref[...])
    t_in = _row_index(rows, tl)
    prev = jnp.where(t_in == 0, jnp.broadcast_to(carry_ref[...], (nb, tl, d)).reshape(rows, d),
                     pltpu.roll(u, 1, axis=0))
    last = u.reshape(nb, tl, d)[:, tl - 1:tl, :]
    carry_ref[...] = last
    sho_ref[...] = last

    diff = prev - u
    mu = mu_ref[...]

    def mixed(s):
        return (u + diff * mu[s:s + 1, :]).astype(BF16)

    r = jnp.dot(mixed(0), wrkv_ref[0], preferred_element_type=F32)
    k = jnp.dot(mixed(1), wrkv_ref[1], preferred_element_type=F32)
    v_o[...] = jnp.dot(mixed(2), wrkv_ref[2], preferred_element_type=F32)
    w_pre = w0_ref[...] + _bdot(jnp.tanh(jnp.dot(mixed(3), w1_ref[...], preferred_element_type=F32)), w2_ref[...])
    sp = jnp.maximum(-w_pre, 0.0) + jnp.log(1.0 + jnp.exp(-jnp.abs(w_pre)))
    lw_o[...] = -jnp.exp(-sp - 0.5)
    a = _sigmoid(a0_ref[...] + _bdot(jnp.dot(mixed(4), a1_ref[...], preferred_element_type=F32), a2_ref[...]))
    g_o[...] = _bdot(_sigmoid(jnp.dot(mixed(5), g1_ref[...], preferred_element_type=F32)), g2_ref[...])
    kk = k * kkw_ref[...]
    kk = kk * jnp.minimum(lax.rsqrt(_head_sum(kk * kk, RWKV_HEAD)), 1e12)
    r_o[...] = r
    k_o[...] = k * (1.0 + (a - 1.0) * kaw_ref[...])
    kk_o[...] = kk
    bb_o[...] = kk * a


def _rwkv_proj(x, shift0, g, mu, wrkv, w0, w1, w2, a0, a1, a2, g1, g2, kkw, kaw, *, batch, seq, nb, tl):
    d = x.shape[1]
    nchunk = seq // tl
    assert batch % nb == 0 and seq % tl == 0 and (nb == 1 or nchunk == 1)
    rows = nb * tl
    row_map = lambda i, c: (i * nchunk + c, 0)
    seq_map = lambda i, c: (i, 0, 0)
    consts = (g, mu, wrkv, w0, w1, w2, a0, a1, a2, g1, g2, kkw, kaw)
    return pl.pallas_call(
        functools.partial(_rwkv_proj_kernel, nb=nb, tl=tl),
        out_shape=[jax.ShapeDtypeStruct((batch * seq, d), F32)] * 7 + [jax.ShapeDtypeStruct((batch, 1, d), F32)],
        grid=(batch // nb, nchunk),
        in_specs=[pl.BlockSpec((rows, d), row_map), pl.BlockSpec((nb, 1, d), seq_map)]
        + [_const_spec(w.shape) for w in consts],
        out_specs=[pl.BlockSpec((rows, d), row_map)] * 7 + [pl.BlockSpec((nb, 1, d), seq_map)],
        scratch_shapes=[pltpu.VMEM((nb, 1, d), F32)],
        compiler_params=_params(("parallel", "arbitrary")),
        name="rwkv_proj",
    )(x, shift0, *consts)


def _unit_lower_solve(a_low, w, size):
    nmat = -a_low
    u = w + _bdot(nmat, w)
    m = 2
    while m < size:
        nmat = _bdot(nmat, nmat)
        u = u + _bdot(nmat, u)
        m *= 2
    return u


def _rwkv_chunk_kernel(r_ref, k_ref, v_ref, kk_ref, bb_ref, lw_ref, g_ref, x_ref, s0_ref, rk_ref, lnw_ref, lnb_ref,
                       wo_ref, o_ref, so_ref, s_ref, og_ref, *, nb, tl, chunk, zero_state):
    c = pl.program_id(1)
    d = x_ref.shape[1]
    hd = RWKV_HEAD
    heads = d // hd
    cpb = tl // chunk
    n_iter = nb * cpb

    @pl.when(c == 0)
    def _():
        if zero_state:
            s_ref[...] = jnp.zeros(s_ref.shape, F32)
        else:
            s_ref[...] = s0_ref[...]

    t_in = _row_index(chunk, chunk)
    ri = lax.broadcasted_iota(jnp.int32, (2 * chunk, 2 * chunk), 0)
    ci = lax.broadcasted_iota(jnp.int32, (2 * chunk, 2 * chunk), 1) & (chunk - 1)
    blk_mask = jnp.where(ri < chunk, ri - 1, ri - chunk) >= ci

    def body(it, carry):
        seq_i = it if cpb == 1 else 0
        rows = pl.ds(pl.multiple_of(it * chunk, chunk), chunk)
        r = r_ref[rows, :]
        k = k_ref[rows, :]
        v = v_ref[rows, :]
        kk = kk_ref[rows, :]
        bb = bb_ref[rows, :]
        lw = lw_ref[rows, :]
        cum = _cumsum_rows(lw, t_in, chunk)
        c_last = cum[chunk - 1:chunk, :]
        g_last = jnp.exp(c_last)
        e_inv = jnp.exp(-cum)
        e_end = jnp.exp(c_last - cum)
        p_st = jnp.concatenate([kk * jnp.exp(cum - lw), r * jnp.exp(cum)], axis=0).astype(BF16)
        q_st = jnp.concatenate([bb * e_inv, k * e_inv], axis=0).astype(BF16)
        e_st = jnp.concatenate([k * e_end, bb * e_end], axis=0).astype(BF16)
        zv_st = jnp.concatenate([jnp.zeros_like(v), v], axis=0).astype(BF16)
        hs = range(heads)
        sls = [slice(h * hd, (h + 1) * hd) for h in hs]
        s_l = [s_ref[seq_i, sl, :] for sl in sls]
        g_m = [jnp.where(blk_mask, _bdot_nt(p_st[:, sl], q_st[:, sl]), 0.0) for sl in sls]
        ps = [_bdot_nt(p_st[:, sls[h]], s_l[h]) for h in hs]
        t1 = [_bdot(g_m[h], zv_st[:, sls[h]]) for h in hs]
        nmat = [-g_m[h][0:chunk, 0:chunk] for h in hs]
        w_l = [ps[h][0:chunk] + t1[h][0:chunk] for h in hs]
        u_l = [w_l[h] + _bdot(nmat[h], w_l[h]) for h in hs]
        m = 2
        while m < chunk:
            nmat = [_bdot(x, x) for x in nmat]
            u_l = [u_l[h] + _bdot(nmat[h], u_l[h]) for h in hs]
            m *= 2
        o_parts = [ps[h][chunk:] + t1[h][chunk:] - _bdot(g_m[h][chunk:, 0:chunk], u_l[h]) for h in hs]
        for h in hs:
            vu = jnp.concatenate([v[:, sls[h]], -u_l[h]], axis=0)
            s_ref[seq_i, sls[h], :] = s_l[h] * g_last[:, sls[h]] + _bdot_tn(vu, e_st[:, sls[h]])
        o = jnp.concatenate(o_parts, axis=1)
        mean = _head_sum(o, hd) * (1.0 / hd)
        oc = o - mean
        var = _head_sum(oc * oc, hd) * (1.0 / hd)
        on = oc * lax.rsqrt(var + GN_EPS) * lnw_ref[...] + lnb_ref[...]
        bonus = _head_sum(r * k * rk_ref[...], hd) * v
        og_ref[rows, :] = (on + bonus) * g_ref[rows, :]
        return carry

    lax.fori_loop(0, n_iter, body, 0)

    @pl.when(c == pl.num_programs(1) - 1)
    def _():
        so_ref[...] = s_ref[...]

    o_ref[...] = x_ref[...] + _bdot(og_ref[...], wo_ref[...])


def _rwkv_mixer(x, r, k, v, kk, bb, lw, g, s0, rk, lnw, lnb, wo, *, batch, seq, nb, tl, chunk):
    d = x.shape[1]
    nchunk = seq // tl
    assert batch % nb == 0 and seq % tl == 0 and tl % chunk == 0 and (nb == 1 or tl == chunk)
    rows = nb * tl
    zero_state = s0 is None
    if zero_state:
        s0 = jnp.zeros((nb, d, RWKV_HEAD), F32)
    row_map = lambda i, c: (i * nchunk + c, 0)
    seq_map = lambda i, c: (i, 0, 0)
    s0_map = (lambda i, c: (0, 0, 0)) if zero_state else seq_map
    row_spec = pl.BlockSpec((rows, d), row_map)
    return pl.pallas_call(
        functools.partial(_rwkv_chunk_kernel, nb=nb, tl=tl, chunk=chunk, zero_state=zero_state),
        out_shape=[jax.ShapeDtypeStruct((batch * seq, d), F32), jax.ShapeDtypeStruct((batch, d, RWKV_HEAD), F32)],
        grid=(batch // nb, nchunk),
        in_specs=[row_spec] * 8 + [pl.BlockSpec((nb, d, RWKV_HEAD), s0_map),
                                   _const_spec(rk.shape), _const_spec(lnw.shape), _const_spec(lnb.shape),
                                   _const_spec(wo.shape)],
        out_specs=[row_spec, pl.BlockSpec((nb, d, RWKV_HEAD), seq_map)],
        scratch_shapes=[pltpu.VMEM((nb, d, RWKV_HEAD), F32), pltpu.VMEM((rows, d), F32)],
        compiler_params=_params(("parallel", "arbitrary")),
        name="rwkv_chunk",
    )(r, k, v, kk, bb, lw, g, x, s0, rk, lnw, lnb, wo)


def _pad_conv_state(state):
    return jnp.pad(state, ((0, 0), (CONV_PAD - (CONV_W - 1), 0), (0, 0)))


def _trunk(x3, states, start_pos, p, cfg):
    batch, seq, d = x3.shape
    x = x3.reshape(batch * seq, d)
    depth = p["norm_mix"].shape[0]
    new_lc, new_lh, new_sc, new_ss, new_rs, new_rw = [], [], [], [], [], []
    ia = ib = ic = 0
    for layer in range(depth):
        g_mix = p["norm_mix"][layer][None, :]
        kind = layer % 3
        if kind == 0:
            cs = states["lru_conv"][ia] if states else jnp.zeros((batch, CONV_W - 1, d), F32)
            h0 = states["lru_h"][ia] if states else jnp.zeros((batch, d), F32)
            x, cso, ho = _lru_mixer(
                x, g_mix, p["lru_w_in"][ia], _pad_conv_state(cs), h0[:, None, :],
                p["lru_conv_w"][ia], p["lru_conv_b"][ia][None, :], p["lru_w_r"][ia], p["lru_b_r"][ia][None, :],
                p["lru_w_i"][ia], p["lru_b_i"][ia][None, :], p["lru_lambda"][ia][None, :], p["lru_w_out"][ia],
                batch=batch, seq=seq, nb=cfg["lru_nb"], tl=cfg["lru_tl"], start_pos=start_pos)
            new_lc.append(cso[:, CONV_PAD - (CONV_W - 1):, :])
            new_lh.append(ho[:, 0, :])
            ia += 1
        elif kind == 1:
            cdim = p["ssm_w_xbc"].shape[2]
            cs = states["ssm_conv"][ib] if states else jnp.zeros((batch, CONV_W - 1, cdim), F32)
            s0 = states["ssm"][ib].reshape(batch, -1, SSM_STATE) if states else None
            x, cso, so = _ssd_mixer(
                x, g_mix, p["ssm_w_z"][ib], p["ssm_w_xbc"][ib], p["ssm_w_dt"][ib], _pad_conv_state(cs), s0,
                p["ssm_conv_w"][ib], p["ssm_conv_b"][ib][None, :], p["ssm_dt_bias"][ib], p["ssm_a_log"][ib],
                p["ssm_d_x"][ib], p["ssm_norm_w"][ib][None, :], p["ssm_expand"], p["ssm_w_out"][ib],
                batch=batch, seq=seq, q=cfg["ssd_q"], fused_proj=cfg["ssd_fused"])
            new_sc.append(cso[:, CONV_PAD - (CONV_W - 1):, :])
            new_ss.append(so.reshape(batch, -1, SSM_HEADDIM, SSM_STATE))
            ib += 1
        else:
            sh0 = states["rwkv_shift"][ic] if states else jnp.zeros((batch, d), F32)
            s0 = states["rwkv_wkv"][ic].reshape(batch, d, RWKV_HEAD) if states else None
            r, k, v, kk, bb, lw, g, sho = _rwkv_proj(
                x, sh0[:, None, :], g_mix, p["rwkv_mu"][ic], p["rwkv_w_rkv"][ic], p["rwkv_w0"][ic][None, :],
                p["rwkv_w_w1"][ic], p["rwkv_w_w2"][ic], p["rwkv_a0"][ic][None, :], p["rwkv_w_a1"][ic],
                p["rwkv_w_a2"][ic], p["rwkv_w_g1"][ic], p["rwkv_w_g2"][ic], p["rwkv_k_k"][ic][None, :],
                p["rwkv_k_a"][ic][None, :], batch=batch, seq=seq, nb=cfg["rwkv_proj_nb"], tl=cfg["rwkv_tl"])
            x, so = _rwkv_mixer(
                x, r, k, v, kk, bb, lw, g, s0, p["rwkv_r_k"][ic].reshape(1, d), p["rwkv_lnx_w"][ic][None, :],
                p["rwkv_lnx_b"][ic][None, :], p["rwkv_w_out"][ic],
                batch=batch, seq=seq, nb=cfg["rwkv_nb"], tl=cfg["rwkv_tl"], chunk=cfg["rwkv_chunk"])
            new_rs.append(sho[:, 0, :])
            new_rw.append(so.reshape(batch, d // RWKV_HEAD, RWKV_HEAD, RWKV_HEAD))
            ic += 1
        x = _ffn(x, p["norm_ffn"][layer][None, :], p["ffn_w1"][layer], p["ffn_w2"][layer],
                 p["norm_final"][None, :], tm=cfg["tm_ffn"], final_norm=(layer == depth - 1))
    stack = lambda parts: parts[0][None] if len(parts) == 1 else jnp.stack(parts)
    return (x.reshape(batch, seq, d), stack(new_lc), stack(new_lh), stack(new_sc), stack(new_ss),
            stack(new_rs), stack(new_rw))


def _prep_params(p):
    q = dict(p)
    for name in ("lru_w_in", "lru_w_r", "lru_w_i", "lru_w_out", "ssm_w_out", "rwkv_w_rkv", "rwkv_w_w1", "rwkv_w_w2",
                 "rwkv_w_a1", "rwkv_w_a2", "rwkv_w_g1", "rwkv_w_g2", "rwkv_w_out", "ffn_w1", "ffn_w2"):
        q[name] = p[name].astype(BF16)
    d_ssm = p["ssm_w_out"].shape[1]
    heads = p["ssm_dt_bias"].shape[1]
    w_in = p["ssm_w_in"]
    cdim = w_in.shape[2] - d_ssm - heads
    pad = V7X_LANES - heads
    q["ssm_w_z"] = w_in[:, :, :d_ssm].astype(BF16)
    q["ssm_w_xbc"] = w_in[:, :, d_ssm:d_ssm + cdim].astype(BF16)
    q["ssm_w_dt"] = jnp.pad(w_in[:, :, d_ssm + cdim:], ((0, 0), (0, 0), (0, pad))).astype(BF16)
    for name in ("ssm_dt_bias", "ssm_a_log"):
        q[name] = jnp.pad(p[name], ((0, 0), (0, pad)))[:, None, :]
    q["ssm_d_x"] = jnp.repeat(p["ssm_d"], SSM_HEADDIM, axis=1)[:, None, :]
    q["ssm_expand"] = (jnp.arange(V7X_LANES)[:, None] == jnp.arange(d_ssm)[None, :] // SSM_HEADDIM).astype(BF16)
    return q


def kernel(x_prompt, x_sample, state_lru_conv, state_lru_h, state_ssm_conv, state_ssm, state_rwkv_shift, state_rwkv_wkv, norm_mix, norm_ffn, norm_final, lru_w_in, lru_conv_w, lru_conv_b, lru_w_r, lru_b_r, lru_w_i, lru_b_i, lru_lambda, lru_w_out, ssm_w_in, ssm_conv_w, ssm_conv_b, ssm_dt_bias, ssm_a_log, ssm_d, ssm_norm_w, ssm_w_out, rwkv_mu, rwkv_w_rkv, rwkv_w0, rwkv_w_w1, rwkv_w_w2, rwkv_a0, rwkv_w_a1, rwkv_w_a2, rwkv_w_g1, rwkv_w_g2, rwkv_k_k, rwkv_k_a, rwkv_r_k, rwkv_lnx_w, rwkv_lnx_b, rwkv_w_out, ffn_w1, ffn_w2):
    p = _prep_params(dict(
        norm_mix=norm_mix, norm_ffn=norm_ffn, norm_final=norm_final,
        lru_w_in=lru_w_in, lru_conv_w=lru_conv_w, lru_conv_b=lru_conv_b, lru_w_r=lru_w_r, lru_b_r=lru_b_r,
        lru_w_i=lru_w_i, lru_b_i=lru_b_i, lru_lambda=lru_lambda, lru_w_out=lru_w_out,
        ssm_w_in=ssm_w_in, ssm_conv_w=ssm_conv_w, ssm_conv_b=ssm_conv_b, ssm_dt_bias=ssm_dt_bias,
        ssm_a_log=ssm_a_log, ssm_d=ssm_d, ssm_norm_w=ssm_norm_w, ssm_w_out=ssm_w_out,
        rwkv_mu=rwkv_mu, rwkv_w_rkv=rwkv_w_rkv, rwkv_w0=rwkv_w0, rwkv_w_w1=rwkv_w_w1, rwkv_w_w2=rwkv_w_w2,
        rwkv_a0=rwkv_a0, rwkv_w_a1=rwkv_w_a1, rwkv_w_a2=rwkv_w_a2, rwkv_w_g1=rwkv_w_g1, rwkv_w_g2=rwkv_w_g2,
        rwkv_k_k=rwkv_k_k, rwkv_k_a=rwkv_k_a, rwkv_r_k=rwkv_r_k, rwkv_lnx_w=rwkv_lnx_w, rwkv_lnx_b=rwkv_lnx_b,
        rwkv_w_out=rwkv_w_out, ffn_w1=ffn_w1, ffn_w2=ffn_w2))
    seq_p = x_prompt.shape[1]
    seq_s = x_sample.shape[1]
    cfg_p = dict(tm_ffn=512, lru_nb=1, lru_tl=256, ssd_q=min(SSM_CHUNK, seq_p), ssd_fused=True,
                 rwkv_proj_nb=1, rwkv_nb=1, rwkv_tl=256, rwkv_chunk=RWKV_CHUNK)
    cfg_s = dict(tm_ffn=512, lru_nb=32, lru_tl=seq_s, ssd_q=seq_s, ssd_fused=False,
                 rwkv_proj_nb=32, rwkv_nb=8, rwkv_tl=seq_s, rwkv_chunk=seq_s)
    y_p, lc_p, lh_p, sc_p, ss_p, rs_p, rw_p = _trunk(x_prompt, None, 0, p, cfg_p)
    states = dict(lru_conv=state_lru_conv, lru_h=state_lru_h, ssm_conv=state_ssm_conv, ssm=state_ssm,
                  rwkv_shift=state_rwkv_shift, rwkv_wkv=state_rwkv_wkv)
    y_s, lc_s, lh_s, sc_s, ss_s, rs_s, rw_s = _trunk(x_sample, states, PAST_LEN, p, cfg_s)
    return (y_p, y_s, lc_p, lc_s, lh_p, lh_s, sc_p, sc_s, ss_p, ss_s, rs_p, rs_s, rw_p, rw_s)
```

```python
import functools
import math

import jax
import jax.numpy as jnp
from jax import lax
from jax.experimental import pallas as pl
from jax.experimental.pallas import tpu as pltpu

F32 = jnp.float32
BF16 = jnp.bfloat16

NORM_EPS = 1e-6
CONV_W = 4
LRU_BLOCKS = 8
LRU_C = 8.0
SSM_HEADDIM = 64
SSM_GROUPS = 8
SSM_STATE = 128
SSM_CHUNK = 128
SSM_NORM_EPS = 1e-5
RWKV_HEAD = 64
GN_EPS = 64e-5
PAST_LEN = 16384

V7X_LANES = 128
V7X_SUBLANES = 8
V7X_VMEM_BYTES = 64 * 1024 * 1024
VMEM_LIMIT = V7X_VMEM_BYTES - 8 * 1024 * 1024

RWKV_CHUNK = 64
CONV_PAD = V7X_SUBLANES


def _bdot(a, b):
    return jnp.dot(a.astype(BF16), b.astype(BF16), preferred_element_type=F32)


def _bdot_nt(a, b):
    return lax.dot_general(a.astype(BF16), b.astype(BF16), (((1,), (1,)), ((), ())),
                           preferred_element_type=F32)


def _bdot_tn(a, b):
    return lax.dot_general(a.astype(BF16), b.astype(BF16), (((0,), (0,)), ((), ())),
                           preferred_element_type=F32)


def _rms(x, g):
    return x * lax.rsqrt(jnp.mean(x * x, axis=-1, keepdims=True) + NORM_EPS) * g


def _sigmoid(x):
    return 0.5 * jnp.tanh(0.5 * x) + 0.5


def _softplus(x):
    return jnp.maximum(x, 0.0) + jnp.log1p(jnp.exp(-jnp.abs(x)))


def _silu(x):
    return x * _sigmoid(x)


def _gelu_tanh(x):
    c = math.sqrt(2.0 / math.pi)
    return 0.5 * x * (1.0 + jnp.tanh(c * (x + 0.044715 * (x * x * x))))


def _row_index(rows, period):
    return lax.broadcasted_iota(jnp.int32, (rows, 1), 0) & (period - 1)


def _cumsum_rows(x, t_in, period):
    d = 1
    while d < period:
        x = x + jnp.where(t_in >= d, pltpu.roll(x, d, axis=0), 0.0)
        d *= 2
    return x


def _head_sum(x, head):
    rows, width = x.shape
    lo = lax.broadcasted_iota(jnp.int32, (1, V7X_LANES), 1) < head
    parts = []
    for j in range(width // V7X_LANES):
        seg = x[:, j * V7X_LANES:(j + 1) * V7X_LANES]
        s_lo = jnp.sum(jnp.where(lo, seg, 0.0), axis=-1, keepdims=True)
        s_hi = jnp.sum(jnp.where(lo, 0.0, seg), axis=-1, keepdims=True)
        parts.append(jnp.where(lo, s_lo, s_hi))
    return jnp.concatenate(parts, axis=1)


def _const_spec(shape):
    nd = len(shape)
    return pl.BlockSpec(shape, lambda *_: (0,) * nd, pipeline_mode=pl.Buffered(1))


def _params(sem):
    return pltpu.CompilerParams(dimension_semantics=sem, vmem_limit_bytes=VMEM_LIMIT)


def _proj_kernel(x_ref, g_ref, *refs, n_out, col_chunk):
    w_refs, o_refs = refs[:n_out], refs[n_out:]
    u = _rms(x_ref[...], g_ref[...]).astype(BF16)
    for w_ref, o_ref in zip(w_refs, o_refs):
        n = w_ref.shape[1]
        step = min(col_chunk, n)
        for c0 in range(0, n, step):
            o_ref[:, c0:c0 + step] = jnp.dot(u, w_ref[:, c0:c0 + step], preferred_element_type=F32)


def _norm_proj(x, g, weights, *, tm):
    t, d = x.shape
    assert t % tm == 0
    n_out = len(weights)
    return pl.pallas_call(
        functools.partial(_proj_kernel, n_out=n_out, col_chunk=1024),
        out_shape=[jax.ShapeDtypeStruct((t, w.shape[1]), F32) for w in weights],
        grid=(t // tm,),
        in_specs=[pl.BlockSpec((tm, d), lambda i: (i, 0)), _const_spec((1, d))]
        + [_const_spec(w.shape) for w in weights],
        out_specs=[pl.BlockSpec((tm, w.shape[1]), lambda i: (i, 0)) for w in weights],
        compiler_params=_params(("parallel",)),
        name="norm_proj",
    )(x, g, *weights)


def _ffn_kernel(x_ref, g_ref, w1_ref, w2_ref, gf_ref, o_ref, *, f_chunk, final_norm):
    x = x_ref[...]
    u = _rms(x, g_ref[...]).astype(BF16)
    acc = x
    for c0 in range(0, w1_ref.shape[1], f_chunk):
        h = jnp.dot(u, w1_ref[:, c0:c0 + f_chunk], preferred_element_type=F32)
        h = jnp.square(jnp.maximum(h, 0.0)).astype(BF16)
        acc = acc + jnp.dot(h, w2_ref[c0:c0 + f_chunk, :], preferred_element_type=F32)
    if final_norm:
        acc = _rms(acc, gf_ref[...])
    o_ref[...] = acc


def _ffn(x, g, w1, w2, g_final, *, tm, final_norm):
    t, d = x.shape
    f = w1.shape[1]
    assert t % tm == 0
    return pl.pallas_call(
        functools.partial(_ffn_kernel, f_chunk=1024, final_norm=final_norm),
        out_shape=jax.ShapeDtypeStruct((t, d), F32),
        grid=(t // tm,),
        in_specs=[pl.BlockSpec((tm, d), lambda i: (i, 0)), _const_spec((1, d)),
                  _const_spec((d, f)), _const_spec((f, d)), _const_spec((1, d))],
        out_specs=pl.BlockSpec((tm, d), lambda i: (i, 0)),
        compiler_params=_params(("parallel",)),
        name="ffn",
    )(x, g, w1, w2, g_final)


def _conv_taps(xe_ref, cw, cb, tl):
    nd = len(xe_ref.shape)

    def win(off):
        if nd == 3:
            return xe_ref[:, off:off + tl, :]
        return xe_ref[off:off + tl, :]

    base = CONV_PAD - (CONV_W - 1)
    y = cb + win(base) * cw[0:1, :]
    for k in range(1, CONV_W):
        y = y + win(base + k) * cw[k:k + 1, :]
    return y


def _lru_kernel(x_ref, g_ref, win_ref, cs_ref, h0_ref, cw_ref, cb_ref, wr_ref, br_ref, wi_ref, bi_ref, lam_ref, wo_ref,
                o_ref, cso_ref, ho_ref, xe_ref, hs_ref, hc_ref, *, nb, tl, start_pos):
    c = pl.program_id(1)
    rows = nb * tl
    d = x_ref.shape[2]
    blk = d // LRU_BLOCKS
    ncol = d // V7X_LANES
    pitch = CONV_PAD + tl
    cols = [slice(j * V7X_LANES, (j + 1) * V7X_LANES) for j in range(ncol)]

    @pl.when(c == 0)
    def _():
        for j in range(ncol):
            for s in range(nb):
                xe_ref[j, s * pitch:s * pitch + CONV_PAD, :] = cs_ref[s, :, cols[j]]
        hc_ref[...] = h0_ref[...]

    x = x_ref[...].reshape(rows, d)
    u = _rms(x, g_ref[...]).astype(BF16)
    xb = jnp.dot(u, win_ref[:, 0:d], preferred_element_type=F32)
    for j in range(ncol):
        for s in range(nb):
            xe_ref[j, s * pitch + CONV_PAD:(s + 1) * pitch, :] = xb[s * tl:(s + 1) * tl, cols[j]]

    cw = cw_ref[...]
    cb = cb_ref[...]
    base = CONV_PAD - (CONV_W - 1)
    slabs = []
    for t in range(tl):
        parts = []
        for j in range(ncol):
            acc = cb[:, cols[j]] + xe_ref[j, pl.ds(base + t, nb, stride=pitch), :] * cw[0:1, cols[j]]
            for k in range(1, CONV_W):
                acc = acc + xe_ref[j, pl.ds(base + t + k, nb, stride=pitch), :] * cw[k:k + 1, cols[j]]
            parts.append(acc)
        slabs.append(jnp.concatenate(parts, axis=1))
    xc = jnp.concatenate(slabs, axis=0)
    for j in range(ncol):
        for s in range(nb):
            tail = xe_ref[j, s * pitch + tl:(s + 1) * pitch, :]
            xe_ref[j, s * pitch:s * pitch + CONV_PAD, :] = tail
            cso_ref[s, :, cols[j]] = tail

    xcb = xc.astype(BF16)
    r_pre = jnp.concatenate(
        [jnp.dot(xcb[:, j * blk:(j + 1) * blk], wr_ref[j], preferred_element_type=F32) for j in range(LRU_BLOCKS)],
        axis=1) + br_ref[...]
    i_pre = jnp.concatenate(
        [jnp.dot(xcb[:, j * blk:(j + 1) * blk], wi_ref[j], preferred_element_type=F32) for j in range(LRU_BLOCKS)],
        axis=1) + bi_ref[...]
    r = _sigmoid(r_pre)
    gi = _sigmoid(i_pre)
    log_a = (-LRU_C) * r * _softplus(-lam_ref[...])
    a = jnp.exp(log_a)
    m2 = -jnp.tanh(log_a) * (a * a + 1.0)
    mult = jnp.where(m2 > 0.0, m2 * lax.rsqrt(m2), 0.0)
    if start_pos == 0:
        reset = (lax.broadcasted_iota(jnp.int32, (rows, 1), 0) < nb) & (c == 0)
        a = jnp.where(reset, 0.0, a)
        mult = jnp.where(reset, 1.0, mult)
    b = mult * (gi * xc)

    h = hc_ref[...]
    for t in range(tl):
        h = a[t * nb:(t + 1) * nb] * h + b[t * nb:(t + 1) * nb]
        for j in range(ncol):
            hs_ref[j, pl.ds(t, nb, stride=pitch), :] = h[:, cols[j]]
    hc_ref[...] = h
    ho_ref[...] = h
    h_bm = jnp.concatenate(
        [jnp.concatenate([hs_ref[j, s * pitch:s * pitch + tl, :] for s in range(nb)], axis=0) for j in range(ncol)],
        axis=1)

    gate = jnp.dot(u, win_ref[:, d:2 * d], preferred_element_type=F32)
    y = (h_bm * _gelu_tanh(gate)).astype(BF16)
    o_ref[...] = (x + jnp.dot(y, wo_ref[...], preferred_element_type=F32)).reshape(nb, tl, d)


def _lru_mixer(x3, g, w_in, conv_state, h0, cw, cb, wr, br, wi, bi, lam, wo, *, nb, tl, start_pos):
    batch, seq, d = x3.shape
    nchunk = seq // tl
    assert batch % nb == 0 and seq % tl == 0 and nb % V7X_SUBLANES == 0 and tl % V7X_SUBLANES == 0
    ncol = d // V7X_LANES
    blk_map = lambda i, c: (i, c, 0)
    seq_map = lambda i, c: (i, 0, 0)
    row_map = lambda i, c: (i, 0)
    return pl.pallas_call(
        functools.partial(_lru_kernel, nb=nb, tl=tl, start_pos=start_pos),
        out_shape=[jax.ShapeDtypeStruct((batch, seq, d), F32),
                   jax.ShapeDtypeStruct((batch, CONV_PAD, d), F32),
                   jax.ShapeDtypeStruct((batch, d), F32)],
        grid=(batch // nb, nchunk),
        in_specs=[pl.BlockSpec((nb, tl, d), blk_map), _const_spec(g.shape), _const_spec(w_in.shape),
                  pl.BlockSpec((nb, CONV_PAD, d), seq_map), pl.BlockSpec((nb, d), row_map),
                  _const_spec(cw.shape), _const_spec(cb.shape), _const_spec(wr.shape), _const_spec(br.shape),
                  _const_spec(wi.shape), _const_spec(bi.shape), _const_spec(lam.shape), _const_spec(wo.shape)],
        out_specs=[pl.BlockSpec((nb, tl, d), blk_map), pl.BlockSpec((nb, CONV_PAD, d), seq_map),
                   pl.BlockSpec((nb, d), row_map)],
        scratch_shapes=[pltpu.VMEM((ncol, nb * (CONV_PAD + tl), V7X_LANES), F32),
                        pltpu.VMEM((ncol, nb * (CONV_PAD + tl), V7X_LANES), F32),
                        pltpu.VMEM((nb, d), F32)],
        compiler_params=_params(("parallel", "arbitrary")),
        name="lru_mixer",
    )(x3, g, w_in, conv_state, h0, cw, cb, wr, br, wi, bi, lam, wo)


def _split_bf16(x, terms):
    parts = []
    for _ in range(terms - 1):
        hi = x.astype(BF16).astype(F32)
        parts.append(hi)
        x = x - hi
    parts.append(x)
    return parts


def _ssd_kernel(x_ref, g_ref, pz_ref, pxbc_ref, pdt_ref, cs_ref, s0_ref, cw_ref, cb_ref, dtb_ref, alog_ref, dskx_ref, nw_ref,
                ex_ref, wo_ref, o_ref, cso_ref, so_ref, xe_ref, st_ref, *, q, zero_state, fused_proj):
    c = pl.program_id(1)
    d_ssm = pz_ref.shape[1]
    n = SSM_STATE
    hpg = d_ssm // SSM_HEADDIM // SSM_GROUPS
    gw = hpg * SSM_HEADDIM
    nblk = d_ssm // n

    @pl.when(c == 0)
    def _():
        xe_ref[0:CONV_PAD, :] = cs_ref[0]
        if zero_state:
            st_ref[...] = jnp.zeros(st_ref.shape, F32)
        else:
            for j in range(nblk):
                st_ref[:, j * n:(j + 1) * n] = s0_ref[0, j * n:(j + 1) * n, :].T

    x = x_ref[...]
    if fused_proj:
        u = _rms(x, g_ref[...]).astype(BF16)
        for c0 in range(0, pxbc_ref.shape[1], 1024):
            xe_ref[CONV_PAD:CONV_PAD + q, c0:c0 + 1024] = jnp.dot(u, pxbc_ref[:, c0:c0 + 1024],
                                                                  preferred_element_type=F32)
        dt_pre = jnp.dot(u, pdt_ref[...], preferred_element_type=F32)
    else:
        xe_ref[CONV_PAD:CONV_PAD + q, :] = pxbc_ref[...]
        dt_pre = pdt_ref[...]
    xbc = _silu(_conv_taps(xe_ref, cw_ref[...], cb_ref[...], q))
    tail = xe_ref[q:q + CONV_PAD, :]
    xe_ref[0:CONV_PAD, :] = tail
    cso_ref[0] = tail

    xs = xbc[:, 0:d_ssm]
    bm = xbc[:, d_ssm:d_ssm + SSM_GROUPS * n]
    cm = xbc[:, d_ssm + SSM_GROUPS * n:]
    dt = _softplus(dt_pre + dtb_ref[...])
    da = dt * (-jnp.exp(alog_ref[...]))
    t_in = _row_index(q, q)
    acs = _cumsum_rows(da, t_in, q)
    if q < V7X_LANES:
        acs_sq = jnp.concatenate([acs, jnp.zeros((V7X_LANES - q, V7X_LANES), F32)], axis=0)
    else:
        acs_sq = acs
    acs_t = acs_sq.T

    lhs = jnp.concatenate(_split_bf16(dt, 2) + _split_bf16(acs, 3), axis=0).astype(BF16)
    ex = jnp.dot(lhs, ex_ref[...], preferred_element_type=F32)
    dt_x = ex[0:q] + ex[q:2 * q]
    acs_x = ex[2 * q:3 * q] + ex[3 * q:4 * q] + ex[4 * q:5 * q]
    last_x = acs_x[q - 1:q, :]
    xdt = (xs * dt_x).astype(BF16)
    xdtd = (xs * (dt_x * jnp.exp(last_x - acs_x))).astype(BF16)

    tri = lax.broadcasted_iota(jnp.int32, (q, q), 0) >= lax.broadcasted_iota(jnp.int32, (q, q), 1)
    lane_head = lax.broadcasted_iota(jnp.int32, (1, gw), 1) // SSM_HEADDIM
    groups = range(SSM_GROUPS)
    bgs = [bm[:, g * n:(g + 1) * n].astype(BF16) for g in groups]
    cgs = [cm[:, g * n:(g + 1) * n].astype(BF16) for g in groups]
    cb_mats = [_bdot_nt(cgs[g], bgs[g]) for g in groups]
    y_off = [_bdot(cgs[g], st_ref[:, g * gw:(g + 1) * gw]) for g in groups]
    y_diag = []
    for g in groups:
        xdt_g = xdt[:, g * gw:(g + 1) * gw]
        acc = None
        for e in range(hpg):
            h = g * hpg + e
            seg = acs[:, h:h + 1] - acs_t[h:h + 1, 0:q]
            l_mat = jnp.exp(jnp.where(tri, seg, -jnp.inf))
            part = jnp.dot((cb_mats[g] * l_mat).astype(BF16), jnp.where(lane_head == e, xdt_g, jnp.zeros_like(xdt_g)),
                           preferred_element_type=F32)
            acc = part if acc is None else acc + part
        y_diag.append(acc)
    st_new = [_bdot_tn(bgs[g], xdtd[:, g * gw:(g + 1) * gw]) for g in groups]
    e_last_x = jnp.exp(last_x)
    for g in groups:
        cols = slice(g * gw, (g + 1) * gw)
        st_ref[:, cols] = st_ref[:, cols] * e_last_x[:, cols] + st_new[g]

    @pl.when(c == pl.num_programs(1) - 1)
    def _():
        for j in range(nblk):
            so_ref[0, j * n:(j + 1) * n, :] = st_ref[:, j * n:(j + 1) * n].T

    y = jnp.concatenate(y_diag, axis=1) + jnp.concatenate(y_off, axis=1) * jnp.exp(acs_x) + dskx_ref[...] * xs
    y = y * _silu(jnp.dot(u, pz_ref[...], preferred_element_type=F32) if fused_proj else pz_ref[...])
    norm_parts = []
    for g in groups:
        yg = y[:, g * gw:(g + 1) * gw]
        norm_parts.append(yg * lax.rsqrt(jnp.mean(yg * yg, axis=-1, keepdims=True) + SSM_NORM_EPS))
    yn = (jnp.concatenate(norm_parts, axis=1) * nw_ref[...]).astype(BF16)
    o_ref[...] = x + jnp.dot(yn, wo_ref[...], preferred_element_type=F32)


def _ssd_mixer(x, g, w_z, w_xbc, w_dt, conv_state, s0, cw, cb, dtb, alog, dskx, nw, expand, wo, *, batch, seq, q,
               fused_proj):
    d = x.shape[1]
    d_ssm = w_z.shape[1]
    cdim = w_xbc.shape[1]
    nchunk = seq // q
    assert seq % q == 0
    zero_state = s0 is None
    if zero_state:
        s0 = jnp.zeros((1, d_ssm, SSM_STATE), F32)
    row_map = lambda b, c: (b * nchunk + c, 0)
    seq_map = lambda b, c: (b, 0, 0)
    s0_map = (lambda b, c: (0, 0, 0)) if zero_state else seq_map
    if fused_proj:
        proj = (w_z, w_xbc, w_dt)
        proj_specs = [_const_spec(w.shape) for w in proj]
    else:
        proj = _norm_proj(x, g, [w_z, w_xbc, w_dt], tm=min(256, batch * seq))
        proj_specs = [pl.BlockSpec((q, a.shape[1]), row_map) for a in proj]
    return pl.pallas_call(
        functools.partial(_ssd_kernel, q=q, zero_state=zero_state, fused_proj=fused_proj),
        out_shape=[jax.ShapeDtypeStruct((batch * seq, d), F32),
                   jax.ShapeDtypeStruct((batch, CONV_PAD, cdim), F32),
                   jax.ShapeDtypeStruct((batch, d_ssm, SSM_STATE), F32)],
        grid=(batch, nchunk),
        in_specs=[pl.BlockSpec((q, d), row_map), _const_spec(g.shape)] + proj_specs
        + [pl.BlockSpec((1, CONV_PAD, cdim), seq_map), pl.BlockSpec((1, d_ssm, SSM_STATE), s0_map),
                  _const_spec(cw.shape), _const_spec(cb.shape), _const_spec(dtb.shape), _const_spec(alog.shape),
                  _const_spec(dskx.shape), _const_spec(nw.shape), _const_spec(expand.shape), _const_spec(wo.shape)],
        out_specs=[pl.BlockSpec((q, d), row_map), pl.BlockSpec((1, CONV_PAD, cdim), seq_map),
                   pl.BlockSpec((1, d_ssm, SSM_STATE), seq_map)],
        scratch_shapes=[pltpu.VMEM((CONV_PAD + q, cdim), F32), pltpu.VMEM((SSM_STATE, d_ssm), F32)],
        compiler_params=_params(("parallel", "arbitrary")),
        name="ssd_mixer",
    )(x, g, *proj, conv_state, s0, cw, cb, dtb, alog, dskx, nw, expand, wo)


def _rwkv_proj_kernel(x_ref, sh_ref, g_ref, mu_ref, wrkv_ref, w0_ref, w1_ref, w2_ref, a0_ref, a1_ref, a2_ref,
                      g1_ref, g2_ref, kkw_ref, kaw_ref,
                      r_o, k_o, v_o, kk_o, bb_o, lw_o, g_o, sho_ref, carry_ref, *, nb, tl):
    c = pl.program_id(1)
    rows = nb * tl
    d = x_ref.shape[1]

    @pl.when(c == 0)
    def _():
        carry_ref[...] = sh_ref[...]

    u = _rms(x_ref[...], g_ref[...])
    t_in = _row_index(rows, tl)
    prev = jnp.where(t_in == 0, jnp.broadcast_to(carry_ref[...], (nb, tl, d)).reshape(rows, d),
                     pltpu.roll(u, 1, axis=0))
    last = u.reshape(nb, tl, d)[:, tl - 1:tl, :]
    carry_ref[...] = last
    sho_ref[...] = last

    diff = prev - u
    mu = mu_ref[...]

    def mixed(s):
        return (u + diff * mu[s:s + 1, :]).astype(BF16)

    r = jnp.dot(mixed(0), wrkv_ref[0], preferred_element_type=F32)
    k = jnp.dot(mixed(1), wrkv_ref[1], preferred_element_type=F32)
    v_o[...] = jnp.dot(mixed(2), wrkv_ref[2], preferred_element_type=F32)
    w_pre = w0_ref[...] + _bdot(jnp.tanh(jnp.dot(mixed(3), w1_ref[...], preferred_element_type=F32)), w2_ref[...])
    sp = jnp.maximum(-w_pre, 0.0) + jnp.log(1.0 + jnp.exp(-jnp.abs(w_pre)))
    lw_o[...] = -jnp.exp(-sp - 0.5)
    a = _sigmoid(a0_ref[...] + _bdot(jnp.dot(mixed(4), a1_ref[...], preferred_element_type=F32), a2_ref[...]))
    g_o[...] = _bdot(_sigmoid(jnp.dot(mixed(5), g1_ref[...], preferred_element_type=F32)), g2_ref[...])
    kk = k * kkw_ref[...]
    kk = kk * jnp.minimum(lax.rsqrt(_head_sum(kk * kk, RWKV_HEAD)), 1e12)
    r_o[...] = r
    k_o[...] = k * (1.0 + (a - 1.0) * kaw_ref[...])
    kk_o[...] = kk
    bb_o[...] = kk * a


def _rwkv_proj(x, shift0, g, mu, wrkv, w0, w1, w2, a0, a1, a2, g1, g2, kkw, kaw, *, batch, seq, nb, tl):
    d = x.shape[1]
    nchunk = seq // tl
    assert batch % nb == 0 and seq % tl == 0 and (nb == 1 or nchunk == 1)
    rows = nb * tl
    row_map = lambda i, c: (i * nchunk + c, 0)
    seq_map = lambda i, c: (i, 0, 0)
    consts = (g, mu, wrkv, w0, w1, w2, a0, a1, a2, g1, g2, kkw, kaw)
    return pl.pallas_call(
        functools.partial(_rwkv_proj_kernel, nb=nb, tl=tl),
        out_shape=[jax.ShapeDtypeStruct((batch * seq, d), F32)] * 7 + [jax.ShapeDtypeStruct((batch, 1, d), F32)],
        grid=(batch // nb, nchunk),
        in_specs=[pl.BlockSpec((rows, d), row_map), pl.BlockSpec((nb, 1, d), seq_map)]
        + [_const_spec(w.shape) for w in consts],
        out_specs=[pl.BlockSpec((rows, d), row_map)] * 7 + [pl.BlockSpec((nb, 1, d), seq_map)],
        scratch_shapes=[pltpu.VMEM((nb, 1, d), F32)],
        compiler_params=_params(("parallel", "arbitrary")),
        name="rwkv_proj",
    )(x, shift0, *consts)


def _unit_lower_solve(a_low, w, size):
    nmat = -a_low
    u = w + _bdot(nmat, w)
    m = 2
    while m < size:
        nmat = _bdot(nmat, nmat)
        u = u + _bdot(nmat, u)
        m *= 2
    return u


def _rwkv_chunk_kernel(r_ref, k_ref, v_ref, kk_ref, bb_ref, lw_ref, g_ref, x_ref, s0_ref, rk_ref, lnw_ref, lnb_ref,
                       wo_ref, o_ref, so_ref, s_ref, og_ref, *, nb, tl, chunk, zero_state):
    c = pl.program_id(1)
    d = x_ref.shape[1]
    hd = RWKV_HEAD
    heads = d // hd
    cpb = tl // chunk
    n_iter = nb * cpb

    @pl.when(c == 0)
    def _():
        if zero_state:
            s_ref[...] = jnp.zeros(s_ref.shape, F32)
        else:
            s_ref[...] = s0_ref[...]

    t_in = _row_index(chunk, chunk)
    ri = lax.broadcasted_iota(jnp.int32, (2 * chunk, 2 * chunk), 0)
    ci = lax.broadcasted_iota(jnp.int32, (2 * chunk, 2 * chunk), 1) & (chunk - 1)
    blk_mask = jnp.where(ri < chunk, ri - 1, ri - chunk) >= ci

    def body(it, carry):
        seq_i = it if cpb == 1 else 0
        rows = pl.ds(pl.multiple_of(it * chunk, chunk), chunk)
        r = r_ref[rows, :]
        k = k_ref[rows, :]
        v = v_ref[rows, :]
        kk = kk_ref[rows, :]
        bb = bb_ref[rows, :]
        lw = lw_ref[rows, :]
        cum = _cumsum_rows(lw, t_in, chunk)
        c_last = cum[chunk - 1:chunk, :]
        g_last = jnp.exp(c_last)
        e_inv = jnp.exp(-cum)
        e_end = jnp.exp(c_last - cum)
        p_st = jnp.concatenate([kk * jnp.exp(cum - lw), r * jnp.exp(cum)], axis=0).astype(BF16)
        q_st = jnp.concatenate([bb * e_inv, k * e_inv], axis=0).astype(BF16)
        e_st = jnp.concatenate([k * e_end, bb * e_end], axis=0).astype(BF16)
        zv_st = jnp.concatenate([jnp.zeros_like(v), v], axis=0).astype(BF16)
        hs = range(heads)
        sls = [slice(h * hd, (h + 1) * hd) for h in hs]
        s_l = [s_ref[seq_i, sl, :] for sl in sls]
        g_m = [jnp.where(blk_mask, _bdot_nt(p_st[:, sl], q_st[:, sl]), 0.0) for sl in sls]
        ps = [_bdot_nt(p_st[:, sls[h]], s_l[h]) for h in hs]
        t1 = [_bdot(g_m[h], zv_st[:, sls[h]]) for h in hs]
        nmat = [-g_m[h][0:chunk, 0:chunk] for h in hs]
        w_l = [ps[h][0:chunk] + t1[h][0:chunk] for h in hs]
        u_l = [w_l[h] + _bdot(nmat[h], w_l[h]) for h in hs]
        m = 2
        while m < chunk:
            nmat = [_bdot(x, x) for x in nmat]
            u_l = [u_l[h] + _bdot(nmat[h], u_l[h]) for h in hs]
            m *= 2
        o_parts = [ps[h][chunk:] + t1[h][chunk:] - _bdot(g_m[h][chunk:, 0:chunk], u_l[h]) for h in hs]
        for h in hs:
            vu = jnp.concatenate([v[:, sls[h]], -u_l[h]], axis=0)
            s_ref[seq_i, sls[h], :] = s_l[h] * g_last[:, sls[h]] + _bdot_tn(vu, e_st[:, sls[h]])
        o = jnp.concatenate(o_parts, axis=1)
        mean = _head_sum(o, hd) * (1.0 / hd)
        oc = o - mean
        var = _head_sum(oc * oc, hd) * (1.0 / hd)
        on = oc * lax.rsqrt(var + GN_EPS) * lnw_ref[...] + lnb_ref[...]
        bonus = _head_sum(r * k * rk_ref[...], hd) * v
        og_ref[rows, :] = (on + bonus) * g_ref[rows, :]
        return carry

    lax.fori_loop(0, n_iter, body, 0)

    @pl.when(c == pl.num_programs(1) - 1)
    def _():
        so_ref[...] = s_ref[...]

    o_ref[...] = x_ref[...] + _bdot(og_ref[...], wo_ref[...])


def _rwkv_mixer(x, r, k, v, kk, bb, lw, g, s0, rk, lnw, lnb, wo, *, batch, seq, nb, tl, chunk):
    d = x.shape[1]
    nchunk = seq // tl
    assert batch % nb == 0 and seq % tl == 0 and tl % chunk == 0 and (nb == 1 or tl == chunk)
    rows = nb * tl
    zero_state = s0 is None
    if zero_state:
        s0 = jnp.zeros((nb, d, RWKV_HEAD), F32)
    row_map = lambda i, c: (i * nchunk + c, 0)
    seq_map = lambda i, c: (i, 0, 0)
    s0_map = (lambda i, c: (0, 0, 0)) if zero_state else seq_map
    row_spec = pl.BlockSpec((rows, d), row_map)
    return pl.pallas_call(
        functools.partial(_rwkv_chunk_kernel, nb=nb, tl=tl, chunk=chunk, zero_state=zero_state),
        out_shape=[jax.ShapeDtypeStruct((batch * seq, d), F32), jax.ShapeDtypeStruct((batch, d, RWKV_HEAD), F32)],
        grid=(batch // nb, nchunk),
        in_specs=[row_spec] * 8 + [pl.BlockSpec((nb, d, RWKV_HEAD), s0_map),
                                   _const_spec(rk.shape), _const_spec(lnw.shape), _const_spec(lnb.shape),
                                   _const_spec(wo.shape)],
        out_specs=[row_spec, pl.BlockSpec((nb, d, RWKV_HEAD), seq_map)],
        scratch_shapes=[pltpu.VMEM((nb, d, RWKV_HEAD), F32), pltpu.VMEM((rows, d), F32)],
        compiler_params=_params(("parallel", "arbitrary")),
        name="rwkv_chunk",
    )(r, k, v, kk, bb, lw, g, x, s0, rk, lnw, lnb, wo)


def _pad_conv_state(state):
    return jnp.pad(state, ((0, 0), (CONV_PAD - (CONV_W - 1), 0), (0, 0)))


def _trunk(x3, states, start_pos, p, cfg):
    batch, seq, d = x3.shape
    x = x3.reshape(batch * seq, d)
    depth = p["norm_mix"].shape[0]
    new_lc, new_lh, new_sc, new_ss, new_rs, new_rw = [], [], [], [], [], []
    ia = ib = ic = 0
    for layer in range(depth):
        g_mix = p["norm_mix"][layer][None, :]
        kind = layer % 3
        if kind == 0:
            cs = states["lru_conv"][ia] if states else jnp.zeros((batch, CONV_W - 1, d), F32)
            h0 = states["lru_h"][ia] if states else jnp.zeros((batch, d), F32)
            x3, cso, ho = _lru_mixer(
                x.reshape(batch, seq, d), g_mix, p["lru_w_in"][ia], _pad_conv_state(cs), h0,
                p["lru_conv_w"][ia], p["lru_conv_b"][ia][None, :], p["lru_w_r"][ia], p["lru_b_r"][ia][None, :],
                p["lru_w_i"][ia], p["lru_b_i"][ia][None, :], p["lru_lambda"][ia][None, :], p["lru_w_out"][ia],
                nb=cfg["lru_nb"], tl=cfg["lru_tl"], start_pos=start_pos)
            x = x3.reshape(batch * seq, d)
            new_lc.append(cso[:, CONV_PAD - (CONV_W - 1):, :])
            new_lh.append(ho)
            ia += 1
        elif kind == 1:
            cdim = p["ssm_w_xbc"].shape[2]
            cs = states["ssm_conv"][ib] if states else jnp.zeros((batch, CONV_W - 1, cdim), F32)
            s0 = states["ssm"][ib].reshape(batch, -1, SSM_STATE) if states else None
            x, cso, so = _ssd_mixer(
                x, g_mix, p["ssm_w_z"][ib], p["ssm_w_xbc"][ib], p["ssm_w_dt"][ib], _pad_conv_state(cs), s0,
                p["ssm_conv_w"][ib], p["ssm_conv_b"][ib][None, :], p["ssm_dt_bias"][ib], p["ssm_a_log"][ib],
                p["ssm_d_x"][ib], p["ssm_norm_w"][ib][None, :], p["ssm_expand"], p["ssm_w_out"][ib],
                batch=batch, seq=seq, q=cfg["ssd_q"], fused_proj=cfg["ssd_fused"])
            new_sc.append(cso[:, CONV_PAD - (CONV_W - 1):, :])
            new_ss.append(so.reshape(batch, -1, SSM_HEADDIM, SSM_STATE))
            ib += 1
        else:
            sh0 = states["rwkv_shift"][ic] if states else jnp.zeros((batch, d), F32)
            s0 = states["rwkv_wkv"][ic].reshape(batch, d, RWKV_HEAD) if states else None
            r, k, v, kk, bb, lw, g, sho = _rwkv_proj(
                x, sh0[:, None, :], g_mix, p["rwkv_mu"][ic], p["rwkv_w_rkv"][ic], p["rwkv_w0"][ic][None, :],
                p["rwkv_w_w1"][ic], p["rwkv_w_w2"][ic], p["rwkv_a0"][ic][None, :], p["rwkv_w_a1"][ic],
                p["rwkv_w_a2"][ic], p["rwkv_w_g1"][ic], p["rwkv_w_g2"][ic], p["rwkv_k_k"][ic][None, :],
                p["rwkv_k_a"][ic][None, :], batch=batch, seq=seq, nb=cfg["rwkv_proj_nb"], tl=cfg["rwkv_tl"])
            x, so = _rwkv_mixer(
                x, r, k, v, kk, bb, lw, g, s0, p["rwkv_r_k"][ic].reshape(1, d), p["rwkv_lnx_w"][ic][None, :],
                p["rwkv_lnx_b"][ic][None, :], p["rwkv_w_out"][ic],
                batch=batch, seq=seq, nb=cfg["rwkv_nb"], tl=cfg["rwkv_tl"], chunk=cfg["rwkv_chunk"])
            new_rs.append(sho[:, 0, :])
            new_rw.append(so.reshape(batch, d // RWKV_HEAD, RWKV_HEAD, RWKV_HEAD))
            ic += 1
        x = _ffn(x, p["norm_ffn"][layer][None, :], p["ffn_w1"][layer], p["ffn_w2"][layer],
                 p["norm_final"][None, :], tm=cfg["tm_ffn"], final_norm=(layer == depth - 1))
    stack = lambda parts: parts[0][None] if len(parts) == 1 else jnp.stack(parts)
    return (x.reshape(batch, seq, d), stack(new_lc), stack(new_lh), stack(new_sc), stack(new_ss),
            stack(new_rs), stack(new_rw))


def _prep_params(p):
    q = dict(p)
    for name in ("lru_w_in", "lru_w_r", "lru_w_i", "lru_w_out", "ssm_w_out", "rwkv_w_rkv", "rwkv_w_w1", "rwkv_w_w2",
                 "rwkv_w_a1", "rwkv_w_a2", "rwkv_w_g1", "rwkv_w_g2", "rwkv_w_out", "ffn_w1", "ffn_w2"):
        q[name] = p[name].astype(BF16)
    d_ssm = p["ssm_w_out"].shape[1]
    heads = p["ssm_dt_bias"].shape[1]
    w_in = p["ssm_w_in"]
    cdim = w_in.shape[2] - d_ssm - heads
    pad = V7X_LANES - heads
    q["ssm_w_z"] = w_in[:, :, :d_ssm].astype(BF16)
    q["ssm_w_xbc"] = w_in[:, :, d_ssm:d_ssm + cdim].astype(BF16)
    q["ssm_w_dt"] = jnp.pad(w_in[:, :, d_ssm + cdim:], ((0, 0), (0, 0), (0, pad))).astype(BF16)
    for name in ("ssm_dt_bias", "ssm_a_log"):
        q[name] = jnp.pad(p[name], ((0, 0), (0, pad)))[:, None, :]
    q["ssm_d_x"] = jnp.repeat(p["ssm_d"], SSM_HEADDIM, axis=1)[:, None, :]
    q["ssm_expand"] = (jnp.arange(V7X_LANES)[:, None] == jnp.arange(d_ssm)[None, :] // SSM_HEADDIM).astype(BF16)
    return q


def kernel(x_prompt, x_sample, state_lru_conv, state_lru_h, state_ssm_conv, state_ssm, state_rwkv_shift, state_rwkv_wkv, norm_mix, norm_ffn, norm_final, lru_w_in, lru_conv_w, lru_conv_b, lru_w_r, lru_b_r, lru_w_i, lru_b_i, lru_lambda, lru_w_out, ssm_w_in, ssm_conv_w, ssm_conv_b, ssm_dt_bias, ssm_a_log, ssm_d, ssm_norm_w, ssm_w_out, rwkv_mu, rwkv_w_rkv, rwkv_w0, rwkv_w_w1, rwkv_w_w2, rwkv_a0, rwkv_w_a1, rwkv_w_a2, rwkv_w_g1, rwkv_w_g2, rwkv_k_k, rwkv_k_a, rwkv_r_k, rwkv_lnx_w, rwkv_lnx_b, rwkv_w_out, ffn_w1, ffn_w2):
    p = _prep_params(dict(
        norm_mix=norm_mix, norm_ffn=norm_ffn, norm_final=norm_final,
        lru_w_in=lru_w_in, lru_conv_w=lru_conv_w, lru_conv_b=lru_conv_b, lru_w_r=lru_w_r, lru_b_r=lru_b_r,
        lru_w_i=lru_w_i, lru_b_i=lru_b_i, lru_lambda=lru_lambda, lru_w_out=lru_w_out,
        ssm_w_in=ssm_w_in, ssm_conv_w=ssm_conv_w, ssm_conv_b=ssm_conv_b, ssm_dt_bias=ssm_dt_bias,
        ssm_a_log=ssm_a_log, ssm_d=ssm_d, ssm_norm_w=ssm_norm_w, ssm_w_out=ssm_w_out,
        rwkv_mu=rwkv_mu, rwkv_w_rkv=rwkv_w_rkv, rwkv_w0=rwkv_w0, rwkv_w_w1=rwkv_w_w1, rwkv_w_w2=rwkv_w_w2,
        rwkv_a0=rwkv_a0, rwkv_w_a1=rwkv_w_a1, rwkv_w_a2=rwkv_w_a2, rwkv_w_g1=rwkv_w_g1, rwkv_w_g2=rwkv_w_g2,
        rwkv_k_k=rwkv_k_k, rwkv_k_a=rwkv_k_a, rwkv_r_k=rwkv_r_k, rwkv_lnx_w=rwkv_lnx_w, rwkv_lnx_b=rwkv_lnx_b,
        rwkv_w_out=rwkv_w_out, ffn_w1=ffn_w1, ffn_w2=ffn_w2))
    seq_p = x_prompt.shape[1]
    seq_s = x_sample.shape[1]
    cfg_p = dict(tm_ffn=512, lru_nb=8, lru_tl=32, ssd_q=min(SSM_CHUNK, seq_p), ssd_fused=True,
                 rwkv_proj_nb=1, rwkv_nb=1, rwkv_tl=256, rwkv_chunk=RWKV_CHUNK)
    cfg_s = dict(tm_ffn=512, lru_nb=32, lru_tl=seq_s, ssd_q=seq_s, ssd_fused=False,
                 rwkv_proj_nb=32, rwkv_nb=8, rwkv_tl=seq_s, rwkv_chunk=seq_s)
    y_p, lc_p, lh_p, sc_p, ss_p, rs_p, rw_p = _trunk(x_prompt, None, 0, p, cfg_p)
    states = dict(lru_conv=state_lru_conv, lru_h=state_lru_h, ssm_conv=state_ssm_conv, ssm=state_ssm,
                  rwkv_shift=state_rwkv_shift, rwkv_wkv=state_rwkv_wkv)
    y_s, lc_s, lh_s, sc_s, ss_s, rs_s, rw_s = _trunk(x_sample, states, PAST_LEN, p, cfg_s)
    return (y_p, y_s, lc_p, lc_s, lh_p, lh_s, sc_p, sc_s, ss_p, ss_s, rs_p, rs_s, rw_p, rw_s)
```

```python
import functools
import math

import jax
import jax.numpy as jnp
from jax import lax
from jax.experimental import pallas as pl
from jax.experimental.pallas import tpu as pltpu

F32 = jnp.float32
BF16 = jnp.bfloat16

NORM_EPS = 1e-6
CONV_W = 4
LRU_BLOCKS = 8
LRU_C = 8.0
SSM_HEADDIM = 64
SSM_GROUPS = 8
SSM_STATE = 128
SSM_CHUNK = 128
SSM_NORM_EPS = 1e-5
RWKV_HEAD = 64
GN_EPS = 64e-5
PAST_LEN = 16384

V7X_LANES = 128
V7X_SUBLANES = 8
V7X_VMEM_BYTES = 64 * 1024 * 1024
VMEM_LIMIT = V7X_VMEM_BYTES - 8 * 1024 * 1024

RWKV_CHUNK = 64
CONV_PAD = V7X_SUBLANES


def _bdot(a, b):
    return jnp.dot(a.astype(BF16), b.astype(BF16), preferred_element_type=F32)


def _bdot_nt(a, b):
    return lax.dot_general(a.astype(BF16), b.astype(BF16), (((1,), (1,)), ((), ())),
                           preferred_element_type=F32)


def _bdot_tn(a, b):
    return lax.dot_general(a.astype(BF16), b.astype(BF16), (((0,), (0,)), ((), ())),
                           preferred_element_type=F32)


def _rms(x, g):
    return x * lax.rsqrt(jnp.mean(x * x, axis=-1, keepdims=True) + NORM_EPS) * g


def _sigmoid(x):
    return 0.5 * jnp.tanh(0.5 * x) + 0.5


def _softplus(x):
    return jnp.maximum(x, 0.0) + jnp.log1p(jnp.exp(-jnp.abs(x)))


def _silu(x):
    return x * _sigmoid(x)


def _gelu_tanh(x):
    c = math.sqrt(2.0 / math.pi)
    return 0.5 * x * (1.0 + jnp.tanh(c * (x + 0.044715 * (x * x * x))))


def _row_index(rows, period):
    return lax.broadcasted_iota(jnp.int32, (rows, 1), 0) & (period - 1)


def _cumsum_rows(x, t_in, period):
    d = 1
    while d < period:
        x = x + jnp.where(t_in >= d, pltpu.roll(x, d, axis=0), 0.0)
        d *= 2
    return x


def _head_sum(x, head):
    rows, width = x.shape
    lo = lax.broadcasted_iota(jnp.int32, (1, V7X_LANES), 1) < head
    parts = []
    for j in range(width // V7X_LANES):
        seg = x[:, j * V7X_LANES:(j + 1) * V7X_LANES]
        s_lo = jnp.sum(jnp.where(lo, seg, 0.0), axis=-1, keepdims=True)
        s_hi = jnp.sum(jnp.where(lo, 0.0, seg), axis=-1, keepdims=True)
        parts.append(jnp.where(lo, s_lo, s_hi))
    return jnp.concatenate(parts, axis=1)


def _const_spec(shape):
    nd = len(shape)
    return pl.BlockSpec(shape, lambda *_: (0,) * nd, pipeline_mode=pl.Buffered(1))


def _params(sem):
    return pltpu.CompilerParams(dimension_semantics=sem, vmem_limit_bytes=VMEM_LIMIT)


def _proj_kernel(x_ref, g_ref, *refs, n_out, col_chunk):
    w_refs, o_refs = refs[:n_out], refs[n_out:]
    u = _rms(x_ref[...], g_ref[...]).astype(BF16)
    for w_ref, o_ref in zip(w_refs, o_refs):
        n = w_ref.shape[1]
        step = min(col_chunk, n)
        for c0 in range(0, n, step):
            o_ref[:, c0:c0 + step] = jnp.dot(u, w_ref[:, c0:c0 + step], preferred_element_type=F32)


def _norm_proj(x, g, weights, *, tm):
    t, d = x.shape
    assert t % tm == 0
    n_out = len(weights)
    return pl.pallas_call(
        functools.partial(_proj_kernel, n_out=n_out, col_chunk=1024),
        out_shape=[jax.ShapeDtypeStruct((t, w.shape[1]), F32) for w in weights],
        grid=(t // tm,),
        in_specs=[pl.BlockSpec((tm, d), lambda i: (i, 0)), _const_spec((1, d))]
        + [_const_spec(w.shape) for w in weights],
        out_specs=[pl.BlockSpec((tm, w.shape[1]), lambda i: (i, 0)) for w in weights],
        compiler_params=_params(("parallel",)),
        name="norm_proj",
    )(x, g, *weights)


def _ffn_kernel(xa_ref, xb_ref, g_ref, w1_ref, w2_ref, gf_ref, oa_ref, ob_ref, *, n_a, f_chunk, final_norm):
    def run(x_ref, o_ref):
        x = x_ref[...]
        u = _rms(x, g_ref[...]).astype(BF16)
        acc = x
        for c0 in range(0, w1_ref.shape[1], f_chunk):
            h = jnp.dot(u, w1_ref[:, c0:c0 + f_chunk], preferred_element_type=F32)
            h = jnp.square(jnp.maximum(h, 0.0)).astype(BF16)
            acc = acc + jnp.dot(h, w2_ref[c0:c0 + f_chunk, :], preferred_element_type=F32)
        if final_norm:
            acc = _rms(acc, gf_ref[...])
        o_ref[...] = acc

    i = pl.program_id(0)

    @pl.when(i < n_a)
    def _():
        run(xa_ref, oa_ref)

    @pl.when(i >= n_a)
    def _():
        run(xb_ref, ob_ref)


def _ffn(xa, xb, g, w1, w2, g_final, *, tm, final_norm):
    d = xa.shape[1]
    f = w1.shape[1]
    assert xa.shape[0] % tm == 0 and xb.shape[0] % tm == 0
    n_a, n_b = xa.shape[0] // tm, xb.shape[0] // tm
    a_map = lambda i: (jnp.minimum(i, n_a - 1), 0)
    b_map = lambda i: (jnp.maximum(i - n_a, 0), 0)
    return pl.pallas_call(
        functools.partial(_ffn_kernel, n_a=n_a, f_chunk=1024, final_norm=final_norm),
        out_shape=[jax.ShapeDtypeStruct(xa.shape, F32), jax.ShapeDtypeStruct(xb.shape, F32)],
        grid=(n_a + n_b,),
        in_specs=[pl.BlockSpec((tm, d), a_map), pl.BlockSpec((tm, d), b_map), _const_spec((1, d)),
                  _const_spec((d, f)), _const_spec((f, d)), _const_spec((1, d))],
        out_specs=[pl.BlockSpec((tm, d), a_map), pl.BlockSpec((tm, d), b_map)],
        compiler_params=_params(("arbitrary",)),
        name="ffn",
    )(xa, xb, g, w1, w2, g_final)


def _conv_taps(xe_ref, cw, cb, tl):
    nd = len(xe_ref.shape)

    def win(off):
        if nd == 3:
            return xe_ref[:, off:off + tl, :]
        return xe_ref[off:off + tl, :]

    base = CONV_PAD - (CONV_W - 1)
    y = cb + win(base) * cw[0:1, :]
    for k in range(1, CONV_W):
        y = y + win(base + k) * cw[k:k + 1, :]
    return y


def _lru_kernel(x_ref, g_ref, win_ref, cs_ref, h0_ref, cw_ref, cb_ref, wr_ref, br_ref, wi_ref, bi_ref, lam_ref, wo_ref,
                o_ref, cso_ref, ho_ref, xe_ref, hs_ref, hc_ref, *, nb, tl, start_pos):
    c = pl.program_id(1)
    rows = nb * tl
    d = x_ref.shape[2]
    blk = d // LRU_BLOCKS
    ncol = d // V7X_LANES
    pitch = CONV_PAD + tl
    cols = [slice(j * V7X_LANES, (j + 1) * V7X_LANES) for j in range(ncol)]

    @pl.when(c == 0)
    def _():
        for j in range(ncol):
            for s in range(nb):
                xe_ref[j, s * pitch:s * pitch + CONV_PAD, :] = cs_ref[s, :, cols[j]]
        hc_ref[...] = h0_ref[...]

    x = x_ref[...].reshape(rows, d)
    u = _rms(x, g_ref[...]).astype(BF16)
    xb = jnp.dot(u, win_ref[:, 0:d], preferred_element_type=F32)
    for j in range(ncol):
        for s in range(nb):
            xe_ref[j, s * pitch + CONV_PAD:(s + 1) * pitch, :] = xb[s * tl:(s + 1) * tl, cols[j]]

    cw = cw_ref[...]
    cb = cb_ref[...]
    base = CONV_PAD - (CONV_W - 1)
    slabs = []
    for t in range(tl):
        parts = []
        for j in range(ncol):
            acc = cb[:, cols[j]] + xe_ref[j, pl.ds(base + t, nb, stride=pitch), :] * cw[0:1, cols[j]]
            for k in range(1, CONV_W):
                acc = acc + xe_ref[j, pl.ds(base + t + k, nb, stride=pitch), :] * cw[k:k + 1, cols[j]]
            parts.append(acc)
        slabs.append(jnp.concatenate(parts, axis=1))
    xc = jnp.concatenate(slabs, axis=0)
    for j in range(ncol):
        for s in range(nb):
            tail = xe_ref[j, s * pitch + tl:(s + 1) * pitch, :]
            xe_ref[j, s * pitch:s * pitch + CONV_PAD, :] = tail
            cso_ref[s, :, cols[j]] = tail

    xcb = xc.astype(BF16)
    r_pre = jnp.concatenate(
        [jnp.dot(xcb[:, j * blk:(j + 1) * blk], wr_ref[j], preferred_element_type=F32) for j in range(LRU_BLOCKS)],
        axis=1) + br_ref[...]
    i_pre = jnp.concatenate(
        [jnp.dot(xcb[:, j * blk:(j + 1) * blk], wi_ref[j], preferred_element_type=F32) for j in range(LRU_BLOCKS)],
        axis=1) + bi_ref[...]
    r = _sigmoid(r_pre)
    gi = _sigmoid(i_pre)
    log_a = (-LRU_C) * r * _softplus(-lam_ref[...])
    a = jnp.exp(log_a)
    m2 = -jnp.tanh(log_a) * (a * a + 1.0)
    mult = jnp.where(m2 > 0.0, m2 * lax.rsqrt(m2), 0.0)
    if start_pos == 0:
        reset = (lax.broadcasted_iota(jnp.int32, (rows, 1), 0) < nb) & (c == 0)
        a = jnp.where(reset, 0.0, a)
        mult = jnp.where(reset, 1.0, mult)
    b = mult * (gi * xc)

    h = hc_ref[...]
    for t in range(tl):
        h = a[t * nb:(t + 1) * nb] * h + b[t * nb:(t + 1) * nb]
        for j in range(ncol):
            hs_ref[j, pl.ds(t, nb, stride=pitch), :] = h[:, cols[j]]
    hc_ref[...] = h
    ho_ref[...] = h
    h_bm = jnp.concatenate(
        [jnp.concatenate([hs_ref[j, s * pitch:s * pitch + tl, :] for s in range(nb)], axis=0) for j in range(ncol)],
        axis=1)

    gate = jnp.dot(u, win_ref[:, d:2 * d], preferred_element_type=F32)
    y = (h_bm * _gelu_tanh(gate)).astype(BF16)
    o_ref[...] = (x + jnp.dot(y, wo_ref[...], preferred_element_type=F32)).reshape(nb, tl, d)


def _lru_mixer(x3, g, w_in, conv_state, h0, cw, cb, wr, br, wi, bi, lam, wo, *, nb, tl, start_pos):
    batch, seq, d = x3.shape
    nchunk = seq // tl
    assert batch % nb == 0 and seq % tl == 0 and nb % V7X_SUBLANES == 0 and tl % V7X_SUBLANES == 0
    ncol = d // V7X_LANES
    blk_map = lambda i, c: (i, c, 0)
    seq_map = lambda i, c: (i, 0, 0)
    row_map = lambda i, c: (i, 0)
    return pl.pallas_call(
        functools.partial(_lru_kernel, nb=nb, tl=tl, start_pos=start_pos),
        out_shape=[jax.ShapeDtypeStruct((batch, seq, d), F32),
                   jax.ShapeDtypeStruct((batch, CONV_PAD, d), F32),
                   jax.ShapeDtypeStruct((batch, d), F32)],
        grid=(batch // nb, nchunk),
        in_specs=[pl.BlockSpec((nb, tl, d), blk_map), _const_spec(g.shape), _const_spec(w_in.shape),
                  pl.BlockSpec((nb, CONV_PAD, d), seq_map), pl.BlockSpec((nb, d), row_map),
                  _const_spec(cw.shape), _const_spec(cb.shape), _const_spec(wr.shape), _const_spec(br.shape),
                  _const_spec(wi.shape), _const_spec(bi.shape), _const_spec(lam.shape), _const_spec(wo.shape)],
        out_specs=[pl.BlockSpec((nb, tl, d), blk_map), pl.BlockSpec((nb, CONV_PAD, d), seq_map),
                   pl.BlockSpec((nb, d), row_map)],
        scratch_shapes=[pltpu.VMEM((ncol, nb * (CONV_PAD + tl), V7X_LANES), F32),
                        pltpu.VMEM((ncol, nb * (CONV_PAD + tl), V7X_LANES), F32),
                        pltpu.VMEM((nb, d), F32)],
        compiler_params=_params(("parallel", "arbitrary")),
        name="lru_mixer",
    )(x3, g, w_in, conv_state, h0, cw, cb, wr, br, wi, bi, lam, wo)


def _split_bf16(x, terms):
    parts = []
    for _ in range(terms - 1):
        hi = x.astype(BF16).astype(F32)
        parts.append(hi)
        x = x - hi
    parts.append(x)
    return parts


def _ssd_kernel(x_ref, g_ref, pz_ref, pxbc_ref, pdt_ref, cs_ref, s0_ref, cw_ref, cb_ref, dtb_ref, alog_ref, dskx_ref, nw_ref,
                ex_ref, wo_ref, o_ref, cso_ref, so_ref, xe_ref, st_ref, *, q, zero_state, fused_proj):
    c = pl.program_id(1)
    d_ssm = pz_ref.shape[1]
    n = SSM_STATE
    hpg = d_ssm // SSM_HEADDIM // SSM_GROUPS
    gw = hpg * SSM_HEADDIM
    nblk = d_ssm // n

    @pl.when(c == 0)
    def _():
        xe_ref[0:CONV_PAD, :] = cs_ref[0]
        if zero_state:
            st_ref[...] = jnp.zeros(st_ref.shape, F32)
        else:
            for j in range(nblk):
                st_ref[:, j * n:(j + 1) * n] = s0_ref[0, j * n:(j + 1) * n, :].T

    x = x_ref[...]
    if fused_proj:
        u = _rms(x, g_ref[...]).astype(BF16)
        for c0 in range(0, pxbc_ref.shape[1], 1024):
            xe_ref[CONV_PAD:CONV_PAD + q, c0:c0 + 1024] = jnp.dot(u, pxbc_ref[:, c0:c0 + 1024],
                                                                  preferred_element_type=F32)
        dt_pre = jnp.dot(u, pdt_ref[...], preferred_element_type=F32)
    else:
        xe_ref[CONV_PAD:CONV_PAD + q, :] = pxbc_ref[...]
        dt_pre = pdt_ref[...]
    xbc = _silu(_conv_taps(xe_ref, cw_ref[...], cb_ref[...], q))
    tail = xe_ref[q:q + CONV_PAD, :]
    xe_ref[0:CONV_PAD, :] = tail
    cso_ref[0] = tail

    xs = xbc[:, 0:d_ssm]
    bm = xbc[:, d_ssm:d_ssm + SSM_GROUPS * n]
    cm = xbc[:, d_ssm + SSM_GROUPS * n:]
    dt = _softplus(dt_pre + dtb_ref[...])
    da = dt * (-jnp.exp(alog_ref[...]))
    t_in = _row_index(q, q)
    acs = _cumsum_rows(da, t_in, q)
    if q < V7X_LANES:
        acs_sq = jnp.concatenate([acs, jnp.zeros((V7X_LANES - q, V7X_LANES), F32)], axis=0)
    else:
        acs_sq = acs
    acs_t = acs_sq.T

    lhs = jnp.concatenate(_split_bf16(dt, 2) + _split_bf16(acs, 3), axis=0).astype(BF16)
    ex = jnp.dot(lhs, ex_ref[...], preferred_element_type=F32)
    dt_x = ex[0:q] + ex[q:2 * q]
    acs_x = ex[2 * q:3 * q] + ex[3 * q:4 * q] + ex[4 * q:5 * q]
    last_x = acs_x[q - 1:q, :]
    xdt = (xs * dt_x).astype(BF16)
    xdtd = (xs * (dt_x * jnp.exp(last_x - acs_x))).astype(BF16)

    tri = lax.broadcasted_iota(jnp.int32, (q, q), 0) >= lax.broadcasted_iota(jnp.int32, (q, q), 1)
    lane_head = lax.broadcasted_iota(jnp.int32, (1, gw), 1) // SSM_HEADDIM
    groups = range(SSM_GROUPS)
    bgs = [bm[:, g * n:(g + 1) * n].astype(BF16) for g in groups]
    cgs = [cm[:, g * n:(g + 1) * n].astype(BF16) for g in groups]
    cb_mats = [_bdot_nt(cgs[g], bgs[g]) for g in groups]
    y_off = [_bdot(cgs[g], st_ref[:, g * gw:(g + 1) * gw]) for g in groups]
    y_diag = []
    for g in groups:
        xdt_g = xdt[:, g * gw:(g + 1) * gw]
        acc = None
        for e in range(hpg):
            h = g * hpg + e
            seg = acs[:, h:h + 1] - acs_t[h:h + 1, 0:q]
            l_mat = jnp.exp(jnp.where(tri, seg, -jnp.inf))
            part = jnp.dot((cb_mats[g] * l_mat).astype(BF16), jnp.where(lane_head == e, xdt_g, jnp.zeros_like(xdt_g)),
                           preferred_element_type=F32)
            acc = part if acc is None else acc + part
        y_diag.append(acc)
    st_new = [_bdot_tn(bgs[g], xdtd[:, g * gw:(g + 1) * gw]) for g in groups]
    e_last_x = jnp.exp(last_x)
    for g in groups:
        cols = slice(g * gw, (g + 1) * gw)
        st_ref[:, cols] = st_ref[:, cols] * e_last_x[:, cols] + st_new[g]

    @pl.when(c == pl.num_programs(1) - 1)
    def _():
        for j in range(nblk):
            so_ref[0, j * n:(j + 1) * n, :] = st_ref[:, j * n:(j + 1) * n].T

    y = jnp.concatenate(y_diag, axis=1) + jnp.concatenate(y_off, axis=1) * jnp.exp(acs_x) + dskx_ref[...] * xs
    y = y * _silu(jnp.dot(u, pz_ref[...], preferred_element_type=F32) if fused_proj else pz_ref[...])
    norm_parts = []
    for g in groups:
        yg = y[:, g * gw:(g + 1) * gw]
        norm_parts.append(yg * lax.rsqrt(jnp.mean(yg * yg, axis=-1, keepdims=True) + SSM_NORM_EPS))
    yn = (jnp.concatenate(norm_parts, axis=1) * nw_ref[...]).astype(BF16)
    o_ref[...] = x + jnp.dot(yn, wo_ref[...], preferred_element_type=F32)


def _ssd_mixer(x, g, w_z, w_xbc, w_dt, conv_state, s0, cw, cb, dtb, alog, dskx, nw, expand, wo, *, batch, seq, q,
               fused_proj):
    d = x.shape[1]
    d_ssm = w_z.shape[1]
    cdim = w_xbc.shape[1]
    nchunk = seq // q
    assert seq % q == 0
    zero_state = s0 is None
    if zero_state:
        s0 = jnp.zeros((1, d_ssm, SSM_STATE), F32)
    row_map = lambda b, c: (b * nchunk + c, 0)
    seq_map = lambda b, c: (b, 0, 0)
    s0_map = (lambda b, c: (0, 0, 0)) if zero_state else seq_map
    if fused_proj:
        proj = (w_z, w_xbc, w_dt)
        proj_specs = [_const_spec(w.shape) for w in proj]
    else:
        proj = _norm_proj(x, g, [w_z, w_xbc, w_dt], tm=min(256, batch * seq))
        proj_specs = [pl.BlockSpec((q, a.shape[1]), row_map) for a in proj]
    return pl.pallas_call(
        functools.partial(_ssd_kernel, q=q, zero_state=zero_state, fused_proj=fused_proj),
        out_shape=[jax.ShapeDtypeStruct((batch * seq, d), F32),
                   jax.ShapeDtypeStruct((batch, CONV_PAD, cdim), F32),
                   jax.ShapeDtypeStruct((batch, d_ssm, SSM_STATE), F32)],
        grid=(batch, nchunk),
        in_specs=[pl.BlockSpec((q, d), row_map), _const_spec(g.shape)] + proj_specs
        + [pl.BlockSpec((1, CONV_PAD, cdim), seq_map), pl.BlockSpec((1, d_ssm, SSM_STATE), s0_map),
                  _const_spec(cw.shape), _const_spec(cb.shape), _const_spec(dtb.shape), _const_spec(alog.shape),
                  _const_spec(dskx.shape), _const_spec(nw.shape), _const_spec(expand.shape), _const_spec(wo.shape)],
        out_specs=[pl.BlockSpec((q, d), row_map), pl.BlockSpec((1, CONV_PAD, cdim), seq_map),
                   pl.BlockSpec((1, d_ssm, SSM_STATE), seq_map)],
        scratch_shapes=[pltpu.VMEM((CONV_PAD + q, cdim), F32), pltpu.VMEM((SSM_STATE, d_ssm), F32)],
        compiler_params=_params(("parallel", "arbitrary")),
        name="ssd_mixer",
    )(x, g, *proj, conv_state, s0, cw, cb, dtb, alog, dskx, nw, expand, wo)


def _rwkv_proj_kernel(x_ref, sh_ref, g_ref, mu_ref, wrkv_ref, w0_ref, w1_ref, w2_ref, a0_ref, a1_ref, a2_ref,
                      g1_ref, g2_ref, kkw_ref, kaw_ref,
                      r_o, k_o, v_o, kk_o, bb_o, lw_o, g_o, sho_ref, carry_ref, *, nb, tl):
    c = pl.program_id(1)
    rows = nb * tl
    d = x_ref.shape[1]

    @pl.when(c == 0)
    def _():
        carry_ref[...] = sh_ref[...]

    u = _rms(x_ref[...], g_ref[...])
    t_in = _row_index(rows, tl)
    prev = jnp.where(t_in == 0, jnp.broadcast_to(carry_ref[...], (nb, tl, d)).reshape(rows, d),
                     pltpu.roll(u, 1, axis=0))
    last = u.reshape(nb, tl, d)[:, tl - 1:tl, :]
    carry_ref[...] = last
    sho_ref[...] = last

    diff = prev - u
    mu = mu_ref[...]

    def mixed(s):
        return (u + diff * mu[s:s + 1, :]).astype(BF16)

    r = jnp.dot(mixed(0), wrkv_ref[0], preferred_element_type=F32)
    k = jnp.dot(mixed(1), wrkv_ref[1], preferred_element_type=F32)
    v_o[...] = jnp.dot(mixed(2), wrkv_ref[2], preferred_element_type=F32)
    w_pre = w0_ref[...] + _bdot(jnp.tanh(jnp.dot(mixed(3), w1_ref[...], preferred_element_type=F32)), w2_ref[...])
    sp = jnp.maximum(-w_pre, 0.0) + jnp.log(1.0 + jnp.exp(-jnp.abs(w_pre)))
    lw_o[...] = -jnp.exp(-sp - 0.5)
    a = _sigmoid(a0_ref[...] + _bdot(jnp.dot(mixed(4), a1_ref[...], preferred_element_type=F32), a2_ref[...]))
    g_o[...] = _bdot(_sigmoid(jnp.dot(mixed(5), g1_ref[...], preferred_element_type=F32)), g2_ref[...])
    kk = k * kkw_ref[...]
    kk = kk * jnp.minimum(lax.rsqrt(_head_sum(kk * kk, RWKV_HEAD)), 1e12)
    r_o[...] = r
    k_o[...] = k * (1.0 + (a - 1.0) * kaw_ref[...])
    kk_o[...] = kk
    bb_o[...] = kk * a


def _rwkv_proj(x, shift0, g, mu, wrkv, w0, w1, w2, a0, a1, a2, g1, g2, kkw, kaw, *, batch, seq, nb, tl):
    d = x.shape[1]
    nchunk = seq // tl
    assert batch % nb == 0 and seq % tl == 0 and (nb == 1 or nchunk == 1)
    rows = nb * tl
    row_map = lambda i, c: (i * nchunk + c, 0)
    seq_map = lambda i, c: (i, 0, 0)
    consts = (g, mu, wrkv, w0, w1, w2, a0, a1, a2, g1, g2, kkw, kaw)
    return pl.pallas_call(
        functools.partial(_rwkv_proj_kernel, nb=nb, tl=tl),
        out_shape=[jax.ShapeDtypeStruct((batch * seq, d), F32)] * 7 + [jax.ShapeDtypeStruct((batch, 1, d), F32)],
        grid=(batch // nb, nchunk),
        in_specs=[pl.BlockSpec((rows, d), row_map), pl.BlockSpec((nb, 1, d), seq_map)]
        + [_const_spec(w.shape) for w in consts],
        out_specs=[pl.BlockSpec((rows, d), row_map)] * 7 + [pl.BlockSpec((nb, 1, d), seq_map)],
        scratch_shapes=[pltpu.VMEM((nb, 1, d), F32)],
        compiler_params=_params(("parallel", "arbitrary")),
        name="rwkv_proj",
    )(x, shift0, *consts)


def _unit_lower_solve(a_low, w, size):
    nmat = -a_low
    u = w + _bdot(nmat, w)
    m = 2
    while m < size:
        nmat = _bdot(nmat, nmat)
        u = u + _bdot(nmat, u)
        m *= 2
    return u


def _rwkv_chunk_kernel(r_ref, k_ref, v_ref, kk_ref, bb_ref, lw_ref, g_ref, x_ref, s0_ref, rk_ref, lnw_ref, lnb_ref,
                       wo_ref, o_ref, so_ref, s_ref, og_ref, *, nb, tl, chunk, zero_state):
    c = pl.program_id(1)
    d = x_ref.shape[1]
    hd = RWKV_HEAD
    heads = d // hd
    cpb = tl // chunk
    n_iter = nb * cpb

    @pl.when(c == 0)
    def _():
        if zero_state:
            s_ref[...] = jnp.zeros(s_ref.shape, F32)
        else:
            s_ref[...] = s0_ref[...]

    t_in = _row_index(chunk, chunk)
    ri = lax.broadcasted_iota(jnp.int32, (2 * chunk, 2 * chunk), 0)
    ci = lax.broadcasted_iota(jnp.int32, (2 * chunk, 2 * chunk), 1) & (chunk - 1)
    blk_mask = jnp.where(ri < chunk, ri - 1, ri - chunk) >= ci

    def body(it, carry):
        seq_i = it if cpb == 1 else 0
        rows = pl.ds(pl.multiple_of(it * chunk, chunk), chunk)
        r = r_ref[rows, :]
        k = k_ref[rows, :]
        v = v_ref[rows, :]
        kk = kk_ref[rows, :]
        bb = bb_ref[rows, :]
        lw = lw_ref[rows, :]
        cum = _cumsum_rows(lw, t_in, chunk)
        c_last = cum[chunk - 1:chunk, :]
        g_last = jnp.exp(c_last)
        e_inv = jnp.exp(-cum)
        e_end = jnp.exp(c_last - cum)
        p_st = jnp.concatenate([kk * jnp.exp(cum - lw), r * jnp.exp(cum)], axis=0).astype(BF16)
        q_st = jnp.concatenate([bb * e_inv, k * e_inv], axis=0).astype(BF16)
        e_st = jnp.concatenate([k * e_end, bb * e_end], axis=0).astype(BF16)
        zv_st = jnp.concatenate([jnp.zeros_like(v), v], axis=0).astype(BF16)
        hs = range(heads)
        sls = [slice(h * hd, (h + 1) * hd) for h in hs]
        s_l = [s_ref[seq_i, sl, :] for sl in sls]
        g_m = [jnp.where(blk_mask, _bdot_nt(p_st[:, sl], q_st[:, sl]), 0.0) for sl in sls]
        ps = [_bdot_nt(p_st[:, sls[h]], s_l[h]) for h in hs]
        t1 = [_bdot(g_m[h], zv_st[:, sls[h]]) for h in hs]
        nmat = [-g_m[h][0:chunk, 0:chunk] for h in hs]
        w_l = [ps[h][0:chunk] + t1[h][0:chunk] for h in hs]
        u_l = [w_l[h] + _bdot(nmat[h], w_l[h]) for h in hs]
        m = 2
        while m < chunk:
            nmat = [_bdot(x, x) for x in nmat]
            u_l = [u_l[h] + _bdot(nmat[h], u_l[h]) for h in hs]
            m *= 2
        o_parts = [ps[h][chunk:] + t1[h][chunk:] - _bdot(g_m[h][chunk:, 0:chunk], u_l[h]) for h in hs]
        for h in hs:
            vu = jnp.concatenate([v[:, sls[h]], -u_l[h]], axis=0)
            s_ref[seq_i, sls[h], :] = s_l[h] * g_last[:, sls[h]] + _bdot_tn(vu, e_st[:, sls[h]])
        o = jnp.concatenate(o_parts, axis=1)
        mean = _head_sum(o, hd) * (1.0 / hd)
        oc = o - mean
        var = _head_sum(oc * oc, hd) * (1.0 / hd)
        on = oc * lax.rsqrt(var + GN_EPS) * lnw_ref[...] + lnb_ref[...]
        bonus = _head_sum(r * k * rk_ref[...], hd) * v
        og_ref[rows, :] = (on + bonus) * g_ref[rows, :]
        return carry

    lax.fori_loop(0, n_iter, body, 0)

    @pl.when(c == pl.num_programs(1) - 1)
    def _():
        so_ref[...] = s_ref[...]

    o_ref[...] = x_ref[...] + _bdot(og_ref[...], wo_ref[...])


def _rwkv_mixer(x, r, k, v, kk, bb, lw, g, s0, rk, lnw, lnb, wo, *, batch, seq, nb, tl, chunk):
    d = x.shape[1]
    nchunk = seq // tl
    assert batch % nb == 0 and seq % tl == 0 and tl % chunk == 0 and (nb == 1 or tl == chunk)
    rows = nb * tl
    zero_state = s0 is None
    if zero_state:
        s0 = jnp.zeros((nb, d, RWKV_HEAD), F32)
    row_map = lambda i, c: (i * nchunk + c, 0)
    seq_map = lambda i, c: (i, 0, 0)
    s0_map = (lambda i, c: (0, 0, 0)) if zero_state else seq_map
    row_spec = pl.BlockSpec((rows, d), row_map)
    return pl.pallas_call(
        functools.partial(_rwkv_chunk_kernel, nb=nb, tl=tl, chunk=chunk, zero_state=zero_state),
        out_shape=[jax.ShapeDtypeStruct((batch * seq, d), F32), jax.ShapeDtypeStruct((batch, d, RWKV_HEAD), F32)],
        grid=(batch // nb, nchunk),
        in_specs=[row_spec] * 8 + [pl.BlockSpec((nb, d, RWKV_HEAD), s0_map),
                                   _const_spec(rk.shape), _const_spec(lnw.shape), _const_spec(lnb.shape),
                                   _const_spec(wo.shape)],
        out_specs=[row_spec, pl.BlockSpec((nb, d, RWKV_HEAD), seq_map)],
        scratch_shapes=[pltpu.VMEM((nb, d, RWKV_HEAD), F32), pltpu.VMEM((rows, d), F32)],
        compiler_params=_params(("parallel", "arbitrary")),
        name="rwkv_chunk",
    )(r, k, v, kk, bb, lw, g, x, s0, rk, lnw, lnb, wo)


def _pad_conv_state(state):
    return jnp.pad(state, ((0, 0), (CONV_PAD - (CONV_W - 1), 0), (0, 0)))


class _Group:
    def __init__(self, x3, states, start_pos, cfg):
        self.batch, self.seq, d = x3.shape
        self.x = x3.reshape(self.batch * self.seq, d)
        self.states, self.start_pos, self.cfg = states, start_pos, cfg
        self.out = dict(lru_conv=[], lru_h=[], ssm_conv=[], ssm=[], rwkv_shift=[], rwkv_wkv=[])

    def state(self, name, idx, zero_shape):
        return self.states[name][idx] if self.states else jnp.zeros(zero_shape, F32)


def _mixer_layer(grp, layer, idx, p):
    batch, seq, cfg = grp.batch, grp.seq, grp.cfg
    x = grp.x
    d = x.shape[1]
    g_mix = p["norm_mix"][layer][None, :]
    kind = layer % 3
    if kind == 0:
        cs = grp.state("lru_conv", idx, (batch, CONV_W - 1, d))
        h0 = grp.state("lru_h", idx, (batch, d))
        x3, cso, ho = _lru_mixer(
            x.reshape(batch, seq, d), g_mix, p["lru_w_in"][idx], _pad_conv_state(cs), h0,
            p["lru_conv_w"][idx], p["lru_conv_b"][idx][None, :], p["lru_w_r"][idx], p["lru_b_r"][idx][None, :],
            p["lru_w_i"][idx], p["lru_b_i"][idx][None, :], p["lru_lambda"][idx][None, :], p["lru_w_out"][idx],
            nb=cfg["lru_nb"], tl=cfg["lru_tl"], start_pos=grp.start_pos)
        grp.x = x3.reshape(batch * seq, d)
        grp.out["lru_conv"].append(cso[:, CONV_PAD - (CONV_W - 1):, :])
        grp.out["lru_h"].append(ho)
    elif kind == 1:
        cdim = p["ssm_w_xbc"].shape[2]
        cs = grp.state("ssm_conv", idx, (batch, CONV_W - 1, cdim))
        s0 = grp.states["ssm"][idx].reshape(batch, -1, SSM_STATE) if grp.states else None
        grp.x, cso, so = _ssd_mixer(
            x, g_mix, p["ssm_w_z"][idx], p["ssm_w_xbc"][idx], p["ssm_w_dt"][idx], _pad_conv_state(cs), s0,
            p["ssm_conv_w"][idx], p["ssm_conv_b"][idx][None, :], p["ssm_dt_bias"][idx], p["ssm_a_log"][idx],
            p["ssm_d_x"][idx], p["ssm_norm_w"][idx][None, :], p["ssm_expand"], p["ssm_w_out"][idx],
            batch=batch, seq=seq, q=cfg["ssd_q"], fused_proj=cfg["ssd_fused"])
        grp.out["ssm_conv"].append(cso[:, CONV_PAD - (CONV_W - 1):, :])
        grp.out["ssm"].append(so.reshape(batch, -1, SSM_HEADDIM, SSM_STATE))
    else:
        sh0 = grp.state("rwkv_shift", idx, (batch, d))
        s0 = grp.states["rwkv_wkv"][idx].reshape(batch, d, RWKV_HEAD) if grp.states else None
        r, k, v, kk, bb, lw, g, sho = _rwkv_proj(
            x, sh0[:, None, :], g_mix, p["rwkv_mu"][idx], p["rwkv_w_rkv"][idx], p["rwkv_w0"][idx][None, :],
            p["rwkv_w_w1"][idx], p["rwkv_w_w2"][idx], p["rwkv_a0"][idx][None, :], p["rwkv_w_a1"][idx],
            p["rwkv_w_a2"][idx], p["rwkv_w_g1"][idx], p["rwkv_w_g2"][idx], p["rwkv_k_k"][idx][None, :],
            p["rwkv_k_a"][idx][None, :], batch=batch, seq=seq, nb=cfg["rwkv_proj_nb"], tl=cfg["rwkv_tl"])
        grp.x, so = _rwkv_mixer(
            x, r, k, v, kk, bb, lw, g, s0, p["rwkv_r_k"][idx].reshape(1, d), p["rwkv_lnx_w"][idx][None, :],
            p["rwkv_lnx_b"][idx][None, :], p["rwkv_w_out"][idx],
            batch=batch, seq=seq, nb=cfg["rwkv_nb"], tl=cfg["rwkv_tl"], chunk=cfg["rwkv_chunk"])
        grp.out["rwkv_shift"].append(sho[:, 0, :])
        grp.out["rwkv_wkv"].append(so.reshape(batch, d // RWKV_HEAD, RWKV_HEAD, RWKV_HEAD))


def _run_trunk(groups, p, tm_ffn):
    depth = p["norm_mix"].shape[0]
    counts = [0, 0, 0]
    for layer in range(depth):
        kind = layer % 3
        for grp in groups:
            _mixer_layer(grp, layer, counts[kind], p)
        counts[kind] += 1
        groups[0].x, groups[1].x = _ffn(
            groups[0].x, groups[1].x, p["norm_ffn"][layer][None, :], p["ffn_w1"][layer], p["ffn_w2"][layer],
            p["norm_final"][None, :], tm=tm_ffn, final_norm=(layer == depth - 1))
    stack = lambda parts: parts[0][None] if len(parts) == 1 else jnp.stack(parts)
    results = []
    for grp in groups:
        d = grp.x.shape[1]
        results.append([grp.x.reshape(grp.batch, grp.seq, d)]
                       + [stack(grp.out[name]) for name in ("lru_conv", "lru_h", "ssm_conv", "ssm", "rwkv_shift", "rwkv_wkv")])
    return results


def _prep_params(p):
    q = dict(p)
    for name in ("lru_w_in", "lru_w_r", "lru_w_i", "lru_w_out", "ssm_w_out", "rwkv_w_rkv", "rwkv_w_w1", "rwkv_w_w2",
                 "rwkv_w_a1", "rwkv_w_a2", "rwkv_w_g1", "rwkv_w_g2", "rwkv_w_out", "ffn_w1", "ffn_w2"):
        q[name] = p[name].astype(BF16)
    d_ssm = p["ssm_w_out"].shape[1]
    heads = p["ssm_dt_bias"].shape[1]
    w_in = p["ssm_w_in"]
    cdim = w_in.shape[2] - d_ssm - heads
    pad = V7X_LANES - heads
    q["ssm_w_z"] = w_in[:, :, :d_ssm].astype(BF16)
    q["ssm_w_xbc"] = w_in[:, :, d_ssm:d_ssm + cdim].astype(BF16)
    q["ssm_w_dt"] = jnp.pad(w_in[:, :, d_ssm + cdim:], ((0, 0), (0, 0), (0, pad))).astype(BF16)
    for name in ("ssm_dt_bias", "ssm_a_log"):
        q[name] = jnp.pad(p[name], ((0, 0), (0, pad)))[:, None, :]
    q["ssm_d_x"] = jnp.repeat(p["ssm_d"], SSM_HEADDIM, axis=1)[:, None, :]
    q["ssm_expand"] = (jnp.arange(V7X_LANES)[:, None] == jnp.arange(d_ssm)[None, :] // SSM_HEADDIM).astype(BF16)
    return q


def kernel(x_prompt, x_sample, state_lru_conv, state_lru_h, state_ssm_conv, state_ssm, state_rwkv_shift, state_rwkv_wkv, norm_mix, norm_ffn, norm_final, lru_w_in, lru_conv_w, lru_conv_b, lru_w_r, lru_b_r, lru_w_i, lru_b_i, lru_lambda, lru_w_out, ssm_w_in, ssm_conv_w, ssm_conv_b, ssm_dt_bias, ssm_a_log, ssm_d, ssm_norm_w, ssm_w_out, rwkv_mu, rwkv_w_rkv, rwkv_w0, rwkv_w_w1, rwkv_w_w2, rwkv_a0, rwkv_w_a1, rwkv_w_a2, rwkv_w_g1, rwkv_w_g2, rwkv_k_k, rwkv_k_a, rwkv_r_k, rwkv_lnx_w, rwkv_lnx_b, rwkv_w_out, ffn_w1, ffn_w2):
    p = _prep_params(dict(
        norm_mix=norm_mix, norm_ffn=norm_ffn, norm_final=norm_final,
        lru_w_in=lru_w_in, lru_conv_w=lru_conv_w, lru_conv_b=lru_conv_b, lru_w_r=lru_w_r, lru_b_r=lru_b_r,
        lru_w_i=lru_w_i, lru_b_i=lru_b_i, lru_lambda=lru_lambda, lru_w_out=lru_w_out,
        ssm_w_in=ssm_w_in, ssm_conv_w=ssm_conv_w, ssm_conv_b=ssm_conv_b, ssm_dt_bias=ssm_dt_bias,
        ssm_a_log=ssm_a_log, ssm_d=ssm_d, ssm_norm_w=ssm_norm_w, ssm_w_out=ssm_w_out,
        rwkv_mu=rwkv_mu, rwkv_w_rkv=rwkv_w_rkv, rwkv_w0=rwkv_w0, rwkv_w_w1=rwkv_w_w1, rwkv_w_w2=rwkv_w_w2,
        rwkv_a0=rwkv_a0, rwkv_w_a1=rwkv_w_a1, rwkv_w_a2=rwkv_w_a2, rwkv_w_g1=rwkv_w_g1, rwkv_w_g2=rwkv_w_g2,
        rwkv_k_k=rwkv_k_k, rwkv_k_a=rwkv_k_a, rwkv_r_k=rwkv_r_k, rwkv_lnx_w=rwkv_lnx_w, rwkv_lnx_b=rwkv_lnx_b,
        rwkv_w_out=rwkv_w_out, ffn_w1=ffn_w1, ffn_w2=ffn_w2))
    seq_p = x_prompt.shape[1]
    seq_s = x_sample.shape[1]
    cfg_p = dict(lru_nb=8, lru_tl=64, ssd_q=min(SSM_CHUNK, seq_p), ssd_fused=True,
                 rwkv_proj_nb=1, rwkv_nb=1, rwkv_tl=256, rwkv_chunk=RWKV_CHUNK)
    cfg_s = dict(lru_nb=32, lru_tl=seq_s, ssd_q=seq_s, ssd_fused=False,
                 rwkv_proj_nb=32, rwkv_nb=8, rwkv_tl=seq_s, rwkv_chunk=seq_s)
    states = dict(lru_conv=state_lru_conv, lru_h=state_lru_h, ssm_conv=state_ssm_conv, ssm=state_ssm,
                  rwkv_shift=state_rwkv_shift, rwkv_wkv=state_rwkv_wkv)
    groups = [_Group(x_prompt, None, 0, cfg_p), _Group(x_sample, states, PAST_LEN, cfg_s)]
    (y_p, lc_p, lh_p, sc_p, ss_p, rs_p, rw_p), (y_s, lc_s, lh_s, sc_s, ss_s, rs_s, rw_s) = _run_trunk(groups, p, 512)
    return (y_p, y_s, lc_p, lc_s, lh_p, lh_s, sc_p, sc_s, ss_p, ss_s, rs_p, rs_s, rw_p, rw_s)
```

```python
import functools
import math

import jax
import jax.numpy as jnp
from jax import lax
from jax.experimental import pallas as pl
from jax.experimental.pallas import tpu as pltpu

F32 = jnp.float32
BF16 = jnp.bfloat16

NORM_EPS = 1e-6
CONV_W = 4
LRU_BLOCKS = 8
LRU_C = 8.0
SSM_HEADDIM = 64
SSM_GROUPS = 8
SSM_STATE = 128
SSM_CHUNK = 128
SSM_NORM_EPS = 1e-5
RWKV_HEAD = 64
GN_EPS = 64e-5
PAST_LEN = 16384

V7X_LANES = 128
V7X_SUBLANES = 8
V7X_VMEM_BYTES = 64 * 1024 * 1024
VMEM_LIMIT = V7X_VMEM_BYTES - 8 * 1024 * 1024

RWKV_CHUNK = 64
CONV_PAD = V7X_SUBLANES


def _bdot(a, b):
    return jnp.dot(a.astype(BF16), b.astype(BF16), preferred_element_type=F32)


def _bdot_nt(a, b):
    return lax.dot_general(a.astype(BF16), b.astype(BF16), (((1,), (1,)), ((), ())),
                           preferred_element_type=F32)


def _bdot_tn(a, b):
    return lax.dot_general(a.astype(BF16), b.astype(BF16), (((0,), (0,)), ((), ())),
                           preferred_element_type=F32)


def _rms(x, g):
    return x * lax.rsqrt(jnp.mean(x * x, axis=-1, keepdims=True) + NORM_EPS) * g


def _sigmoid(x):
    return 0.5 * jnp.tanh(0.5 * x) + 0.5


def _softplus(x):
    return jnp.maximum(x, 0.0) + jnp.log1p(jnp.exp(-jnp.abs(x)))


def _silu(x):
    return x * _sigmoid(x)


def _gelu_tanh(x):
    c = math.sqrt(2.0 / math.pi)
    return 0.5 * x * (1.0 + jnp.tanh(c * (x + 0.044715 * (x * x * x))))


def _row_index(rows, period):
    return lax.broadcasted_iota(jnp.int32, (rows, 1), 0) & (period - 1)


def _cumsum_rows(x, t_in, period):
    d = 1
    while d < period:
        x = x + jnp.where(t_in >= d, pltpu.roll(x, d, axis=0), 0.0)
        d *= 2
    return x


def _head_sum(x, head):
    rows, width = x.shape
    lo = lax.broadcasted_iota(jnp.int32, (1, V7X_LANES), 1) < head
    parts = []
    for j in range(width // V7X_LANES):
        seg = x[:, j * V7X_LANES:(j + 1) * V7X_LANES]
        s_lo = jnp.sum(jnp.where(lo, seg, 0.0), axis=-1, keepdims=True)
        s_hi = jnp.sum(jnp.where(lo, 0.0, seg), axis=-1, keepdims=True)
        parts.append(jnp.where(lo, s_lo, s_hi))
    return jnp.concatenate(parts, axis=1)


def _const_spec(shape):
    nd = len(shape)
    return pl.BlockSpec(shape, lambda *_: (0,) * nd, pipeline_mode=pl.Buffered(1))


def _params(sem):
    return pltpu.CompilerParams(dimension_semantics=sem, vmem_limit_bytes=VMEM_LIMIT)


def _proj_kernel(x_ref, g_ref, *refs, n_out, col_chunk):
    w_refs, o_refs = refs[:n_out], refs[n_out:]
    u = _rms(x_ref[...], g_ref[...]).astype(BF16)
    for w_ref, o_ref in zip(w_refs, o_refs):
        n = w_ref.shape[1]
        step = min(col_chunk, n)
        for c0 in range(0, n, step):
            o_ref[:, c0:c0 + step] = jnp.dot(u, w_ref[:, c0:c0 + step], preferred_element_type=F32)


def _norm_proj(x, g, weights, *, tm):
    t, d = x.shape
    assert t % tm == 0
    n_out = len(weights)
    return pl.pallas_call(
        functools.partial(_proj_kernel, n_out=n_out, col_chunk=1024),
        out_shape=[jax.ShapeDtypeStruct((t, w.shape[1]), F32) for w in weights],
        grid=(t // tm,),
        in_specs=[pl.BlockSpec((tm, d), lambda i: (i, 0)), _const_spec((1, d))]
        + [_const_spec(w.shape) for w in weights],
        out_specs=[pl.BlockSpec((tm, w.shape[1]), lambda i: (i, 0)) for w in weights],
        compiler_params=_params(("parallel",)),
        name="norm_proj",
    )(x, g, *weights)


def _ffn_kernel(xa_ref, xb_ref, g_ref, w1_ref, w2_ref, gf_ref, oa_ref, ob_ref, *, n_a, f_chunk, final_norm):
    def run(x_ref, o_ref):
        x = x_ref[...]
        u = _rms(x, g_ref[...]).astype(BF16)
        acc = x
        for c0 in range(0, w1_ref.shape[1], f_chunk):
            h = jnp.dot(u, w1_ref[:, c0:c0 + f_chunk], preferred_element_type=F32)
            h = jnp.square(jnp.maximum(h, 0.0)).astype(BF16)
            acc = acc + jnp.dot(h, w2_ref[c0:c0 + f_chunk, :], preferred_element_type=F32)
        if final_norm:
            acc = _rms(acc, gf_ref[...])
        o_ref[...] = acc

    i = pl.program_id(0)

    @pl.when(i < n_a)
    def _():
        run(xa_ref, oa_ref)

    @pl.when(i >= n_a)
    def _():
        run(xb_ref, ob_ref)


def _ffn(xa, xb, g, w1, w2, g_final, *, tm, final_norm):
    d = xa.shape[1]
    f = w1.shape[1]
    assert xa.shape[0] % tm == 0 and xb.shape[0] % tm == 0
    n_a, n_b = xa.shape[0] // tm, xb.shape[0] // tm
    a_map = lambda i: (jnp.minimum(i, n_a - 1), 0)
    b_map = lambda i: (jnp.maximum(i - n_a, 0), 0)
    return pl.pallas_call(
        functools.partial(_ffn_kernel, n_a=n_a, f_chunk=1024, final_norm=final_norm),
        out_shape=[jax.ShapeDtypeStruct(xa.shape, F32), jax.ShapeDtypeStruct(xb.shape, F32)],
        grid=(n_a + n_b,),
        in_specs=[pl.BlockSpec((tm, d), a_map), pl.BlockSpec((tm, d), b_map), _const_spec((1, d)),
                  _const_spec((d, f)), _const_spec((f, d)), _const_spec((1, d))],
        out_specs=[pl.BlockSpec((tm, d), a_map), pl.BlockSpec((tm, d), b_map)],
        compiler_params=_params(("arbitrary",)),
        name="ffn",
    )(xa, xb, g, w1, w2, g_final)


def _conv_taps(xe_ref, cw, cb, tl):
    nd = len(xe_ref.shape)

    def win(off):
        if nd == 3:
            return xe_ref[:, off:off + tl, :]
        return xe_ref[off:off + tl, :]

    base = CONV_PAD - (CONV_W - 1)
    y = cb + win(base) * cw[0:1, :]
    for k in range(1, CONV_W):
        y = y + win(base + k) * cw[k:k + 1, :]
    return y


def _lru_kernel(x_ref, g_ref, win_ref, cs_ref, h0_ref, cw_ref, cb_ref, wr_ref, br_ref, wi_ref, bi_ref, lam_ref, wo_ref,
                o_ref, cso_ref, ho_ref, xe_ref, hs_ref, hc_ref, *, nb, tl, start_pos):
    c = pl.program_id(1)
    rows = nb * tl
    d = x_ref.shape[2]
    blk = d // LRU_BLOCKS
    ncol = d // V7X_LANES
    pitch = CONV_PAD + tl
    cols = [slice(j * V7X_LANES, (j + 1) * V7X_LANES) for j in range(ncol)]

    @pl.when(c == 0)
    def _():
        for j in range(ncol):
            for s in range(nb):
                xe_ref[j, s * pitch:s * pitch + CONV_PAD, :] = cs_ref[s, :, cols[j]]
        hc_ref[...] = h0_ref[...]

    x = x_ref[...].reshape(rows, d)
    u = _rms(x, g_ref[...]).astype(BF16)
    xb = jnp.dot(u, win_ref[:, 0:d], preferred_element_type=F32)
    for j in range(ncol):
        for s in range(nb):
            xe_ref[j, s * pitch + CONV_PAD:(s + 1) * pitch, :] = xb[s * tl:(s + 1) * tl, cols[j]]

    cw = cw_ref[...]
    cb = cb_ref[...]
    base = CONV_PAD - (CONV_W - 1)
    slabs = []
    for t in range(tl):
        parts = []
        for j in range(ncol):
            acc = cb[:, cols[j]] + xe_ref[j, pl.ds(base + t, nb, stride=pitch), :] * cw[0:1, cols[j]]
            for k in range(1, CONV_W):
                acc = acc + xe_ref[j, pl.ds(base + t + k, nb, stride=pitch), :] * cw[k:k + 1, cols[j]]
            parts.append(acc)
        slabs.append(jnp.concatenate(parts, axis=1))
    xc = jnp.concatenate(slabs, axis=0)
    for j in range(ncol):
        for s in range(nb):
            tail = xe_ref[j, s * pitch + tl:(s + 1) * pitch, :]
            xe_ref[j, s * pitch:s * pitch + CONV_PAD, :] = tail
            cso_ref[s, :, cols[j]] = tail

    xcb = xc.astype(BF16)
    r_pre = jnp.concatenate(
        [jnp.dot(xcb[:, j * blk:(j + 1) * blk], wr_ref[j], preferred_element_type=F32) for j in range(LRU_BLOCKS)],
        axis=1) + br_ref[...]
    i_pre = jnp.concatenate(
        [jnp.dot(xcb[:, j * blk:(j + 1) * blk], wi_ref[j], preferred_element_type=F32) for j in range(LRU_BLOCKS)],
        axis=1) + bi_ref[...]
    r = _sigmoid(r_pre)
    gi = _sigmoid(i_pre)
    log_a = (-LRU_C) * r * _softplus(-lam_ref[...])
    a = jnp.exp(log_a)
    m2 = -jnp.tanh(log_a) * (a * a + 1.0)
    mult = jnp.where(m2 > 0.0, m2 * lax.rsqrt(m2), 0.0)
    if start_pos == 0:
        reset = (lax.broadcasted_iota(jnp.int32, (rows, 1), 0) < nb) & (c == 0)
        a = jnp.where(reset, 0.0, a)
        mult = jnp.where(reset, 1.0, mult)
    b = mult * (gi * xc)

    h = hc_ref[...]
    for t in range(tl):
        h = a[t * nb:(t + 1) * nb] * h + b[t * nb:(t + 1) * nb]
        for j in range(ncol):
            hs_ref[j, pl.ds(t, nb, stride=pitch), :] = h[:, cols[j]]
    hc_ref[...] = h
    ho_ref[...] = h
    h_bm = jnp.concatenate(
        [jnp.concatenate([hs_ref[j, s * pitch:s * pitch + tl, :] for s in range(nb)], axis=0) for j in range(ncol)],
        axis=1)

    gate = jnp.dot(u, win_ref[:, d:2 * d], preferred_element_type=F32)
    y = (h_bm * _gelu_tanh(gate)).astype(BF16)
    o_ref[...] = (x + jnp.dot(y, wo_ref[...], preferred_element_type=F32)).reshape(nb, tl, d)


def _lru_mixer(x3, g, w_in, conv_state, h0, cw, cb, wr, br, wi, bi, lam, wo, *, nb, tl, start_pos):
    batch, seq, d = x3.shape
    nchunk = seq // tl
    assert batch % nb == 0 and seq % tl == 0 and nb % V7X_SUBLANES == 0 and tl % V7X_SUBLANES == 0
    ncol = d // V7X_LANES
    blk_map = lambda i, c: (i, c, 0)
    seq_map = lambda i, c: (i, 0, 0)
    row_map = lambda i, c: (i, 0)
    return pl.pallas_call(
        functools.partial(_lru_kernel, nb=nb, tl=tl, start_pos=start_pos),
        out_shape=[jax.ShapeDtypeStruct((batch, seq, d), F32),
                   jax.ShapeDtypeStruct((batch, CONV_PAD, d), F32),
                   jax.ShapeDtypeStruct((batch, d), F32)],
        grid=(batch // nb, nchunk),
        in_specs=[pl.BlockSpec((nb, tl, d), blk_map), _const_spec(g.shape), _const_spec(w_in.shape),
                  pl.BlockSpec((nb, CONV_PAD, d), seq_map), pl.BlockSpec((nb, d), row_map),
                  _const_spec(cw.shape), _const_spec(cb.shape), _const_spec(wr.shape), _const_spec(br.shape),
                  _const_spec(wi.shape), _const_spec(bi.shape), _const_spec(lam.shape), _const_spec(wo.shape)],
        out_specs=[pl.BlockSpec((nb, tl, d), blk_map), pl.BlockSpec((nb, CONV_PAD, d), seq_map),
                   pl.BlockSpec((nb, d), row_map)],
        scratch_shapes=[pltpu.VMEM((ncol, nb * (CONV_PAD + tl), V7X_LANES), F32),
                        pltpu.VMEM((ncol, nb * (CONV_PAD + tl), V7X_LANES), F32),
                        pltpu.VMEM((nb, d), F32)],
        compiler_params=_params(("parallel", "arbitrary")),
        name="lru_mixer",
    )(x3, g, w_in, conv_state, h0, cw, cb, wr, br, wi, bi, lam, wo)


def _split_bf16(x, terms):
    parts = []
    for _ in range(terms - 1):
        hi = x.astype(BF16).astype(F32)
        parts.append(hi)
        x = x - hi
    parts.append(x)
    return parts


def _ssd_kernel(x_ref, g_ref, pz_ref, pxbc_ref, pdt_ref, cs_ref, s0_ref, cw_ref, cb_ref, dtb_ref, alog_ref, dskx_ref, nw_ref,
                ex_ref, wo_ref, o_ref, cso_ref, so_ref, xe_ref, st_ref, *, q, zero_state, fused_proj):
    c = pl.program_id(1)
    d_ssm = pz_ref.shape[1]
    n = SSM_STATE
    hpg = d_ssm // SSM_HEADDIM // SSM_GROUPS
    gw = hpg * SSM_HEADDIM
    nblk = d_ssm // n

    @pl.when(c == 0)
    def _():
        xe_ref[0:CONV_PAD, :] = cs_ref[0]
        if zero_state:
            st_ref[...] = jnp.zeros(st_ref.shape, F32)
        else:
            for j in range(nblk):
                st_ref[:, j * n:(j + 1) * n] = s0_ref[0, j * n:(j + 1) * n, :].T

    x = x_ref[...]
    if fused_proj:
        u = _rms(x, g_ref[...]).astype(BF16)
        for c0 in range(0, pxbc_ref.shape[1], 1024):
            xe_ref[CONV_PAD:CONV_PAD + q, c0:c0 + 1024] = jnp.dot(u, pxbc_ref[:, c0:c0 + 1024],
                                                                  preferred_element_type=F32)
        dt_pre = jnp.dot(u, pdt_ref[...], preferred_element_type=F32)
    else:
        xe_ref[CONV_PAD:CONV_PAD + q, :] = pxbc_ref[...]
        dt_pre = pdt_ref[...]
    xbc = _silu(_conv_taps(xe_ref, cw_ref[...], cb_ref[...], q))
    tail = xe_ref[q:q + CONV_PAD, :]
    xe_ref[0:CONV_PAD, :] = tail
    cso_ref[0] = tail

    xs = xbc[:, 0:d_ssm]
    bm = xbc[:, d_ssm:d_ssm + SSM_GROUPS * n]
    cm = xbc[:, d_ssm + SSM_GROUPS * n:]
    dt = _softplus(dt_pre + dtb_ref[...])
    da = dt * (-jnp.exp(alog_ref[...]))
    t_in = _row_index(q, q)
    acs = _cumsum_rows(da, t_in, q)
    if q < V7X_LANES:
        acs_sq = jnp.concatenate([acs, jnp.zeros((V7X_LANES - q, V7X_LANES), F32)], axis=0)
    else:
        acs_sq = acs
    acs_t = acs_sq.T

    lhs = jnp.concatenate(_split_bf16(dt, 2) + _split_bf16(acs, 3), axis=0).astype(BF16)
    ex = jnp.dot(lhs, ex_ref[...], preferred_element_type=F32)
    dt_x = ex[0:q] + ex[q:2 * q]
    acs_x = ex[2 * q:3 * q] + ex[3 * q:4 * q] + ex[4 * q:5 * q]
    last_x = acs_x[q - 1:q, :]
    xdt = (xs * dt_x).astype(BF16)
    xdtd = (xs * (dt_x * jnp.exp(last_x - acs_x))).astype(BF16)

    tri = lax.broadcasted_iota(jnp.int32, (q, q), 0) >= lax.broadcasted_iota(jnp.int32, (q, q), 1)
    lane_head = lax.broadcasted_iota(jnp.int32, (1, gw), 1) // SSM_HEADDIM
    groups = range(SSM_GROUPS)
    bgs = [bm[:, g * n:(g + 1) * n].astype(BF16) for g in groups]
    cgs = [cm[:, g * n:(g + 1) * n].astype(BF16) for g in groups]
    cb_mats = [_bdot_nt(cgs[g], bgs[g]) for g in groups]
    y_off = [_bdot(cgs[g], st_ref[:, g * gw:(g + 1) * gw]) for g in groups]
    y_diag = []
    for g in groups:
        xdt_g = xdt[:, g * gw:(g + 1) * gw]
        acc = None
        for e in range(hpg):
            h = g * hpg + e
            seg = acs[:, h:h + 1] - acs_t[h:h + 1, 0:q]
            l_mat = jnp.exp(jnp.where(tri, seg, -jnp.inf))
            part = jnp.dot((cb_mats[g] * l_mat).astype(BF16), jnp.where(lane_head == e, xdt_g, jnp.zeros_like(xdt_g)),
                           preferred_element_type=F32)
            acc = part if acc is None else acc + part
        y_diag.append(acc)
    st_new = [_bdot_tn(bgs[g], xdtd[:, g * gw:(g + 1) * gw]) for g in groups]
    e_last_x = jnp.exp(last_x)
    for g in groups:
        cols = slice(g * gw, (g + 1) * gw)
        st_ref[:, cols] = st_ref[:, cols] * e_last_x[:, cols] + st_new[g]

    @pl.when(c == pl.num_programs(1) - 1)
    def _():
        for j in range(nblk):
            so_ref[0, j * n:(j + 1) * n, :] = st_ref[:, j * n:(j + 1) * n].T

    y = jnp.concatenate(y_diag, axis=1) + jnp.concatenate(y_off, axis=1) * jnp.exp(acs_x) + dskx_ref[...] * xs
    y = y * _silu(jnp.dot(u, pz_ref[...], preferred_element_type=F32) if fused_proj else pz_ref[...])
    norm_parts = []
    for g in groups:
        yg = y[:, g * gw:(g + 1) * gw]
        norm_parts.append(yg * lax.rsqrt(jnp.mean(yg * yg, axis=-1, keepdims=True) + SSM_NORM_EPS))
    yn = (jnp.concatenate(norm_parts, axis=1) * nw_ref[...]).astype(BF16)
    o_ref[...] = x + jnp.dot(yn, wo_ref[...], preferred_element_type=F32)


def _ssd_mixer(x, g, w_z, w_xbc, w_dt, conv_state, s0, cw, cb, dtb, alog, dskx, nw, expand, wo, *, batch, seq, q,
               fused_proj):
    d = x.shape[1]
    d_ssm = w_z.shape[1]
    cdim = w_xbc.shape[1]
    nchunk = seq // q
    assert seq % q == 0
    zero_state = s0 is None
    if zero_state:
        s0 = jnp.zeros((1, d_ssm, SSM_STATE), F32)
    row_map = lambda b, c: (b * nchunk + c, 0)
    seq_map = lambda b, c: (b, 0, 0)
    s0_map = (lambda b, c: (0, 0, 0)) if zero_state else seq_map
    if fused_proj:
        proj = (w_z, w_xbc, w_dt)
        proj_specs = [_const_spec(w.shape) for w in proj]
    else:
        proj = _norm_proj(x, g, [w_z, w_xbc, w_dt], tm=min(256, batch * seq))
        proj_specs = [pl.BlockSpec((q, a.shape[1]), row_map) for a in proj]
    return pl.pallas_call(
        functools.partial(_ssd_kernel, q=q, zero_state=zero_state, fused_proj=fused_proj),
        out_shape=[jax.ShapeDtypeStruct((batch * seq, d), F32),
                   jax.ShapeDtypeStruct((batch, CONV_PAD, cdim), F32),
                   jax.ShapeDtypeStruct((batch, d_ssm, SSM_STATE), F32)],
        grid=(batch, nchunk),
        in_specs=[pl.BlockSpec((q, d), row_map), _const_spec(g.shape)] + proj_specs
        + [pl.BlockSpec((1, CONV_PAD, cdim), seq_map), pl.BlockSpec((1, d_ssm, SSM_STATE), s0_map),
                  _const_spec(cw.shape), _const_spec(cb.shape), _const_spec(dtb.shape), _const_spec(alog.shape),
                  _const_spec(dskx.shape), _const_spec(nw.shape), _const_spec(expand.shape), _const_spec(wo.shape)],
        out_specs=[pl.BlockSpec((q, d), row_map), pl.BlockSpec((1, CONV_PAD, cdim), seq_map),
                   pl.BlockSpec((1, d_ssm, SSM_STATE), seq_map)],
        scratch_shapes=[pltpu.VMEM((CONV_PAD + q, cdim), F32), pltpu.VMEM((SSM_STATE, d_ssm), F32)],
        compiler_params=_params(("parallel", "arbitrary")),
        name="ssd_mixer",
    )(x, g, *proj, conv_state, s0, cw, cb, dtb, alog, dskx, nw, expand, wo)


def _rwkv_proj_kernel(x_ref, sh_ref, g_ref, mu_ref, wrkv_ref, w0_ref, w1_ref, w2_ref, a0_ref, a1_ref, a2_ref,
                      g1_ref, g2_ref, kkw_ref, kaw_ref,
                      r_o, k_o, v_o, kk_o, bb_o, lw_o, g_o, sho_ref, carry_ref, *, nb, tl, time_major):
    c = pl.program_id(1)
    rows = x_ref.shape[0]
    d = x_ref.shape[1]

    @pl.when(c == 0)
    def _():
        carry_ref[...] = sh_ref[...]

    u = _rms(x_ref[...], g_ref[...])
    if time_major:
        prev = carry_ref[...]
        last = u
    else:
        t_in = _row_index(rows, tl)
        prev = jnp.where(t_in == 0, jnp.broadcast_to(carry_ref[...], (nb, tl, d)).reshape(rows, d),
                         pltpu.roll(u, 1, axis=0))
        last = u.reshape(nb, tl, d)[:, tl - 1:tl, :]
    carry_ref[...] = last
    sho_ref[...] = last

    diff = prev - u
    mu = mu_ref[...]

    def mixed(s):
        return (u + diff * mu[s:s + 1, :]).astype(BF16)

    r = jnp.dot(mixed(0), wrkv_ref[0], preferred_element_type=F32)
    k = jnp.dot(mixed(1), wrkv_ref[1], preferred_element_type=F32)
    v_o[...] = jnp.dot(mixed(2), wrkv_ref[2], preferred_element_type=F32)
    w_pre = w0_ref[...] + _bdot(jnp.tanh(jnp.dot(mixed(3), w1_ref[...], preferred_element_type=F32)), w2_ref[...])
    sp = jnp.maximum(-w_pre, 0.0) + jnp.log(1.0 + jnp.exp(-jnp.abs(w_pre)))
    lw_o[...] = -jnp.exp(-sp - 0.5)
    a = _sigmoid(a0_ref[...] + _bdot(jnp.dot(mixed(4), a1_ref[...], preferred_element_type=F32), a2_ref[...]))
    g_o[...] = _bdot(_sigmoid(jnp.dot(mixed(5), g1_ref[...], preferred_element_type=F32)), g2_ref[...])
    kk = k * kkw_ref[...]
    kk = kk * jnp.minimum(lax.rsqrt(_head_sum(kk * kk, RWKV_HEAD)), 1e12)
    r_o[...] = r
    k_o[...] = k * (1.0 + (a - 1.0) * kaw_ref[...])
    kk_o[...] = kk
    bb_o[...] = kk * a


def _rwkv_proj(x, shift0, g, mu, wrkv, w0, w1, w2, a0, a1, a2, g1, g2, kkw, kaw, *, batch, seq, nb, tl, time_major):
    d = x.shape[1]
    consts = (g, mu, wrkv, w0, w1, w2, a0, a1, a2, g1, g2, kkw, kaw)
    if time_major:
        rows, grid = batch, (1, seq)
        row_map = lambda i, c: (c, 0)
        sh_shape, sh_block, sh_map = (batch, d), (batch, d), (lambda i, c: (0, 0))
    else:
        nchunk = seq // tl
        assert batch % nb == 0 and seq % tl == 0 and (nb == 1 or nchunk == 1)
        rows, grid = nb * tl, (batch // nb, nchunk)
        row_map = lambda i, c: (i * nchunk + c, 0)
        sh_shape, sh_block, sh_map = (batch, 1, d), (nb, 1, d), (lambda i, c: (i, 0, 0))
    return pl.pallas_call(
        functools.partial(_rwkv_proj_kernel, nb=nb, tl=tl, time_major=time_major),
        out_shape=[jax.ShapeDtypeStruct((batch * seq, d), F32)] * 7 + [jax.ShapeDtypeStruct(sh_shape, F32)],
        grid=grid,
        in_specs=[pl.BlockSpec((rows, d), row_map), pl.BlockSpec(sh_block, sh_map)]
        + [_const_spec(w.shape) for w in consts],
        out_specs=[pl.BlockSpec((rows, d), row_map)] * 7 + [pl.BlockSpec(sh_block, sh_map)],
        scratch_shapes=[pltpu.VMEM(sh_block, F32)],
        compiler_params=_params(("parallel", "arbitrary")),
        name="rwkv_proj",
    )(x, shift0, *consts)


def _rwkv_chunk_kernel(r_ref, k_ref, v_ref, kk_ref, bb_ref, lw_ref, g_ref, x_ref, s0_ref, rk_ref, lnw_ref, lnb_ref,
                       wo_ref, o_ref, so_ref, s_ref, og_ref, *, nb, tl, chunk, zero_state):
    c = pl.program_id(1)
    d = x_ref.shape[1]
    hd = RWKV_HEAD
    heads = d // hd
    cpb = tl // chunk
    n_iter = nb * cpb

    @pl.when(c == 0)
    def _():
        if zero_state:
            s_ref[...] = jnp.zeros(s_ref.shape, F32)
        else:
            s_ref[...] = s0_ref[...]

    t_in = _row_index(chunk, chunk)
    ri = lax.broadcasted_iota(jnp.int32, (2 * chunk, 2 * chunk), 0)
    ci = lax.broadcasted_iota(jnp.int32, (2 * chunk, 2 * chunk), 1) & (chunk - 1)
    blk_mask = jnp.where(ri < chunk, ri - 1, ri - chunk) >= ci

    def body(it, carry):
        seq_i = it if cpb == 1 else 0
        rows = pl.ds(pl.multiple_of(it * chunk, chunk), chunk)
        r = r_ref[rows, :]
        k = k_ref[rows, :]
        v = v_ref[rows, :]
        kk = kk_ref[rows, :]
        bb = bb_ref[rows, :]
        lw = lw_ref[rows, :]
        cum = _cumsum_rows(lw, t_in, chunk)
        c_last = cum[chunk - 1:chunk, :]
        g_last = jnp.exp(c_last)
        e_inv = jnp.exp(-cum)
        e_end = jnp.exp(c_last - cum)
        p_st = jnp.concatenate([kk * jnp.exp(cum - lw), r * jnp.exp(cum)], axis=0).astype(BF16)
        q_st = jnp.concatenate([bb * e_inv, k * e_inv], axis=0).astype(BF16)
        e_st = jnp.concatenate([k * e_end, bb * e_end], axis=0).astype(BF16)
        zv_st = jnp.concatenate([jnp.zeros_like(v), v], axis=0).astype(BF16)
        hs = range(heads)
        sls = [slice(h * hd, (h + 1) * hd) for h in hs]
        s_l = [s_ref[seq_i, sl, :] for sl in sls]
        g_m = [jnp.where(blk_mask, _bdot_nt(p_st[:, sl], q_st[:, sl]), 0.0) for sl in sls]
        ps = [_bdot_nt(p_st[:, sls[h]], s_l[h]) for h in hs]
        t1 = [_bdot(g_m[h], zv_st[:, sls[h]]) for h in hs]
        nmat = [-g_m[h][0:chunk, 0:chunk] for h in hs]
        w_l = [ps[h][0:chunk] + t1[h][0:chunk] for h in hs]
        u_l = [w_l[h] + _bdot(nmat[h], w_l[h]) for h in hs]
        m = 2
        while m < chunk:
            nmat = [_bdot(x, x) for x in nmat]
            u_l = [u_l[h] + _bdot(nmat[h], u_l[h]) for h in hs]
            m *= 2
        o_parts = [ps[h][chunk:] + t1[h][chunk:] - _bdot(g_m[h][chunk:, 0:chunk], u_l[h]) for h in hs]
        for h in hs:
            vu = jnp.concatenate([v[:, sls[h]], -u_l[h]], axis=0)
            s_ref[seq_i, sls[h], :] = s_l[h] * g_last[:, sls[h]] + _bdot_tn(vu, e_st[:, sls[h]])
        o = jnp.concatenate(o_parts, axis=1)
        mean = _head_sum(o, hd) * (1.0 / hd)
        oc = o - mean
        var = _head_sum(oc * oc, hd) * (1.0 / hd)
        on = oc * lax.rsqrt(var + GN_EPS) * lnw_ref[...] + lnb_ref[...]
        bonus = _head_sum(r * k * rk_ref[...], hd) * v
        og_ref[rows, :] = (on + bonus) * g_ref[rows, :]
        return carry

    lax.fori_loop(0, n_iter, body, 0)

    @pl.when(c == pl.num_programs(1) - 1)
    def _():
        so_ref[...] = s_ref[...]

    o_ref[...] = x_ref[...] + _bdot(og_ref[...], wo_ref[...])


def _rwkv_mixer(x, r, k, v, kk, bb, lw, g, s0, rk, lnw, lnb, wo, *, batch, seq, nb, tl, chunk):
    d = x.shape[1]
    nchunk = seq // tl
    assert batch % nb == 0 and seq % tl == 0 and tl % chunk == 0 and (nb == 1 or tl == chunk)
    rows = nb * tl
    zero_state = s0 is None
    if zero_state:
        s0 = jnp.zeros((nb, d, RWKV_HEAD), F32)
    row_map = lambda i, c: (i * nchunk + c, 0)
    seq_map = lambda i, c: (i, 0, 0)
    s0_map = (lambda i, c: (0, 0, 0)) if zero_state else seq_map
    row_spec = pl.BlockSpec((rows, d), row_map)
    return pl.pallas_call(
        functools.partial(_rwkv_chunk_kernel, nb=nb, tl=tl, chunk=chunk, zero_state=zero_state),
        out_shape=[jax.ShapeDtypeStruct((batch * seq, d), F32), jax.ShapeDtypeStruct((batch, d, RWKV_HEAD), F32)],
        grid=(batch // nb, nchunk),
        in_specs=[row_spec] * 8 + [pl.BlockSpec((nb, d, RWKV_HEAD), s0_map),
                                   _const_spec(rk.shape), _const_spec(lnw.shape), _const_spec(lnb.shape),
                                   _const_spec(wo.shape)],
        out_specs=[row_spec, pl.BlockSpec((nb, d, RWKV_HEAD), seq_map)],
        scratch_shapes=[pltpu.VMEM((nb, d, RWKV_HEAD), F32), pltpu.VMEM((rows, d), F32)],
        compiler_params=_params(("parallel", "arbitrary")),
        name="rwkv_chunk",
    )(r, k, v, kk, bb, lw, g, x, s0, rk, lnw, lnb, wo)


def _rwkv_seq_kernel(r_ref, k_ref, v_ref, kk_ref, bb_ref, lw_ref, s0_ref, o_ref, so_ref, xt_ref, ot_ref, *, seq, batch):
    hd = RWKV_HEAD
    for i, ref in enumerate((r_ref, k_ref, v_ref, kk_ref, bb_ref, lw_ref)):
        for t in range(seq):
            tile = ref[t * batch:(t + 1) * batch, :]
            if ref is lw_ref:
                tile = jnp.exp(tile)
            xt_ref[i * seq + t] = tile.T
    sub = V7X_SUBLANES
    for hh in range(2):
        base = hh * hd
        for t in range(seq):
            src = s0_ref if t == 0 else so_ref
            r_t, k_t, kk_t, bb_t, w_t = (xt_ref[i * seq + t, base:base + hd, :] for i in (0, 1, 3, 4, 5))

            def body(vg, carry, src=src, t=t, base=base, hh=hh, r_t=r_t, k_t=k_t, kk_t=kk_t, bb_t=bb_t, w_t=w_t):
                v0 = pl.multiple_of(base + vg * sub, sub)
                v_tile = xt_ref[2 * seq + t, pl.ds(v0, sub), :]
                rows = []
                for i in range(sub):
                    off = pl.multiple_of((vg * sub + i) * hd, hd)
                    sv = src[hh, pl.ds(off, hd), :]
                    sa = jnp.sum(sv * kk_t, axis=0, keepdims=True)
                    sv = sv * w_t - sa * bb_t + v_tile[i:i + 1, :] * k_t
                    so_ref[hh, pl.ds(off, hd), :] = sv
                    rows.append(jnp.sum(sv * r_t, axis=0, keepdims=True))
                ot_ref[t, pl.ds(v0, sub), :] = jnp.concatenate(rows, axis=0)
                return carry

            lax.fori_loop(0, hd // sub, body, 0)
    for t in range(seq):
        o_ref[t * batch:(t + 1) * batch, :] = ot_ref[t].T


def _rwkv_seq(r, k, v, kk, bb, lw, s0, *, batch, seq):
    d = r.shape[1]
    assert batch == V7X_LANES and d % V7X_LANES == 0
    npair = d // V7X_LANES
    col_spec = pl.BlockSpec((seq * batch, V7X_LANES), lambda j: (0, j))
    st_spec = pl.BlockSpec((2, RWKV_HEAD * RWKV_HEAD, batch), lambda j: (j, 0, 0))
    return pl.pallas_call(
        functools.partial(_rwkv_seq_kernel, seq=seq, batch=batch),
        out_shape=[jax.ShapeDtypeStruct((seq * batch, d), F32), jax.ShapeDtypeStruct(s0.shape, F32)],
        grid=(npair,),
        in_specs=[col_spec] * 6 + [st_spec],
        out_specs=[col_spec, st_spec],
        scratch_shapes=[pltpu.VMEM((6 * seq, V7X_LANES, batch), F32), pltpu.VMEM((seq, V7X_LANES, batch), F32)],
        compiler_params=_params(("parallel",)),
        name="rwkv_seq",
    )(r, k, v, kk, bb, lw, s0)


def _rwkv_finish_kernel(o_ref, r_ref, k_ref, v_ref, g_ref, x_ref, rk_ref, lnw_ref, lnb_ref, wo_ref, out_ref):
    hd = RWKV_HEAD
    o = o_ref[...]
    mean = _head_sum(o, hd) * (1.0 / hd)
    oc = o - mean
    var = _head_sum(oc * oc, hd) * (1.0 / hd)
    on = oc * lax.rsqrt(var + GN_EPS) * lnw_ref[...] + lnb_ref[...]
    bonus = _head_sum(r_ref[...] * k_ref[...] * rk_ref[...], hd) * v_ref[...]
    out_ref[...] = x_ref[...] + _bdot((on + bonus) * g_ref[...], wo_ref[...])


def _rwkv_finish(o, r, k, v, g, x, rk, lnw, lnb, wo, *, tm):
    t, d = x.shape
    assert t % tm == 0
    row_spec = pl.BlockSpec((tm, d), lambda i: (i, 0))
    return pl.pallas_call(
        _rwkv_finish_kernel,
        out_shape=jax.ShapeDtypeStruct((t, d), F32),
        grid=(t // tm,),
        in_specs=[row_spec] * 6 + [_const_spec(rk.shape), _const_spec(lnw.shape), _const_spec(lnb.shape),
                                   _const_spec(wo.shape)],
        out_specs=row_spec,
        compiler_params=_params(("parallel",)),
        name="rwkv_finish",
    )(o, r, k, v, g, x, rk, lnw, lnb, wo)


def _pad_conv_state(state):
    return jnp.pad(state, ((0, 0), (CONV_PAD - (CONV_W - 1), 0), (0, 0)))


class _Group:
    def __init__(self, x3, states, start_pos, cfg):
        self.batch, self.seq, d = x3.shape
        self.x = x3.reshape(self.batch * self.seq, d)
        self.states, self.start_pos, self.cfg = states, start_pos, cfg
        self.out = dict(lru_conv=[], lru_h=[], ssm_conv=[], ssm=[], rwkv_shift=[], rwkv_wkv=[])

    def state(self, name, idx, zero_shape):
        return self.states[name][idx] if self.states else jnp.zeros(zero_shape, F32)


def _mixer_layer(grp, layer, idx, p):
    batch, seq, cfg = grp.batch, grp.seq, grp.cfg
    x = grp.x
    d = x.shape[1]
    g_mix = p["norm_mix"][layer][None, :]
    kind = layer % 3
    if kind == 0:
        cs = grp.state("lru_conv", idx, (batch, CONV_W - 1, d))
        h0 = grp.state("lru_h", idx, (batch, d))
        x3, cso, ho = _lru_mixer(
            x.reshape(batch, seq, d), g_mix, p["lru_w_in"][idx], _pad_conv_state(cs), h0,
            p["lru_conv_w"][idx], p["lru_conv_b"][idx][None, :], p["lru_w_r"][idx], p["lru_b_r"][idx][None, :],
            p["lru_w_i"][idx], p["lru_b_i"][idx][None, :], p["lru_lambda"][idx][None, :], p["lru_w_out"][idx],
            nb=cfg["lru_nb"], tl=cfg["lru_tl"], start_pos=grp.start_pos)
        grp.x = x3.reshape(batch * seq, d)
        grp.out["lru_conv"].append(cso[:, CONV_PAD - (CONV_W - 1):, :])
        grp.out["lru_h"].append(ho)
    elif kind == 1:
        cdim = p["ssm_w_xbc"].shape[2]
        cs = grp.state("ssm_conv", idx, (batch, CONV_W - 1, cdim))
        s0 = grp.states["ssm"][idx].reshape(batch, -1, SSM_STATE) if grp.states else None
        grp.x, cso, so = _ssd_mixer(
            x, g_mix, p["ssm_w_z"][idx], p["ssm_w_xbc"][idx], p["ssm_w_dt"][idx], _pad_conv_state(cs), s0,
            p["ssm_conv_w"][idx], p["ssm_conv_b"][idx][None, :], p["ssm_dt_bias"][idx], p["ssm_a_log"][idx],
            p["ssm_d_x"][idx], p["ssm_norm_w"][idx][None, :], p["ssm_expand"], p["ssm_w_out"][idx],
            batch=batch, seq=seq, q=cfg["ssd_q"], fused_proj=cfg["ssd_fused"])
        grp.out["ssm_conv"].append(cso[:, CONV_PAD - (CONV_W - 1):, :])
        grp.out["ssm"].append(so.reshape(batch, -1, SSM_HEADDIM, SSM_STATE))
    elif cfg["rwkv_stepwise"]:
        heads = d // RWKV_HEAD
        proj_w = (g_mix, p["rwkv_mu"][idx], p["rwkv_w_rkv"][idx], p["rwkv_w0"][idx][None, :],
                  p["rwkv_w_w1"][idx], p["rwkv_w_w2"][idx], p["rwkv_a0"][idx][None, :], p["rwkv_w_a1"][idx],
                  p["rwkv_w_a2"][idx], p["rwkv_w_g1"][idx], p["rwkv_w_g2"][idx], p["rwkv_k_k"][idx][None, :],
                  p["rwkv_k_a"][idx][None, :])
        x_tm = x.reshape(batch, seq, d).transpose(1, 0, 2).reshape(seq * batch, d)
        r, k, v, kk, bb, lw, g, sho = _rwkv_proj(x_tm, grp.state("rwkv_shift", idx, (batch, d)), *proj_w,
                                                 batch=batch, seq=seq, nb=1, tl=1, time_major=True)
        s0 = grp.state("rwkv_wkv", idx, (batch, heads, RWKV_HEAD, RWKV_HEAD))
        s0 = s0.transpose(1, 2, 3, 0).reshape(heads, RWKV_HEAD * RWKV_HEAD, batch)
        o, so = _rwkv_seq(r, k, v, kk, bb, lw, s0, batch=batch, seq=seq)
        x_tm = _rwkv_finish(o, r, k, v, g, x_tm, p["rwkv_r_k"][idx].reshape(1, d), p["rwkv_lnx_w"][idx][None, :],
                            p["rwkv_lnx_b"][idx][None, :], p["rwkv_w_out"][idx], tm=min(256, seq * batch))
        grp.x = x_tm.reshape(seq, batch, d).transpose(1, 0, 2).reshape(batch * seq, d)
        grp.out["rwkv_shift"].append(sho)
        grp.out["rwkv_wkv"].append(so.reshape(heads, RWKV_HEAD, RWKV_HEAD, batch).transpose(3, 0, 1, 2))
    else:
        sh0 = grp.state("rwkv_shift", idx, (batch, d))
        s0 = grp.states["rwkv_wkv"][idx].reshape(batch, d, RWKV_HEAD) if grp.states else None
        r, k, v, kk, bb, lw, g, sho = _rwkv_proj(
            x, sh0[:, None, :], g_mix, p["rwkv_mu"][idx], p["rwkv_w_rkv"][idx], p["rwkv_w0"][idx][None, :],
            p["rwkv_w_w1"][idx], p["rwkv_w_w2"][idx], p["rwkv_a0"][idx][None, :], p["rwkv_w_a1"][idx],
            p["rwkv_w_a2"][idx], p["rwkv_w_g1"][idx], p["rwkv_w_g2"][idx], p["rwkv_k_k"][idx][None, :],
            p["rwkv_k_a"][idx][None, :], batch=batch, seq=seq, nb=cfg["rwkv_proj_nb"], tl=cfg["rwkv_tl"],
            time_major=False)
        grp.x, so = _rwkv_mixer(
            x, r, k, v, kk, bb, lw, g, s0, p["rwkv_r_k"][idx].reshape(1, d), p["rwkv_lnx_w"][idx][None, :],
            p["rwkv_lnx_b"][idx][None, :], p["rwkv_w_out"][idx],
            batch=batch, seq=seq, nb=cfg["rwkv_nb"], tl=cfg["rwkv_tl"], chunk=cfg["rwkv_chunk"])
        grp.out["rwkv_shift"].append(sho[:, 0, :])
        grp.out["rwkv_wkv"].append(so.reshape(batch, d // RWKV_HEAD, RWKV_HEAD, RWKV_HEAD))


def _run_trunk(groups, p, tm_ffn):
    depth = p["norm_mix"].shape[0]
    counts = [0, 0, 0]
    for layer in range(depth):
        kind = layer % 3
        for grp in groups:
            _mixer_layer(grp, layer, counts[kind], p)
        counts[kind] += 1
        groups[0].x, groups[1].x = _ffn(
            groups[0].x, groups[1].x, p["norm_ffn"][layer][None, :], p["ffn_w1"][layer], p["ffn_w2"][layer],
            p["norm_final"][None, :], tm=tm_ffn, final_norm=(layer == depth - 1))
    stack = lambda parts: parts[0][None] if len(parts) == 1 else jnp.stack(parts)
    results = []
    for grp in groups:
        d = grp.x.shape[1]
        results.append([grp.x.reshape(grp.batch, grp.seq, d)]
                       + [stack(grp.out[name]) for name in ("lru_conv", "lru_h", "ssm_conv", "ssm", "rwkv_shift", "rwkv_wkv")])
    return results


def _prep_params(p):
    q = dict(p)
    for name in ("lru_w_in", "lru_w_r", "lru_w_i", "lru_w_out", "ssm_w_out", "rwkv_w_rkv", "rwkv_w_w1", "rwkv_w_w2",
                 "rwkv_w_a1", "rwkv_w_a2", "rwkv_w_g1", "rwkv_w_g2", "rwkv_w_out", "ffn_w1", "ffn_w2"):
        q[name] = p[name].astype(BF16)
    d_ssm = p["ssm_w_out"].shape[1]
    heads = p["ssm_dt_bias"].shape[1]
    w_in = p["ssm_w_in"]
    cdim = w_in.shape[2] - d_ssm - heads
    pad = V7X_LANES - heads
    q["ssm_w_z"] = w_in[:, :, :d_ssm].astype(BF16)
    q["ssm_w_xbc"] = w_in[:, :, d_ssm:d_ssm + cdim].astype(BF16)
    q["ssm_w_dt"] = jnp.pad(w_in[:, :, d_ssm + cdim:], ((0, 0), (0, 0), (0, pad))).astype(BF16)
    for name in ("ssm_dt_bias", "ssm_a_log"):
        q[name] = jnp.pad(p[name], ((0, 0), (0, pad)))[:, None, :]
    q["ssm_d_x"] = jnp.repeat(p["ssm_d"], SSM_HEADDIM, axis=1)[:, None, :]
    q["ssm_expand"] = (jnp.arange(V7X_LANES)[:, None] == jnp.arange(d_ssm)[None, :] // SSM_HEADDIM).astype(BF16)
    return q


def kernel(x_prompt, x_sample, state_lru_conv, state_lru_h, state_ssm_conv, state_ssm, state_rwkv_shift, state_rwkv_wkv, norm_mix, norm_ffn, norm_final, lru_w_in, lru_conv_w, lru_conv_b, lru_w_r, lru_b_r, lru_w_i, lru_b_i, lru_lambda, lru_w_out, ssm_w_in, ssm_conv_w, ssm_conv_b, ssm_dt_bias, ssm_a_log, ssm_d, ssm_norm_w, ssm_w_out, rwkv_mu, rwkv_w_rkv, rwkv_w0, rwkv_w_w1, rwkv_w_w2, rwkv_a0, rwkv_w_a1, rwkv_w_a2, rwkv_w_g1, rwkv_w_g2, rwkv_k_k, rwkv_k_a, rwkv_r_k, rwkv_lnx_w, rwkv_lnx_b, rwkv_w_out, ffn_w1, ffn_w2):
    p = _prep_params(dict(
        norm_mix=norm_mix, norm_ffn=norm_ffn, norm_final=norm_final,
        lru_w_in=lru_w_in, lru_conv_w=lru_conv_w, lru_conv_b=lru_conv_b, lru_w_r=lru_w_r, lru_b_r=lru_b_r,
        lru_w_i=lru_w_i, lru_b_i=lru_b_i, lru_lambda=lru_lambda, lru_w_out=lru_w_out,
        ssm_w_in=ssm_w_in, ssm_conv_w=ssm_conv_w, ssm_conv_b=ssm_conv_b, ssm_dt_bias=ssm_dt_bias,
        ssm_a_log=ssm_a_log, ssm_d=ssm_d, ssm_norm_w=ssm_norm_w, ssm_w_out=ssm_w_out,
        rwkv_mu=rwkv_mu, rwkv_w_rkv=rwkv_w_rkv, rwkv_w0=rwkv_w0, rwkv_w_w1=rwkv_w_w1, rwkv_w_w2=rwkv_w_w2,
        rwkv_a0=rwkv_a0, rwkv_w_a1=rwkv_w_a1, rwkv_w_a2=rwkv_w_a2, rwkv_w_g1=rwkv_w_g1, rwkv_w_g2=rwkv_w_g2,
        rwkv_k_k=rwkv_k_k, rwkv_k_a=rwkv_k_a, rwkv_r_k=rwkv_r_k, rwkv_lnx_w=rwkv_lnx_w, rwkv_lnx_b=rwkv_lnx_b,
        rwkv_w_out=rwkv_w_out, ffn_w1=ffn_w1, ffn_w2=ffn_w2))
    seq_p = x_prompt.shape[1]
    seq_s = x_sample.shape[1]
    cfg_p = dict(lru_nb=8, lru_tl=64, ssd_q=min(SSM_CHUNK, seq_p), ssd_fused=True,
                 rwkv_stepwise=False, rwkv_proj_nb=1, rwkv_nb=1, rwkv_tl=256, rwkv_chunk=RWKV_CHUNK)
    cfg_s = dict(lru_nb=32, lru_tl=seq_s, ssd_q=seq_s, ssd_fused=False,
                 rwkv_stepwise=True)
    states = dict(lru_conv=state_lru_conv, lru_h=state_lru_h, ssm_conv=state_ssm_conv, ssm=state_ssm,
                  rwkv_shift=state_rwkv_shift, rwkv_wkv=state_rwkv_wkv)
    groups = [_Group(x_prompt, None, 0, cfg_p), _Group(x_sample, states, PAST_LEN, cfg_s)]
    (y_p, lc_p, lh_p, sc_p, ss_p, rs_p, rw_p), (y_s, lc_s, lh_s, sc_s, ss_s, rs_s, rw_s) = _run_trunk(groups, p, 512)
    return (y_p, y_s, lc_p, lc_s, lh_p, lh_s, sc_p, sc_s, ss_p, ss_s, rs_p, rs_s, rw_p, rw_s)
```

```python
import functools
import math

import jax
import jax.numpy as jnp
from jax import lax
from jax.experimental import pallas as pl
from jax.experimental.pallas import tpu as pltpu

F32 = jnp.float32
BF16 = jnp.bfloat16

NORM_EPS = 1e-6
CONV_W = 4
LRU_BLOCKS = 8
LRU_C = 8.0
SSM_HEADDIM = 64
SSM_GROUPS = 8
SSM_STATE = 128
SSM_CHUNK = 128
SSM_NORM_EPS = 1e-5
RWKV_HEAD = 64
GN_EPS = 64e-5
PAST_LEN = 16384

V7X_LANES = 128
V7X_SUBLANES = 8
V7X_VMEM_BYTES = 64 * 1024 * 1024
VMEM_LIMIT = V7X_VMEM_BYTES - 8 * 1024 * 1024

RWKV_CHUNK = 64
CONV_PAD = V7X_SUBLANES


def _bdot(a, b):
    return jnp.dot(a.astype(BF16), b.astype(BF16), preferred_element_type=F32)


def _bdot_nt(a, b):
    return lax.dot_general(a.astype(BF16), b.astype(BF16), (((1,), (1,)), ((), ())),
                           preferred_element_type=F32)


def _bdot_tn(a, b):
    return lax.dot_general(a.astype(BF16), b.astype(BF16), (((0,), (0,)), ((), ())),
                           preferred_element_type=F32)


def _rms(x, g):
    return x * lax.rsqrt(jnp.mean(x * x, axis=-1, keepdims=True) + NORM_EPS) * g


def _sigmoid(x):
    return 0.5 * jnp.tanh(0.5 * x) + 0.5


def _softplus(x):
    return jnp.maximum(x, 0.0) + jnp.log1p(jnp.exp(-jnp.abs(x)))


def _silu(x):
    return x * _sigmoid(x)


def _gelu_tanh(x):
    c = math.sqrt(2.0 / math.pi)
    return 0.5 * x * (1.0 + jnp.tanh(c * (x + 0.044715 * (x * x * x))))


def _row_index(rows, period):
    return lax.broadcasted_iota(jnp.int32, (rows, 1), 0) & (period - 1)


def _cumsum_rows(x, t_in, period):
    d = 1
    while d < period:
        x = x + jnp.where(t_in >= d, pltpu.roll(x, d, axis=0), 0.0)
        d *= 2
    return x


def _head_sum(x, head):
    rows, width = x.shape
    lo = lax.broadcasted_iota(jnp.int32, (1, V7X_LANES), 1) < head
    parts = []
    for j in range(width // V7X_LANES):
        seg = x[:, j * V7X_LANES:(j + 1) * V7X_LANES]
        s_lo = jnp.sum(jnp.where(lo, seg, 0.0), axis=-1, keepdims=True)
        s_hi = jnp.sum(jnp.where(lo, 0.0, seg), axis=-1, keepdims=True)
        parts.append(jnp.where(lo, s_lo, s_hi))
    return jnp.concatenate(parts, axis=1)


def _const_spec(shape):
    nd = len(shape)
    return pl.BlockSpec(shape, lambda *_: (0,) * nd, pipeline_mode=pl.Buffered(1))


def _params(sem):
    return pltpu.CompilerParams(dimension_semantics=sem, vmem_limit_bytes=VMEM_LIMIT)


def _proj_kernel(x_ref, g_ref, *refs, n_out, col_chunk):
    w_refs, o_refs = refs[:n_out], refs[n_out:]
    u = _rms(x_ref[...], g_ref[...]).astype(BF16)
    for w_ref, o_ref in zip(w_refs, o_refs):
        n = w_ref.shape[1]
        step = min(col_chunk, n)
        for c0 in range(0, n, step):
            o_ref[:, c0:c0 + step] = jnp.dot(u, w_ref[:, c0:c0 + step], preferred_element_type=F32)


def _norm_proj(x, g, weights, *, tm):
    t, d = x.shape
    assert t % tm == 0
    n_out = len(weights)
    return pl.pallas_call(
        functools.partial(_proj_kernel, n_out=n_out, col_chunk=1024),
        out_shape=[jax.ShapeDtypeStruct((t, w.shape[1]), F32) for w in weights],
        grid=(t // tm,),
        in_specs=[pl.BlockSpec((tm, d), lambda i: (i, 0)), _const_spec((1, d))]
        + [_const_spec(w.shape) for w in weights],
        out_specs=[pl.BlockSpec((tm, w.shape[1]), lambda i: (i, 0)) for w in weights],
        compiler_params=_params(("parallel",)),
        name="norm_proj",
    )(x, g, *weights)


def _ffn_kernel(xa_ref, xb_ref, g_ref, w1_ref, w2_ref, gf_ref, oa_ref, ob_ref, *, n_a, f_chunk, final_norm):
    def run(x_ref, o_ref):
        x = x_ref[...]
        u = _rms(x, g_ref[...]).astype(BF16)
        acc = x
        for c0 in range(0, w1_ref.shape[1], f_chunk):
            h = jnp.dot(u, w1_ref[:, c0:c0 + f_chunk].astype(BF16), preferred_element_type=F32)
            h = jnp.square(jnp.maximum(h, 0.0)).astype(BF16)
            acc = acc + jnp.dot(h, w2_ref[c0:c0 + f_chunk, :].astype(BF16), preferred_element_type=F32)
        if final_norm:
            acc = _rms(acc, gf_ref[...])
        o_ref[...] = acc

    i = pl.program_id(0)

    @pl.when(i < n_a)
    def _():
        run(xa_ref, oa_ref)

    @pl.when(i >= n_a)
    def _():
        run(xb_ref, ob_ref)


def _ffn(xa, xb, g, w1, w2, g_final, *, tm, final_norm):
    d = xa.shape[1]
    f = w1.shape[1]
    assert xa.shape[0] % tm == 0 and xb.shape[0] % tm == 0
    n_a, n_b = xa.shape[0] // tm, xb.shape[0] // tm
    a_map = lambda i: (jnp.minimum(i, n_a - 1), 0)
    b_map = lambda i: (jnp.maximum(i - n_a, 0), 0)
    return pl.pallas_call(
        functools.partial(_ffn_kernel, n_a=n_a, f_chunk=1024, final_norm=final_norm),
        out_shape=[jax.ShapeDtypeStruct(xa.shape, F32), jax.ShapeDtypeStruct(xb.shape, F32)],
        grid=(n_a + n_b,),
        in_specs=[pl.BlockSpec((tm, d), a_map), pl.BlockSpec((tm, d), b_map), _const_spec((1, d)),
                  _const_spec((d, f)), _const_spec((f, d)), _const_spec((1, d))],
        out_specs=[pl.BlockSpec((tm, d), a_map), pl.BlockSpec((tm, d), b_map)],
        compiler_params=_params(("arbitrary",)),
        name="ffn",
    )(xa, xb, g, w1, w2, g_final)


def _conv_taps(xe_ref, cw, cb, tl):
    nd = len(xe_ref.shape)

    def win(off):
        if nd == 3:
            return xe_ref[:, off:off + tl, :]
        return xe_ref[off:off + tl, :]

    base = CONV_PAD - (CONV_W - 1)
    y = cb + win(base) * cw[0:1, :]
    for k in range(1, CONV_W):
        y = y + win(base + k) * cw[k:k + 1, :]
    return y


def _lru_kernel(x_ref, g_ref, win_ref, cs_ref, h0_ref, cw_ref, cb_ref, wr_ref, br_ref, wi_ref, bi_ref, lam_ref, wo_ref,
                o_ref, cso_ref, ho_ref, xe_ref, hs_ref, hc_ref, *, nb, tl, start_pos):
    c = pl.program_id(1)
    rows = nb * tl
    d = x_ref.shape[2]
    blk = d // LRU_BLOCKS
    ncol = d // V7X_LANES
    pitch = CONV_PAD + tl
    cols = [slice(j * V7X_LANES, (j + 1) * V7X_LANES) for j in range(ncol)]

    @pl.when(c == 0)
    def _():
        for j in range(ncol):
            for s in range(nb):
                xe_ref[j, s * pitch:s * pitch + CONV_PAD, :] = cs_ref[s, :, cols[j]]
        hc_ref[...] = h0_ref[...]

    x = x_ref[...].reshape(rows, d)
    u = _rms(x, g_ref[...]).astype(BF16)
    xb = jnp.dot(u, win_ref[:, 0:d], preferred_element_type=F32)
    for j in range(ncol):
        for s in range(nb):
            xe_ref[j, s * pitch + CONV_PAD:(s + 1) * pitch, :] = xb[s * tl:(s + 1) * tl, cols[j]]

    cw = cw_ref[...]
    cb = cb_ref[...]
    base = CONV_PAD - (CONV_W - 1)
    slabs = []
    for t in range(tl):
        parts = []
        for j in range(ncol):
            acc = cb[:, cols[j]] + xe_ref[j, pl.ds(base + t, nb, stride=pitch), :] * cw[0:1, cols[j]]
            for k in range(1, CONV_W):
                acc = acc + xe_ref[j, pl.ds(base + t + k, nb, stride=pitch), :] * cw[k:k + 1, cols[j]]
            parts.append(acc)
        slabs.append(jnp.concatenate(parts, axis=1))
    xc = jnp.concatenate(slabs, axis=0)
    for j in range(ncol):
        for s in range(nb):
            tail = xe_ref[j, s * pitch + tl:(s + 1) * pitch, :]
            xe_ref[j, s * pitch:s * pitch + CONV_PAD, :] = tail
            cso_ref[s, :, cols[j]] = tail

    xcb = xc.astype(BF16)
    r_pre = jnp.concatenate(
        [jnp.dot(xcb[:, j * blk:(j + 1) * blk], wr_ref[j], preferred_element_type=F32) for j in range(LRU_BLOCKS)],
        axis=1) + br_ref[...]
    i_pre = jnp.concatenate(
        [jnp.dot(xcb[:, j * blk:(j + 1) * blk], wi_ref[j], preferred_element_type=F32) for j in range(LRU_BLOCKS)],
        axis=1) + bi_ref[...]
    r = _sigmoid(r_pre)
    gi = _sigmoid(i_pre)
    log_a = (-LRU_C) * r * _softplus(-lam_ref[...])
    a = jnp.exp(log_a)
    m2 = -jnp.tanh(log_a) * (a * a + 1.0)
    mult = jnp.where(m2 > 0.0, m2 * lax.rsqrt(m2), 0.0)
    if start_pos == 0:
        reset = (lax.broadcasted_iota(jnp.int32, (rows, 1), 0) < nb) & (c == 0)
        a = jnp.where(reset, 0.0, a)
        mult = jnp.where(reset, 1.0, mult)
    b = mult * (gi * xc)

    h = hc_ref[...]
    for t in range(tl):
        h = a[t * nb:(t + 1) * nb] * h + b[t * nb:(t + 1) * nb]
        for j in range(ncol):
            hs_ref[j, pl.ds(t, nb, stride=pitch), :] = h[:, cols[j]]
    hc_ref[...] = h
    ho_ref[...] = h
    h_bm = jnp.concatenate(
        [jnp.concatenate([hs_ref[j, s * pitch:s * pitch + tl, :] for s in range(nb)], axis=0) for j in range(ncol)],
        axis=1)

    gate = jnp.dot(u, win_ref[:, d:2 * d], preferred_element_type=F32)
    y = (h_bm * _gelu_tanh(gate)).astype(BF16)
    o_ref[...] = (x + jnp.dot(y, wo_ref[...], preferred_element_type=F32)).reshape(nb, tl, d)


def _lru_mixer(x3, g, w_in, conv_state, h0, cw, cb, wr, br, wi, bi, lam, wo, *, nb, tl, start_pos):
    batch, seq, d = x3.shape
    nchunk = seq // tl
    assert batch % nb == 0 and seq % tl == 0 and nb % V7X_SUBLANES == 0 and tl % V7X_SUBLANES == 0
    ncol = d // V7X_LANES
    blk_map = lambda i, c: (i, c, 0)
    seq_map = lambda i, c: (i, 0, 0)
    row_map = lambda i, c: (i, 0)
    return pl.pallas_call(
        functools.partial(_lru_kernel, nb=nb, tl=tl, start_pos=start_pos),
        out_shape=[jax.ShapeDtypeStruct((batch, seq, d), F32),
                   jax.ShapeDtypeStruct((batch, CONV_PAD, d), F32),
                   jax.ShapeDtypeStruct((batch, d), F32)],
        grid=(batch // nb, nchunk),
        in_specs=[pl.BlockSpec((nb, tl, d), blk_map), _const_spec(g.shape), _const_spec(w_in.shape),
                  pl.BlockSpec((nb, CONV_PAD, d), seq_map), pl.BlockSpec((nb, d), row_map),
                  _const_spec(cw.shape), _const_spec(cb.shape), _const_spec(wr.shape), _const_spec(br.shape),
                  _const_spec(wi.shape), _const_spec(bi.shape), _const_spec(lam.shape), _const_spec(wo.shape)],
        out_specs=[pl.BlockSpec((nb, tl, d), blk_map), pl.BlockSpec((nb, CONV_PAD, d), seq_map),
                   pl.BlockSpec((nb, d), row_map)],
        scratch_shapes=[pltpu.VMEM((ncol, nb * (CONV_PAD + tl), V7X_LANES), F32),
                        pltpu.VMEM((ncol, nb * (CONV_PAD + tl), V7X_LANES), F32),
                        pltpu.VMEM((nb, d), F32)],
        compiler_params=_params(("parallel", "arbitrary")),
        name="lru_mixer",
    )(x3, g, w_in, conv_state, h0, cw, cb, wr, br, wi, bi, lam, wo)


def _split_bf16(x, terms):
    parts = []
    for _ in range(terms - 1):
        hi = x.astype(BF16).astype(F32)
        parts.append(hi)
        x = x - hi
    parts.append(x)
    return parts


def _ssd_kernel(x_ref, g_ref, pz_ref, pxbc_ref, pdt_ref, cs_ref, s0_ref, cw_ref, cb_ref, dtb_ref, alog_ref, dskx_ref, nw_ref,
                ex_ref, wo_ref, o_ref, cso_ref, so_ref, xe_ref, st_ref, *, q, zero_state, fused_proj):
    c = pl.program_id(1)
    d_ssm = pz_ref.shape[1]
    n = SSM_STATE
    hpg = d_ssm // SSM_HEADDIM // SSM_GROUPS
    gw = hpg * SSM_HEADDIM
    nblk = d_ssm // n

    @pl.when(c == 0)
    def _():
        xe_ref[0:CONV_PAD, :] = cs_ref[0]
        if zero_state:
            st_ref[...] = jnp.zeros(st_ref.shape, F32)
        else:
            for j in range(nblk):
                st_ref[:, j * n:(j + 1) * n] = s0_ref[0, j * n:(j + 1) * n, :].T

    x = x_ref[...]
    if fused_proj:
        u = _rms(x, g_ref[...]).astype(BF16)
        for c0 in range(0, pxbc_ref.shape[1], 1024):
            xe_ref[CONV_PAD:CONV_PAD + q, c0:c0 + 1024] = jnp.dot(u, pxbc_ref[:, c0:c0 + 1024],
                                                                  preferred_element_type=F32)
        dt_pre = jnp.dot(u, pdt_ref[...], preferred_element_type=F32)
    else:
        xe_ref[CONV_PAD:CONV_PAD + q, :] = pxbc_ref[...]
        dt_pre = pdt_ref[...]
    xbc = _silu(_conv_taps(xe_ref, cw_ref[...], cb_ref[...], q))
    tail = xe_ref[q:q + CONV_PAD, :]
    xe_ref[0:CONV_PAD, :] = tail
    cso_ref[0] = tail

    xs = xbc[:, 0:d_ssm]
    bm = xbc[:, d_ssm:d_ssm + SSM_GROUPS * n]
    cm = xbc[:, d_ssm + SSM_GROUPS * n:]
    dt = _softplus(dt_pre + dtb_ref[...])
    da = dt * (-jnp.exp(alog_ref[...]))
    t_in = _row_index(q, q)
    acs = _cumsum_rows(da, t_in, q)
    if q < V7X_LANES:
        acs_sq = jnp.concatenate([acs, jnp.zeros((V7X_LANES - q, V7X_LANES), F32)], axis=0)
    else:
        acs_sq = acs
    acs_t = acs_sq.T

    lhs = jnp.concatenate(_split_bf16(dt, 2) + _split_bf16(acs, 3), axis=0).astype(BF16)
    ex = jnp.dot(lhs, ex_ref[...], preferred_element_type=F32)
    dt_x = ex[0:q] + ex[q:2 * q]
    acs_x = ex[2 * q:3 * q] + ex[3 * q:4 * q] + ex[4 * q:5 * q]
    last_x = acs_x[q - 1:q, :]
    xdt = (xs * dt_x).astype(BF16)
    xdtd = (xs * (dt_x * jnp.exp(last_x - acs_x))).astype(BF16)

    tri = lax.broadcasted_iota(jnp.int32, (q, q), 0) >= lax.broadcasted_iota(jnp.int32, (q, q), 1)
    lane_head = lax.broadcasted_iota(jnp.int32, (1, gw), 1) // SSM_HEADDIM
    groups = range(SSM_GROUPS)
    bgs = [bm[:, g * n:(g + 1) * n].astype(BF16) for g in groups]
    cgs = [cm[:, g * n:(g + 1) * n].astype(BF16) for g in groups]
    cb_mats = [_bdot_nt(cgs[g], bgs[g]) for g in groups]
    y_off = [_bdot(cgs[g], st_ref[:, g * gw:(g + 1) * gw]) for g in groups]
    y_diag = []
    for g in groups:
        xdt_g = xdt[:, g * gw:(g + 1) * gw]
        acc = None
        for e in range(hpg):
            h = g * hpg + e
            seg = acs[:, h:h + 1] - acs_t[h:h + 1, 0:q]
            l_mat = jnp.exp(jnp.where(tri, seg, -jnp.inf))
            part = jnp.dot((cb_mats[g] * l_mat).astype(BF16), jnp.where(lane_head == e, xdt_g, jnp.zeros_like(xdt_g)),
                           preferred_element_type=F32)
            acc = part if acc is None else acc + part
        y_diag.append(acc)
    st_new = [_bdot_tn(bgs[g], xdtd[:, g * gw:(g + 1) * gw]) for g in groups]
    e_last_x = jnp.exp(last_x)
    for g in groups:
        cols = slice(g * gw, (g + 1) * gw)
        st_ref[:, cols] = st_ref[:, cols] * e_last_x[:, cols] + st_new[g]

    @pl.when(c == pl.num_programs(1) - 1)
    def _():
        for j in range(nblk):
            so_ref[0, j * n:(j + 1) * n, :] = st_ref[:, j * n:(j + 1) * n].T

    y = jnp.concatenate(y_diag, axis=1) + jnp.concatenate(y_off, axis=1) * jnp.exp(acs_x) + dskx_ref[...] * xs
    y = y * _silu(jnp.dot(u, pz_ref[...], preferred_element_type=F32) if fused_proj else pz_ref[...])
    norm_parts = []
    for g in groups:
        yg = y[:, g * gw:(g + 1) * gw]
        norm_parts.append(yg * lax.rsqrt(jnp.mean(yg * yg, axis=-1, keepdims=True) + SSM_NORM_EPS))
    yn = (jnp.concatenate(norm_parts, axis=1) * nw_ref[...]).astype(BF16)
    o_ref[...] = x + jnp.dot(yn, wo_ref[...], preferred_element_type=F32)


def _ssd_mixer(x, g, w_z, w_xbc, w_dt, conv_state, s0, cw, cb, dtb, alog, dskx, nw, expand, wo, *, batch, seq, q,
               fused_proj):
    d = x.shape[1]
    d_ssm = w_z.shape[1]
    cdim = w_xbc.shape[1]
    nchunk = seq // q
    assert seq % q == 0
    zero_state = s0 is None
    if zero_state:
        s0 = jnp.zeros((1, d_ssm, SSM_STATE), F32)
    row_map = lambda b, c: (b * nchunk + c, 0)
    seq_map = lambda b, c: (b, 0, 0)
    s0_map = (lambda b, c: (0, 0, 0)) if zero_state else seq_map
    if fused_proj:
        proj = (w_z, w_xbc, w_dt)
        proj_specs = [_const_spec(w.shape) for w in proj]
    else:
        proj = _norm_proj(x, g, [w_z, w_xbc, w_dt], tm=min(256, batch * seq))
        proj_specs = [pl.BlockSpec((q, a.shape[1]), row_map) for a in proj]
    return pl.pallas_call(
        functools.partial(_ssd_kernel, q=q, zero_state=zero_state, fused_proj=fused_proj),
        out_shape=[jax.ShapeDtypeStruct((batch * seq, d), F32),
                   jax.ShapeDtypeStruct((batch, CONV_PAD, cdim), F32),
                   jax.ShapeDtypeStruct((batch, d_ssm, SSM_STATE), F32)],
        grid=(batch, nchunk),
        in_specs=[pl.BlockSpec((q, d), row_map), _const_spec(g.shape)] + proj_specs
        + [pl.BlockSpec((1, CONV_PAD, cdim), seq_map), pl.BlockSpec((1, d_ssm, SSM_STATE), s0_map),
                  _const_spec(cw.shape), _const_spec(cb.shape), _const_spec(dtb.shape), _const_spec(alog.shape),
                  _const_spec(dskx.shape), _const_spec(nw.shape), _const_spec(expand.shape), _const_spec(wo.shape)],
        out_specs=[pl.BlockSpec((q, d), row_map), pl.BlockSpec((1, CONV_PAD, cdim), seq_map),
                   pl.BlockSpec((1, d_ssm, SSM_STATE), seq_map)],
        scratch_shapes=[pltpu.VMEM((CONV_PAD + q, cdim), F32), pltpu.VMEM((SSM_STATE, d_ssm), F32)],
        compiler_params=_params(("parallel", "arbitrary")),
        name="ssd_mixer",
    )(x, g, *proj, conv_state, s0, cw, cb, dtb, alog, dskx, nw, expand, wo)


def _rwkv_proj_kernel(x_ref, sh_ref, g_ref, mu_ref, wrkv_ref, w0_ref, w1_ref, w2_ref, a0_ref, a1_ref, a2_ref,
                      g1_ref, g2_ref, kkw_ref, kaw_ref,
                      r_o, k_o, v_o, kk_o, bb_o, lw_o, g_o, sho_ref, carry_ref, *, nb, tl, time_major):
    c = pl.program_id(1)
    rows = x_ref.shape[0]
    d = x_ref.shape[1]

    @pl.when(c == 0)
    def _():
        carry_ref[...] = sh_ref[...]

    u = _rms(x_ref[...], g_ref[...])
    if time_major:
        prev = carry_ref[...]
        last = u
    else:
        t_in = _row_index(rows, tl)
        prev = jnp.where(t_in == 0, jnp.broadcast_to(carry_ref[...], (nb, tl, d)).reshape(rows, d),
                         pltpu.roll(u, 1, axis=0))
        last = u.reshape(nb, tl, d)[:, tl - 1:tl, :]
    carry_ref[...] = last
    sho_ref[...] = last

    diff = prev - u
    mu = mu_ref[...]

    def mixed(s):
        return (u + diff * mu[s:s + 1, :]).astype(BF16)

    r = jnp.dot(mixed(0), wrkv_ref[0], preferred_element_type=F32)
    k = jnp.dot(mixed(1), wrkv_ref[1], preferred_element_type=F32)
    v_o[...] = jnp.dot(mixed(2), wrkv_ref[2], preferred_element_type=F32)
    w_pre = w0_ref[...] + _bdot(jnp.tanh(jnp.dot(mixed(3), w1_ref[...], preferred_element_type=F32)), w2_ref[...])
    sp = jnp.maximum(-w_pre, 0.0) + jnp.log(1.0 + jnp.exp(-jnp.abs(w_pre)))
    lw_o[...] = -jnp.exp(-sp - 0.5)
    a = _sigmoid(a0_ref[...] + _bdot(jnp.dot(mixed(4), a1_ref[...], preferred_element_type=F32), a2_ref[...]))
    g_o[...] = _bdot(_sigmoid(jnp.dot(mixed(5), g1_ref[...], preferred_element_type=F32)), g2_ref[...])
    kk = k * kkw_ref[...]
    kk = kk * jnp.minimum(lax.rsqrt(_head_sum(kk * kk, RWKV_HEAD)), 1e12)
    r_o[...] = r
    k_o[...] = k * (1.0 + (a - 1.0) * kaw_ref[...])
    kk_o[...] = kk
    bb_o[...] = kk * a


def _rwkv_proj(x, shift0, g, mu, wrkv, w0, w1, w2, a0, a1, a2, g1, g2, kkw, kaw, *, batch, seq, nb, tl, time_major):
    d = x.shape[1]
    consts = (g, mu, wrkv, w0, w1, w2, a0, a1, a2, g1, g2, kkw, kaw)
    if time_major:
        rows, grid = batch, (1, seq)
        row_map = lambda i, c: (c, 0)
        sh_shape, sh_block, sh_map = (batch, d), (batch, d), (lambda i, c: (0, 0))
    else:
        nchunk = seq // tl
        assert batch % nb == 0 and seq % tl == 0 and (nb == 1 or nchunk == 1)
        rows, grid = nb * tl, (batch // nb, nchunk)
        row_map = lambda i, c: (i * nchunk + c, 0)
        sh_shape, sh_block, sh_map = (batch, 1, d), (nb, 1, d), (lambda i, c: (i, 0, 0))
    return pl.pallas_call(
        functools.partial(_rwkv_proj_kernel, nb=nb, tl=tl, time_major=time_major),
        out_shape=[jax.ShapeDtypeStruct((batch * seq, d), F32)] * 7 + [jax.ShapeDtypeStruct(sh_shape, F32)],
        grid=grid,
        in_specs=[pl.BlockSpec((rows, d), row_map), pl.BlockSpec(sh_block, sh_map)]
        + [_const_spec(w.shape) for w in consts],
        out_specs=[pl.BlockSpec((rows, d), row_map)] * 7 + [pl.BlockSpec(sh_block, sh_map)],
        scratch_shapes=[pltpu.VMEM(sh_block, F32)],
        compiler_params=_params(("parallel", "arbitrary")),
        name="rwkv_proj",
    )(x, shift0, *consts)


def _rwkv_chunk_kernel(r_ref, k_ref, v_ref, kk_ref, bb_ref, lw_ref, g_ref, x_ref, s0_ref, rk_ref, lnw_ref, lnb_ref,
                       wo_ref, o_ref, so_ref, s_ref, og_ref, *, nb, tl, chunk, zero_state):
    c = pl.program_id(1)
    d = x_ref.shape[1]
    hd = RWKV_HEAD
    heads = d // hd
    cpb = tl // chunk
    n_iter = nb * cpb

    @pl.when(c == 0)
    def _():
        if zero_state:
            s_ref[...] = jnp.zeros(s_ref.shape, F32)
        else:
            s_ref[...] = s0_ref[...]

    t_in = _row_index(chunk, chunk)
    ri = lax.broadcasted_iota(jnp.int32, (2 * chunk, 2 * chunk), 0)
    ci = lax.broadcasted_iota(jnp.int32, (2 * chunk, 2 * chunk), 1) & (chunk - 1)
    blk_mask = jnp.where(ri < chunk, ri - 1, ri - chunk) >= ci

    def body(it, carry):
        seq_i = it if cpb == 1 else 0
        rows = pl.ds(pl.multiple_of(it * chunk, chunk), chunk)
        r = r_ref[rows, :]
        k = k_ref[rows, :]
        v = v_ref[rows, :]
        kk = kk_ref[rows, :]
        bb = bb_ref[rows, :]
        lw = lw_ref[rows, :]
        cum = _cumsum_rows(lw, t_in, chunk)
        c_last = cum[chunk - 1:chunk, :]
        g_last = jnp.exp(c_last)
        e_inv = jnp.exp(-cum)
        e_end = jnp.exp(c_last - cum)
        p_st = jnp.concatenate([kk * jnp.exp(cum - lw), r * jnp.exp(cum)], axis=0).astype(BF16)
        q_st = jnp.concatenate([bb * e_inv, k * e_inv], axis=0).astype(BF16)
        e_st = jnp.concatenate([k * e_end, bb * e_end], axis=0).astype(BF16)
        zv_st = jnp.concatenate([jnp.zeros_like(v), v], axis=0).astype(BF16)
        hs = range(heads)
        sls = [slice(h * hd, (h + 1) * hd) for h in hs]
        s_l = [s_ref[seq_i, sl, :] for sl in sls]
        g_m = [jnp.where(blk_mask, _bdot_nt(p_st[:, sl], q_st[:, sl]), 0.0) for sl in sls]
        ps = [_bdot_nt(p_st[:, sls[h]], s_l[h]) for h in hs]
        t1 = [_bdot(g_m[h], zv_st[:, sls[h]]) for h in hs]
        nmat = [-g_m[h][0:chunk, 0:chunk] for h in hs]
        w_l = [ps[h][0:chunk] + t1[h][0:chunk] for h in hs]
        u_l = [w_l[h] + _bdot(nmat[h], w_l[h]) for h in hs]
        m = 2
        while m < chunk:
            nmat = [_bdot(x, x) for x in nmat]
            u_l = [u_l[h] + _bdot(nmat[h], u_l[h]) for h in hs]
            m *= 2
        o_parts = [ps[h][chunk:] + t1[h][chunk:] - _bdot(g_m[h][chunk:, 0:chunk], u_l[h]) for h in hs]
        for h in hs:
            vu = jnp.concatenate([v[:, sls[h]], -u_l[h]], axis=0)
            s_ref[seq_i, sls[h], :] = s_l[h] * g_last[:, sls[h]] + _bdot_tn(vu, e_st[:, sls[h]])
        o = jnp.concatenate(o_parts, axis=1)
        mean = _head_sum(o, hd) * (1.0 / hd)
        oc = o - mean
        var = _head_sum(oc * oc, hd) * (1.0 / hd)
        on = oc * lax.rsqrt(var + GN_EPS) * lnw_ref[...] + lnb_ref[...]
        bonus = _head_sum(r * k * rk_ref[...], hd) * v
        og_ref[rows, :] = (on + bonus) * g_ref[rows, :]
        return carry

    lax.fori_loop(0, n_iter, body, 0)

    @pl.when(c == pl.num_programs(1) - 1)
    def _():
        so_ref[...] = s_ref[...]

    o_ref[...] = x_ref[...] + _bdot(og_ref[...], wo_ref[...])


def _rwkv_mixer(x, r, k, v, kk, bb, lw, g, s0, rk, lnw, lnb, wo, *, batch, seq, nb, tl, chunk):
    d = x.shape[1]
    nchunk = seq // tl
    assert batch % nb == 0 and seq % tl == 0 and tl % chunk == 0 and (nb == 1 or tl == chunk)
    rows = nb * tl
    zero_state = s0 is None
    if zero_state:
        s0 = jnp.zeros((nb, d, RWKV_HEAD), F32)
    row_map = lambda i, c: (i * nchunk + c, 0)
    seq_map = lambda i, c: (i, 0, 0)
    s0_map = (lambda i, c: (0, 0, 0)) if zero_state else seq_map
    row_spec = pl.BlockSpec((rows, d), row_map)
    return pl.pallas_call(
        functools.partial(_rwkv_chunk_kernel, nb=nb, tl=tl, chunk=chunk, zero_state=zero_state),
        out_shape=[jax.ShapeDtypeStruct((batch * seq, d), F32), jax.ShapeDtypeStruct((batch, d, RWKV_HEAD), F32)],
        grid=(batch // nb, nchunk),
        in_specs=[row_spec] * 8 + [pl.BlockSpec((nb, d, RWKV_HEAD), s0_map),
                                   _const_spec(rk.shape), _const_spec(lnw.shape), _const_spec(lnb.shape),
                                   _const_spec(wo.shape)],
        out_specs=[row_spec, pl.BlockSpec((nb, d, RWKV_HEAD), seq_map)],
        scratch_shapes=[pltpu.VMEM((nb, d, RWKV_HEAD), F32), pltpu.VMEM((rows, d), F32)],
        compiler_params=_params(("parallel", "arbitrary")),
        name="rwkv_chunk",
    )(r, k, v, kk, bb, lw, g, x, s0, rk, lnw, lnb, wo)


def _rwkv_seq_kernel(r_ref, k_ref, v_ref, kk_ref, bb_ref, lw_ref, s0_ref, o_ref, so_ref, xt_ref, ot_ref, *, seq, batch):
    hd = RWKV_HEAD
    for i, ref in enumerate((r_ref, k_ref, v_ref, kk_ref, bb_ref, lw_ref)):
        for t in range(seq):
            tile = ref[t * batch:(t + 1) * batch, :]
            if ref is lw_ref:
                tile = jnp.exp(tile)
            xt_ref[i * seq + t] = tile.T
    sub = V7X_SUBLANES
    for hh in range(2):
        base = hh * hd
        for t in range(seq):
            src = s0_ref if t == 0 else so_ref
            r_t, k_t, kk_t, bb_t, w_t = (xt_ref[i * seq + t, base:base + hd, :] for i in (0, 1, 3, 4, 5))

            def body(vg, carry, src=src, t=t, base=base, hh=hh, r_t=r_t, k_t=k_t, kk_t=kk_t, bb_t=bb_t, w_t=w_t):
                v0 = pl.multiple_of(base + vg * sub, sub)
                v_tile = xt_ref[2 * seq + t, pl.ds(v0, sub), :]
                rows = []
                for i in range(sub):
                    off = pl.multiple_of((vg * sub + i) * hd, hd)
                    sv = src[hh, pl.ds(off, hd), :]
                    sa = jnp.sum(sv * kk_t, axis=0, keepdims=True)
                    sv = sv * w_t - sa * bb_t + v_tile[i:i + 1, :] * k_t
                    so_ref[hh, pl.ds(off, hd), :] = sv
                    rows.append(jnp.sum(sv * r_t, axis=0, keepdims=True))
                ot_ref[t, pl.ds(v0, sub), :] = jnp.concatenate(rows, axis=0)
                return carry

            lax.fori_loop(0, hd // sub, body, 0)
    for t in range(seq):
        o_ref[t * batch:(t + 1) * batch, :] = ot_ref[t].T


def _rwkv_seq(r, k, v, kk, bb, lw, s0, *, batch, seq):
    d = r.shape[1]
    assert batch == V7X_LANES and d % V7X_LANES == 0
    npair = d // V7X_LANES
    col_spec = pl.BlockSpec((seq * batch, V7X_LANES), lambda j: (0, j))
    st_spec = pl.BlockSpec((2, RWKV_HEAD * RWKV_HEAD, batch), lambda j: (j, 0, 0))
    return pl.pallas_call(
        functools.partial(_rwkv_seq_kernel, seq=seq, batch=batch),
        out_shape=[jax.ShapeDtypeStruct((seq * batch, d), F32), jax.ShapeDtypeStruct(s0.shape, F32)],
        grid=(npair,),
        in_specs=[col_spec] * 6 + [st_spec],
        out_specs=[col_spec, st_spec],
        scratch_shapes=[pltpu.VMEM((6 * seq, V7X_LANES, batch), F32), pltpu.VMEM((seq, V7X_LANES, batch), F32)],
        compiler_params=_params(("parallel",)),
        name="rwkv_seq",
    )(r, k, v, kk, bb, lw, s0)


def _rwkv_finish_kernel(o_ref, r_ref, k_ref, v_ref, g_ref, x_ref, rk_ref, lnw_ref, lnb_ref, wo_ref, out_ref):
    hd = RWKV_HEAD
    o = o_ref[...]
    mean = _head_sum(o, hd) * (1.0 / hd)
    oc = o - mean
    var = _head_sum(oc * oc, hd) * (1.0 / hd)
    on = oc * lax.rsqrt(var + GN_EPS) * lnw_ref[...] + lnb_ref[...]
    bonus = _head_sum(r_ref[...] * k_ref[...] * rk_ref[...], hd) * v_ref[...]
    out_ref[...] = x_ref[...] + _bdot((on + bonus) * g_ref[...], wo_ref[...])


def _rwkv_finish(o, r, k, v, g, x, rk, lnw, lnb, wo, *, tm):
    t, d = x.shape
    assert t % tm == 0
    row_spec = pl.BlockSpec((tm, d), lambda i: (i, 0))
    return pl.pallas_call(
        _rwkv_finish_kernel,
        out_shape=jax.ShapeDtypeStruct((t, d), F32),
        grid=(t // tm,),
        in_specs=[row_spec] * 6 + [_const_spec(rk.shape), _const_spec(lnw.shape), _const_spec(lnb.shape),
                                   _const_spec(wo.shape)],
        out_specs=row_spec,
        compiler_params=_params(("parallel",)),
        name="rwkv_finish",
    )(o, r, k, v, g, x, rk, lnw, lnb, wo)


def _pad_conv_state(state):
    return jnp.pad(state, ((0, 0), (CONV_PAD - (CONV_W - 1), 0), (0, 0)))


class _Group:
    def __init__(self, x3, states, start_pos, cfg):
        self.batch, self.seq, d = x3.shape
        self.x = x3.reshape(self.batch * self.seq, d)
        self.states, self.start_pos, self.cfg = states, start_pos, cfg
        self.out = dict(lru_conv=[], lru_h=[], ssm_conv=[], ssm=[], rwkv_shift=[], rwkv_wkv=[])

    def state(self, name, idx, zero_shape):
        return self.states[name][idx] if self.states else jnp.zeros(zero_shape, F32)


def _mixer_layer(grp, layer, idx, p):
    batch, seq, cfg = grp.batch, grp.seq, grp.cfg
    x = grp.x
    d = x.shape[1]
    g_mix = p["norm_mix"][layer][None, :]
    kind = layer % 3
    if kind == 0:
        cs = grp.state("lru_conv", idx, (batch, CONV_W - 1, d))
        h0 = grp.state("lru_h", idx, (batch, d))
        x3, cso, ho = _lru_mixer(
            x.reshape(batch, seq, d), g_mix, p["lru_w_in"][idx], _pad_conv_state(cs), h0,
            p["lru_conv_w"][idx], p["lru_conv_b"][idx][None, :], p["lru_w_r"][idx], p["lru_b_r"][idx][None, :],
            p["lru_w_i"][idx], p["lru_b_i"][idx][None, :], p["lru_lambda"][idx][None, :], p["lru_w_out"][idx],
            nb=cfg["lru_nb"], tl=cfg["lru_tl"], start_pos=grp.start_pos)
        grp.x = x3.reshape(batch * seq, d)
        grp.out["lru_conv"].append(cso[:, CONV_PAD - (CONV_W - 1):, :])
        grp.out["lru_h"].append(ho)
    elif kind == 1:
        cdim = p["ssm_w_xbc"].shape[2]
        cs = grp.state("ssm_conv", idx, (batch, CONV_W - 1, cdim))
        s0 = grp.states["ssm"][idx].reshape(batch, -1, SSM_STATE) if grp.states else None
        grp.x, cso, so = _ssd_mixer(
            x, g_mix, p["ssm_w_z"][idx], p["ssm_w_xbc"][idx], p["ssm_w_dt"][idx], _pad_conv_state(cs), s0,
            p["ssm_conv_w"][idx], p["ssm_conv_b"][idx][None, :], p["ssm_dt_bias"][idx], p["ssm_a_log"][idx],
            p["ssm_d_x"][idx], p["ssm_norm_w"][idx][None, :], p["ssm_expand"], p["ssm_w_out"][idx],
            batch=batch, seq=seq, q=cfg["ssd_q"], fused_proj=cfg["ssd_fused"])
        grp.out["ssm_conv"].append(cso[:, CONV_PAD - (CONV_W - 1):, :])
        grp.out["ssm"].append(so.reshape(batch, -1, SSM_HEADDIM, SSM_STATE))
    elif cfg["rwkv_stepwise"]:
        heads = d // RWKV_HEAD
        proj_w = (g_mix, p["rwkv_mu"][idx], p["rwkv_w_rkv"][idx], p["rwkv_w0"][idx][None, :],
                  p["rwkv_w_w1"][idx], p["rwkv_w_w2"][idx], p["rwkv_a0"][idx][None, :], p["rwkv_w_a1"][idx],
                  p["rwkv_w_a2"][idx], p["rwkv_w_g1"][idx], p["rwkv_w_g2"][idx], p["rwkv_k_k"][idx][None, :],
                  p["rwkv_k_a"][idx][None, :])
        x_tm = x.reshape(batch, seq, d).transpose(1, 0, 2).reshape(seq * batch, d)
        r, k, v, kk, bb, lw, g, sho = _rwkv_proj(x_tm, grp.state("rwkv_shift", idx, (batch, d)), *proj_w,
                                                 batch=batch, seq=seq, nb=1, tl=1, time_major=True)
        s0 = grp.state("rwkv_wkv", idx, (batch, heads, RWKV_HEAD, RWKV_HEAD))
        s0 = s0.transpose(1, 2, 3, 0).reshape(heads, RWKV_HEAD * RWKV_HEAD, batch)
        o, so = _rwkv_seq(r, k, v, kk, bb, lw, s0, batch=batch, seq=seq)
        x_tm = _rwkv_finish(o, r, k, v, g, x_tm, p["rwkv_r_k"][idx].reshape(1, d), p["rwkv_lnx_w"][idx][None, :],
                            p["rwkv_lnx_b"][idx][None, :], p["rwkv_w_out"][idx], tm=min(256, seq * batch))
        grp.x = x_tm.reshape(seq, batch, d).transpose(1, 0, 2).reshape(batch * seq, d)
        grp.out["rwkv_shift"].append(sho)
        grp.out["rwkv_wkv"].append(so.reshape(heads, RWKV_HEAD, RWKV_HEAD, batch).transpose(3, 0, 1, 2))
    else:
        sh0 = grp.state("rwkv_shift", idx, (batch, d))
        s0 = grp.states["rwkv_wkv"][idx].reshape(batch, d, RWKV_HEAD) if grp.states else None
        r, k, v, kk, bb, lw, g, sho = _rwkv_proj(
            x, sh0[:, None, :], g_mix, p["rwkv_mu"][idx], p["rwkv_w_rkv"][idx], p["rwkv_w0"][idx][None, :],
            p["rwkv_w_w1"][idx], p["rwkv_w_w2"][idx], p["rwkv_a0"][idx][None, :], p["rwkv_w_a1"][idx],
            p["rwkv_w_a2"][idx], p["rwkv_w_g1"][idx], p["rwkv_w_g2"][idx], p["rwkv_k_k"][idx][None, :],
            p["rwkv_k_a"][idx][None, :], batch=batch, seq=seq, nb=cfg["rwkv_proj_nb"], tl=cfg["rwkv_tl"],
            time_major=False)
        grp.x, so = _rwkv_mixer(
            x, r, k, v, kk, bb, lw, g, s0, p["rwkv_r_k"][idx].reshape(1, d), p["rwkv_lnx_w"][idx][None, :],
            p["rwkv_lnx_b"][idx][None, :], p["rwkv_w_out"][idx],
            batch=batch, seq=seq, nb=cfg["rwkv_nb"], tl=cfg["rwkv_tl"], chunk=cfg["rwkv_chunk"])
        grp.out["rwkv_shift"].append(sho[:, 0, :])
        grp.out["rwkv_wkv"].append(so.reshape(batch, d // RWKV_HEAD, RWKV_HEAD, RWKV_HEAD))


def _run_trunk(groups, p, tm_ffn):
    depth = p["norm_mix"].shape[0]
    counts = [0, 0, 0]
    for layer in range(depth):
        kind = layer % 3
        for grp in groups:
            _mixer_layer(grp, layer, counts[kind], p)
        counts[kind] += 1
        groups[0].x, groups[1].x = _ffn(
            groups[0].x, groups[1].x, p["norm_ffn"][layer][None, :], p["ffn_w1"][layer], p["ffn_w2"][layer],
            p["norm_final"][None, :], tm=tm_ffn, final_norm=(layer == depth - 1))
    stack = lambda parts: parts[0][None] if len(parts) == 1 else jnp.stack(parts)
    results = []
    for grp in groups:
        d = grp.x.shape[1]
        results.append([grp.x.reshape(grp.batch, grp.seq, d)]
                       + [stack(grp.out[name]) for name in ("lru_conv", "lru_h", "ssm_conv", "ssm", "rwkv_shift", "rwkv_wkv")])
    return results


def _prep_params(p):
    q = dict(p)
    for name in ("lru_w_in", "lru_w_r", "lru_w_i", "lru_w_out", "ssm_w_out", "rwkv_w_rkv", "rwkv_w_w1", "rwkv_w_w2",
                 "rwkv_w_a1", "rwkv_w_a2", "rwkv_w_g1", "rwkv_w_g2", "rwkv_w_out"):
        q[name] = p[name].astype(BF16)
    d_ssm = p["ssm_w_out"].shape[1]
    heads = p["ssm_dt_bias"].shape[1]
    w_in = p["ssm_w_in"]
    cdim = w_in.shape[2] - d_ssm - heads
    pad = V7X_LANES - heads
    q["ssm_w_z"] = w_in[:, :, :d_ssm].astype(BF16)
    q["ssm_w_xbc"] = w_in[:, :, d_ssm:d_ssm + cdim].astype(BF16)
    q["ssm_w_dt"] = jnp.pad(w_in[:, :, d_ssm + cdim:], ((0, 0), (0, 0), (0, pad))).astype(BF16)
    for name in ("ssm_dt_bias", "ssm_a_log"):
        q[name] = jnp.pad(p[name], ((0, 0), (0, pad)))[:, None, :]
    q["ssm_d_x"] = jnp.repeat(p["ssm_d"], SSM_HEADDIM, axis=1)[:, None, :]
    q["ssm_expand"] = (jnp.arange(V7X_LANES)[:, None] == jnp.arange(d_ssm)[None, :] // SSM_HEADDIM).astype(BF16)
    return q


def kernel(x_prompt, x_sample, state_lru_conv, state_lru_h, state_ssm_conv, state_ssm, state_rwkv_shift, state_rwkv_wkv, norm_mix, norm_ffn, norm_final, lru_w_in, lru_conv_w, lru_conv_b, lru_w_r, lru_b_r, lru_w_i, lru_b_i, lru_lambda, lru_w_out, ssm_w_in, ssm_conv_w, ssm_conv_b, ssm_dt_bias, ssm_a_log, ssm_d, ssm_norm_w, ssm_w_out, rwkv_mu, rwkv_w_rkv, rwkv_w0, rwkv_w_w1, rwkv_w_w2, rwkv_a0, rwkv_w_a1, rwkv_w_a2, rwkv_w_g1, rwkv_w_g2, rwkv_k_k, rwkv_k_a, rwkv_r_k, rwkv_lnx_w, rwkv_lnx_b, rwkv_w_out, ffn_w1, ffn_w2):
    p = _prep_params(dict(
        norm_mix=norm_mix, norm_ffn=norm_ffn, norm_final=norm_final,
        lru_w_in=lru_w_in, lru_conv_w=lru_conv_w, lru_conv_b=lru_conv_b, lru_w_r=lru_w_r, lru_b_r=lru_b_r,
        lru_w_i=lru_w_i, lru_b_i=lru_b_i, lru_lambda=lru_lambda, lru_w_out=lru_w_out,
        ssm_w_in=ssm_w_in, ssm_conv_w=ssm_conv_w, ssm_conv_b=ssm_conv_b, ssm_dt_bias=ssm_dt_bias,
        ssm_a_log=ssm_a_log, ssm_d=ssm_d, ssm_norm_w=ssm_norm_w, ssm_w_out=ssm_w_out,
        rwkv_mu=rwkv_mu, rwkv_w_rkv=rwkv_w_rkv, rwkv_w0=rwkv_w0, rwkv_w_w1=rwkv_w_w1, rwkv_w_w2=rwkv_w_w2,
        rwkv_a0=rwkv_a0, rwkv_w_a1=rwkv_w_a1, rwkv_w_a2=rwkv_w_a2, rwkv_w_g1=rwkv_w_g1, rwkv_w_g2=rwkv_w_g2,
        rwkv_k_k=rwkv_k_k, rwkv_k_a=rwkv_k_a, rwkv_r_k=rwkv_r_k, rwkv_lnx_w=rwkv_lnx_w, rwkv_lnx_b=rwkv_lnx_b,
        rwkv_w_out=rwkv_w_out, ffn_w1=ffn_w1, ffn_w2=ffn_w2))
    seq_p = x_prompt.shape[1]
    seq_s = x_sample.shape[1]
    cfg_p = dict(lru_nb=8, lru_tl=64, ssd_q=min(SSM_CHUNK, seq_p), ssd_fused=True,
                 rwkv_stepwise=False, rwkv_proj_nb=1, rwkv_nb=1, rwkv_tl=256, rwkv_chunk=RWKV_CHUNK)
    cfg_s = dict(lru_nb=32, lru_tl=seq_s, ssd_q=seq_s, ssd_fused=False,
                 rwkv_stepwise=True)
    states = dict(lru_conv=state_lru_conv, lru_h=state_lru_h, ssm_conv=state_ssm_conv, ssm=state_ssm,
                  rwkv_shift=state_rwkv_shift, rwkv_wkv=state_rwkv_wkv)
    groups = [_Group(x_prompt, None, 0, cfg_p), _Group(x_sample, states, PAST_LEN, cfg_s)]
    (y_p, lc_p, lh_p, sc_p, ss_p, rs_p, rw_p), (y_s, lc_s, lh_s, sc_s, ss_s, rs_s, rw_s) = _run_trunk(groups, p, 512)
    return (y_p, y_s, lc_p, lc_s, lh_p, lh_s, sc_p, sc_s, ss_p, ss_s, rs_p, rs_s, rw_p, rw_s)
```

```python
import functools
import math

import jax
import jax.numpy as jnp
from jax import lax
from jax.experimental import pallas as pl
from jax.experimental.pallas import tpu as pltpu

F32 = jnp.float32
BF16 = jnp.bfloat16

NORM_EPS = 1e-6
CONV_W = 4
LRU_BLOCKS = 8
LRU_C = 8.0
SSM_HEADDIM = 64
SSM_GROUPS = 8
SSM_STATE = 128
SSM_CHUNK = 128
SSM_NORM_EPS = 1e-5
RWKV_HEAD = 64
GN_EPS = 64e-5
PAST_LEN = 16384

V7X_LANES = 128
V7X_SUBLANES = 8
V7X_VMEM_BYTES = 64 * 1024 * 1024
VMEM_LIMIT = V7X_VMEM_BYTES - 8 * 1024 * 1024

RWKV_CHUNK = 64
CONV_PAD = V7X_SUBLANES


def _bdot(a, b):
    return jnp.dot(a.astype(BF16), b.astype(BF16), preferred_element_type=F32)


def _bdot_nt(a, b):
    return lax.dot_general(a.astype(BF16), b.astype(BF16), (((1,), (1,)), ((), ())),
                           preferred_element_type=F32)


def _bdot_tn(a, b):
    return lax.dot_general(a.astype(BF16), b.astype(BF16), (((0,), (0,)), ((), ())),
                           preferred_element_type=F32)


def _rms(x, g):
    return x * lax.rsqrt(jnp.mean(x * x, axis=-1, keepdims=True) + NORM_EPS) * g


def _sigmoid(x):
    return 0.5 * jnp.tanh(0.5 * x) + 0.5


def _softplus(x):
    return jnp.maximum(x, 0.0) + jnp.log1p(jnp.exp(-jnp.abs(x)))


def _silu(x):
    return x * _sigmoid(x)


def _gelu_tanh(x):
    c = math.sqrt(2.0 / math.pi)
    return 0.5 * x * (1.0 + jnp.tanh(c * (x + 0.044715 * (x * x * x))))


def _row_index(rows, period):
    return lax.broadcasted_iota(jnp.int32, (rows, 1), 0) & (period - 1)


def _cumsum_rows(x, t_in, period):
    d = 1
    while d < period:
        x = x + jnp.where(t_in >= d, pltpu.roll(x, d, axis=0), 0.0)
        d *= 2
    return x


def _head_sum(x, head):
    rows, width = x.shape
    lo = lax.broadcasted_iota(jnp.int32, (1, V7X_LANES), 1) < head
    parts = []
    for j in range(width // V7X_LANES):
        seg = x[:, j * V7X_LANES:(j + 1) * V7X_LANES]
        s_lo = jnp.sum(jnp.where(lo, seg, 0.0), axis=-1, keepdims=True)
        s_hi = jnp.sum(jnp.where(lo, 0.0, seg), axis=-1, keepdims=True)
        parts.append(jnp.where(lo, s_lo, s_hi))
    return jnp.concatenate(parts, axis=1)


def _const_spec(shape):
    nd = len(shape)
    return pl.BlockSpec(shape, lambda *_: (0,) * nd, pipeline_mode=pl.Buffered(1))


def _params(sem):
    return pltpu.CompilerParams(dimension_semantics=sem, vmem_limit_bytes=VMEM_LIMIT)


def _proj_kernel(x_ref, g_ref, *refs, n_out, col_chunk):
    w_refs, o_refs = refs[:n_out], refs[n_out:]
    u = _rms(x_ref[...], g_ref[...]).astype(BF16)
    for w_ref, o_ref in zip(w_refs, o_refs):
        n = w_ref.shape[1]
        step = min(col_chunk, n)
        for c0 in range(0, n, step):
            o_ref[:, c0:c0 + step] = jnp.dot(u, w_ref[:, c0:c0 + step], preferred_element_type=F32)


def _norm_proj(x, g, weights, *, tm):
    t, d = x.shape
    assert t % tm == 0
    n_out = len(weights)
    return pl.pallas_call(
        functools.partial(_proj_kernel, n_out=n_out, col_chunk=1024),
        out_shape=[jax.ShapeDtypeStruct((t, w.shape[1]), F32) for w in weights],
        grid=(t // tm,),
        in_specs=[pl.BlockSpec((tm, d), lambda i: (i, 0)), _const_spec((1, d))]
        + [_const_spec(w.shape) for w in weights],
        out_specs=[pl.BlockSpec((tm, w.shape[1]), lambda i: (i, 0)) for w in weights],
        compiler_params=_params(("parallel",)),
        name="norm_proj",
    )(x, g, *weights)


def _ffn_kernel(xa_ref, xb_ref, g_ref, w1_ref, w2_ref, gf_ref, oa_ref, ob_ref, *, n_a, f_chunk, final_norm):
    def run(x_ref, o_ref):
        x = x_ref[...]
        u = _rms(x, g_ref[...]).astype(BF16)
        acc = x
        for c0 in range(0, w1_ref.shape[1], f_chunk):
            h = jnp.dot(u, w1_ref[:, c0:c0 + f_chunk].astype(BF16), preferred_element_type=F32)
            h = jnp.square(jnp.maximum(h, 0.0)).astype(BF16)
            acc = acc + jnp.dot(h, w2_ref[c0:c0 + f_chunk, :].astype(BF16), preferred_element_type=F32)
        if final_norm:
            acc = _rms(acc, gf_ref[...])
        o_ref[...] = acc

    i = pl.program_id(0)

    @pl.when(i < n_a)
    def _():
        run(xa_ref, oa_ref)

    @pl.when(i >= n_a)
    def _():
        run(xb_ref, ob_ref)


def _ffn(xa, xb, g, w1, w2, g_final, *, layer, tm, final_norm):
    d = xa.shape[1]
    f = w1.shape[2]
    assert xa.shape[0] % tm == 0 and xb.shape[0] % tm == 0
    n_a, n_b = xa.shape[0] // tm, xb.shape[0] // tm
    a_map = lambda i: (jnp.minimum(i, n_a - 1), 0)
    b_map = lambda i: (jnp.maximum(i - n_a, 0), 0)
    layer_spec = lambda shape: pl.BlockSpec((None,) + shape, lambda i: (layer, 0, 0), pipeline_mode=pl.Buffered(1))
    return pl.pallas_call(
        functools.partial(_ffn_kernel, n_a=n_a, f_chunk=1024, final_norm=final_norm),
        out_shape=[jax.ShapeDtypeStruct(xa.shape, F32), jax.ShapeDtypeStruct(xb.shape, F32)],
        grid=(n_a + n_b,),
        in_specs=[pl.BlockSpec((tm, d), a_map), pl.BlockSpec((tm, d), b_map), _const_spec((1, d)),
                  layer_spec((d, f)), layer_spec((f, d)), _const_spec((1, d))],
        out_specs=[pl.BlockSpec((tm, d), a_map), pl.BlockSpec((tm, d), b_map)],
        compiler_params=_params(("arbitrary",)),
        name="ffn",
    )(xa, xb, g, w1, w2, g_final)


def _conv_taps(xe_ref, cw, cb, tl):
    nd = len(xe_ref.shape)

    def win(off):
        if nd == 3:
            return xe_ref[:, off:off + tl, :]
        return xe_ref[off:off + tl, :]

    base = CONV_PAD - (CONV_W - 1)
    y = cb + win(base) * cw[0:1, :]
    for k in range(1, CONV_W):
        y = y + win(base + k) * cw[k:k + 1, :]
    return y


def _lru_kernel(x_ref, g_ref, win_ref, cs_ref, h0_ref, cw_ref, cb_ref, wr_ref, br_ref, wi_ref, bi_ref, lam_ref, wo_ref,
                o_ref, cso_ref, ho_ref, xe_ref, hs_ref, hc_ref, *, nb, tl, start_pos):
    c = pl.program_id(1)
    rows = nb * tl
    d = x_ref.shape[2]
    blk = d // LRU_BLOCKS
    ncol = d // V7X_LANES
    pitch = CONV_PAD + tl
    cols = [slice(j * V7X_LANES, (j + 1) * V7X_LANES) for j in range(ncol)]

    @pl.when(c == 0)
    def _():
        for j in range(ncol):
            for s in range(nb):
                xe_ref[j, s * pitch:s * pitch + CONV_PAD, :] = cs_ref[s, :, cols[j]]
        hc_ref[...] = h0_ref[...]

    x = x_ref[...].reshape(rows, d)
    u = _rms(x, g_ref[...]).astype(BF16)
    xb = jnp.dot(u, win_ref[:, 0:d], preferred_element_type=F32)
    for j in range(ncol):
        for s in range(nb):
            xe_ref[j, s * pitch + CONV_PAD:(s + 1) * pitch, :] = xb[s * tl:(s + 1) * tl, cols[j]]

    cw = cw_ref[...]
    cb = cb_ref[...]
    base = CONV_PAD - (CONV_W - 1)
    slabs = []
    for t in range(tl):
        parts = []
        for j in range(ncol):
            acc = cb[:, cols[j]] + xe_ref[j, pl.ds(base + t, nb, stride=pitch), :] * cw[0:1, cols[j]]
            for k in range(1, CONV_W):
                acc = acc + xe_ref[j, pl.ds(base + t + k, nb, stride=pitch), :] * cw[k:k + 1, cols[j]]
            parts.append(acc)
        slabs.append(jnp.concatenate(parts, axis=1))
    xc = jnp.concatenate(slabs, axis=0)
    for j in range(ncol):
        for s in range(nb):
            tail = xe_ref[j, s * pitch + tl:(s + 1) * pitch, :]
            xe_ref[j, s * pitch:s * pitch + CONV_PAD, :] = tail
            cso_ref[s, :, cols[j]] = tail

    xcb = xc.astype(BF16)
    r_pre = jnp.concatenate(
        [jnp.dot(xcb[:, j * blk:(j + 1) * blk], wr_ref[j], preferred_element_type=F32) for j in range(LRU_BLOCKS)],
        axis=1) + br_ref[...]
    i_pre = jnp.concatenate(
        [jnp.dot(xcb[:, j * blk:(j + 1) * blk], wi_ref[j], preferred_element_type=F32) for j in range(LRU_BLOCKS)],
        axis=1) + bi_ref[...]
    r = _sigmoid(r_pre)
    gi = _sigmoid(i_pre)
    log_a = (-LRU_C) * r * _softplus(-lam_ref[...])
    a = jnp.exp(log_a)
    m2 = -jnp.tanh(log_a) * (a * a + 1.0)
    mult = jnp.where(m2 > 0.0, m2 * lax.rsqrt(m2), 0.0)
    if start_pos == 0:
        reset = (lax.broadcasted_iota(jnp.int32, (rows, 1), 0) < nb) & (c == 0)
        a = jnp.where(reset, 0.0, a)
        mult = jnp.where(reset, 1.0, mult)
    b = mult * (gi * xc)

    h = hc_ref[...]
    for t in range(tl):
        h = a[t * nb:(t + 1) * nb] * h + b[t * nb:(t + 1) * nb]
        for j in range(ncol):
            hs_ref[j, pl.ds(t, nb, stride=pitch), :] = h[:, cols[j]]
    hc_ref[...] = h
    ho_ref[...] = h
    h_bm = jnp.concatenate(
        [jnp.concatenate([hs_ref[j, s * pitch:s * pitch + tl, :] for s in range(nb)], axis=0) for j in range(ncol)],
        axis=1)

    gate = jnp.dot(u, win_ref[:, d:2 * d], preferred_element_type=F32)
    y = (h_bm * _gelu_tanh(gate)).astype(BF16)
    o_ref[...] = (x + jnp.dot(y, wo_ref[...], preferred_element_type=F32)).reshape(nb, tl, d)


def _lru_mixer(x3, g, w_in, conv_state, h0, cw, cb, wr, br, wi, bi, lam, wo, *, nb, tl, start_pos):
    batch, seq, d = x3.shape
    nchunk = seq // tl
    assert batch % nb == 0 and seq % tl == 0 and nb % V7X_SUBLANES == 0 and tl % V7X_SUBLANES == 0
    ncol = d // V7X_LANES
    blk_map = lambda i, c: (i, c, 0)
    seq_map = lambda i, c: (i, 0, 0)
    row_map = lambda i, c: (i, 0)
    return pl.pallas_call(
        functools.partial(_lru_kernel, nb=nb, tl=tl, start_pos=start_pos),
        out_shape=[jax.ShapeDtypeStruct((batch, seq, d), F32),
                   jax.ShapeDtypeStruct((batch, CONV_PAD, d), F32),
                   jax.ShapeDtypeStruct((batch, d), F32)],
        grid=(batch // nb, nchunk),
        in_specs=[pl.BlockSpec((nb, tl, d), blk_map), _const_spec(g.shape), _const_spec(w_in.shape),
                  pl.BlockSpec((nb, CONV_PAD, d), seq_map), pl.BlockSpec((nb, d), row_map),
                  _const_spec(cw.shape), _const_spec(cb.shape), _const_spec(wr.shape), _const_spec(br.shape),
                  _const_spec(wi.shape), _const_spec(bi.shape), _const_spec(lam.shape), _const_spec(wo.shape)],
        out_specs=[pl.BlockSpec((nb, tl, d), blk_map), pl.BlockSpec((nb, CONV_PAD, d), seq_map),
                   pl.BlockSpec((nb, d), row_map)],
        scratch_shapes=[pltpu.VMEM((ncol, nb * (CONV_PAD + tl), V7X_LANES), F32),
                        pltpu.VMEM((ncol, nb * (CONV_PAD + tl), V7X_LANES), F32),
                        pltpu.VMEM((nb, d), F32)],
        compiler_params=_params(("parallel", "arbitrary")),
        name="lru_mixer",
    )(x3, g, w_in, conv_state, h0, cw, cb, wr, br, wi, bi, lam, wo)


def _split_bf16(x, terms):
    parts = []
    for _ in range(terms - 1):
        hi = x.astype(BF16).astype(F32)
        parts.append(hi)
        x = x - hi
    parts.append(x)
    return parts


def _ssd_kernel(x_ref, g_ref, pz_ref, pxbc_ref, pdt_ref, cs_ref, s0_ref, cw_ref, cb_ref, dtb_ref, alog_ref, dskx_ref, nw_ref,
                ex_ref, wo_ref, o_ref, cso_ref, so_ref, xe_ref, st_ref, *, q, zero_state, fused_proj):
    c = pl.program_id(1)
    d_ssm = pz_ref.shape[1]
    n = SSM_STATE
    hpg = d_ssm // SSM_HEADDIM // SSM_GROUPS
    gw = hpg * SSM_HEADDIM
    nblk = d_ssm // n

    @pl.when(c == 0)
    def _():
        xe_ref[0:CONV_PAD, :] = cs_ref[0]
        if zero_state:
            st_ref[...] = jnp.zeros(st_ref.shape, F32)
        else:
            for j in range(nblk):
                st_ref[:, j * n:(j + 1) * n] = s0_ref[0, j * n:(j + 1) * n, :].T

    x = x_ref[...]
    if fused_proj:
        u = _rms(x, g_ref[...]).astype(BF16)
        for c0 in range(0, pxbc_ref.shape[1], 1024):
            xe_ref[CONV_PAD:CONV_PAD + q, c0:c0 + 1024] = jnp.dot(u, pxbc_ref[:, c0:c0 + 1024],
                                                                  preferred_element_type=F32)
        dt_pre = jnp.dot(u, pdt_ref[...], preferred_element_type=F32)
    else:
        xe_ref[CONV_PAD:CONV_PAD + q, :] = pxbc_ref[...]
        dt_pre = pdt_ref[...]
    xbc = _silu(_conv_taps(xe_ref, cw_ref[...], cb_ref[...], q))
    tail = xe_ref[q:q + CONV_PAD, :]
    xe_ref[0:CONV_PAD, :] = tail
    cso_ref[0] = tail

    xs = xbc[:, 0:d_ssm]
    bm = xbc[:, d_ssm:d_ssm + SSM_GROUPS * n]
    cm = xbc[:, d_ssm + SSM_GROUPS * n:]
    dt = _softplus(dt_pre + dtb_ref[...])
    da = dt * (-jnp.exp(alog_ref[...]))
    t_in = _row_index(q, q)
    acs = _cumsum_rows(da, t_in, q)
    if q < V7X_LANES:
        acs_sq = jnp.concatenate([acs, jnp.zeros((V7X_LANES - q, V7X_LANES), F32)], axis=0)
    else:
        acs_sq = acs
    acs_t = acs_sq.T

    lhs = jnp.concatenate(_split_bf16(dt, 2) + _split_bf16(acs, 3), axis=0).astype(BF16)
    ex = jnp.dot(lhs, ex_ref[...], preferred_element_type=F32)
    dt_x = ex[0:q] + ex[q:2 * q]
    acs_x = ex[2 * q:3 * q] + ex[3 * q:4 * q] + ex[4 * q:5 * q]
    last_x = acs_x[q - 1:q, :]
    xdt = (xs * dt_x).astype(BF16)
    xdtd = (xs * (dt_x * jnp.exp(last_x - acs_x))).astype(BF16)

    tri = lax.broadcasted_iota(jnp.int32, (q, q), 0) >= lax.broadcasted_iota(jnp.int32, (q, q), 1)
    lane_head = lax.broadcasted_iota(jnp.int32, (1, gw), 1) // SSM_HEADDIM
    groups = range(SSM_GROUPS)
    bgs = [bm[:, g * n:(g + 1) * n].astype(BF16) for g in groups]
    cgs = [cm[:, g * n:(g + 1) * n].astype(BF16) for g in groups]
    cb_mats = [_bdot_nt(cgs[g], bgs[g]) for g in groups]
    y_off = [_bdot(cgs[g], st_ref[:, g * gw:(g + 1) * gw]) for g in groups]
    y_diag = []
    for g in groups:
        xdt_g = xdt[:, g * gw:(g + 1) * gw]
        acc = None
        for e in range(hpg):
            h = g * hpg + e
            seg = acs[:, h:h + 1] - acs_t[h:h + 1, 0:q]
            l_mat = jnp.exp(jnp.where(tri, seg, -jnp.inf))
            part = jnp.dot((cb_mats[g] * l_mat).astype(BF16), jnp.where(lane_head == e, xdt_g, jnp.zeros_like(xdt_g)),
                           preferred_element_type=F32)
            acc = part if acc is None else acc + part
        y_diag.append(acc)
    st_new = [_bdot_tn(bgs[g], xdtd[:, g * gw:(g + 1) * gw]) for g in groups]
    e_last_x = jnp.exp(last_x)
    for g in groups:
        cols = slice(g * gw, (g + 1) * gw)
        st_ref[:, cols] = st_ref[:, cols] * e_last_x[:, cols] + st_new[g]

    @pl.when(c == pl.num_programs(1) - 1)
    def _():
        for j in range(nblk):
            so_ref[0, j * n:(j + 1) * n, :] = st_ref[:, j * n:(j + 1) * n].T

    y = jnp.concatenate(y_diag, axis=1) + jnp.concatenate(y_off, axis=1) * jnp.exp(acs_x) + dskx_ref[...] * xs
    y = y * _silu(jnp.dot(u, pz_ref[...], preferred_element_type=F32) if fused_proj else pz_ref[...])
    norm_parts = []
    for g in groups:
        yg = y[:, g * gw:(g + 1) * gw]
        norm_parts.append(yg * lax.rsqrt(jnp.mean(yg * yg, axis=-1, keepdims=True) + SSM_NORM_EPS))
    yn = (jnp.concatenate(norm_parts, axis=1) * nw_ref[...]).astype(BF16)
    o_ref[...] = x + jnp.dot(yn, wo_ref[...], preferred_element_type=F32)


def _ssd_mixer(x, g, w_z, w_xbc, w_dt, conv_state, s0, cw, cb, dtb, alog, dskx, nw, expand, wo, *, batch, seq, q,
               fused_proj):
    d = x.shape[1]
    d_ssm = w_z.shape[1]
    cdim = w_xbc.shape[1]
    nchunk = seq // q
    assert seq % q == 0
    zero_state = s0 is None
    if zero_state:
        s0 = jnp.zeros((1, d_ssm, SSM_STATE), F32)
    row_map = lambda b, c: (b * nchunk + c, 0)
    seq_map = lambda b, c: (b, 0, 0)
    s0_map = (lambda b, c: (0, 0, 0)) if zero_state else seq_map
    if fused_proj:
        proj = (w_z, w_xbc, w_dt)
        proj_specs = [_const_spec(w.shape) for w in proj]
    else:
        proj = _norm_proj(x, g, [w_z, w_xbc, w_dt], tm=min(256, batch * seq))
        proj_specs = [pl.BlockSpec((q, a.shape[1]), row_map) for a in proj]
    return pl.pallas_call(
        functools.partial(_ssd_kernel, q=q, zero_state=zero_state, fused_proj=fused_proj),
        out_shape=[jax.ShapeDtypeStruct((batch * seq, d), F32),
                   jax.ShapeDtypeStruct((batch, CONV_PAD, cdim), F32),
                   jax.ShapeDtypeStruct((batch, d_ssm, SSM_STATE), F32)],
        grid=(batch, nchunk),
        in_specs=[pl.BlockSpec((q, d), row_map), _const_spec(g.shape)] + proj_specs
        + [pl.BlockSpec((1, CONV_PAD, cdim), seq_map), pl.BlockSpec((1, d_ssm, SSM_STATE), s0_map),
                  _const_spec(cw.shape), _const_spec(cb.shape), _const_spec(dtb.shape), _const_spec(alog.shape),
                  _const_spec(dskx.shape), _const_spec(nw.shape), _const_spec(expand.shape), _const_spec(wo.shape)],
        out_specs=[pl.BlockSpec((q, d), row_map), pl.BlockSpec((1, CONV_PAD, cdim), seq_map),
                   pl.BlockSpec((1, d_ssm, SSM_STATE), seq_map)],
        scratch_shapes=[pltpu.VMEM((CONV_PAD + q, cdim), F32), pltpu.VMEM((SSM_STATE, d_ssm), F32)],
        compiler_params=_params(("parallel", "arbitrary")),
        name="ssd_mixer",
    )(x, g, *proj, conv_state, s0, cw, cb, dtb, alog, dskx, nw, expand, wo)


def _rwkv_proj_kernel(x_ref, sh_ref, g_ref, mu_ref, wrkv_ref, w0_ref, w1_ref, w2_ref, a0_ref, a1_ref, a2_ref,
                      g1_ref, g2_ref, kkw_ref, kaw_ref,
                      r_o, k_o, v_o, kk_o, bb_o, lw_o, g_o, sho_ref, carry_ref, *, nb, tl, time_major):
    c = pl.program_id(1)
    rows = x_ref.shape[0]
    d = x_ref.shape[1]

    @pl.when(c == 0)
    def _():
        carry_ref[...] = sh_ref[...]

    u = _rms(x_ref[...], g_ref[...])
    if time_major:
        prev = carry_ref[...]
        last = u
    else:
        t_in = _row_index(rows, tl)
        prev = jnp.where(t_in == 0, jnp.broadcast_to(carry_ref[...], (nb, tl, d)).reshape(rows, d),
                         pltpu.roll(u, 1, axis=0))
        last = u.reshape(nb, tl, d)[:, tl - 1:tl, :]
    carry_ref[...] = last
    sho_ref[...] = last

    diff = prev - u
    mu = mu_ref[...]

    def mixed(s):
        return (u + diff * mu[s:s + 1, :]).astype(BF16)

    r = jnp.dot(mixed(0), wrkv_ref[0], preferred_element_type=F32)
    k = jnp.dot(mixed(1), wrkv_ref[1], preferred_element_type=F32)
    v_o[...] = jnp.dot(mixed(2), wrkv_ref[2], preferred_element_type=F32)
    w_pre = w0_ref[...] + _bdot(jnp.tanh(jnp.dot(mixed(3), w1_ref[...], preferred_element_type=F32)), w2_ref[...])
    sp = jnp.maximum(-w_pre, 0.0) + jnp.log(1.0 + jnp.exp(-jnp.abs(w_pre)))
    lw_o[...] = -jnp.exp(-sp - 0.5)
    a = _sigmoid(a0_ref[...] + _bdot(jnp.dot(mixed(4), a1_ref[...], preferred_element_type=F32), a2_ref[...]))
    g_o[...] = _bdot(_sigmoid(jnp.dot(mixed(5), g1_ref[...], preferred_element_type=F32)), g2_ref[...])
    kk = k * kkw_ref[...]
    kk = kk * jnp.minimum(lax.rsqrt(_head_sum(kk * kk, RWKV_HEAD)), 1e12)
    r_o[...] = r
    k_o[...] = k * (1.0 + (a - 1.0) * kaw_ref[...])
    kk_o[...] = kk
    bb_o[...] = kk * a


def _rwkv_proj(x, shift0, g, mu, wrkv, w0, w1, w2, a0, a1, a2, g1, g2, kkw, kaw, *, batch, seq, nb, tl, time_major):
    d = x.shape[1]
    consts = (g, mu, wrkv, w0, w1, w2, a0, a1, a2, g1, g2, kkw, kaw)
    if time_major:
        rows, grid = batch, (1, seq)
        row_map = lambda i, c: (c, 0)
        sh_shape, sh_block, sh_map = (batch, d), (batch, d), (lambda i, c: (0, 0))
    else:
        nchunk = seq // tl
        assert batch % nb == 0 and seq % tl == 0 and (nb == 1 or nchunk == 1)
        rows, grid = nb * tl, (batch // nb, nchunk)
        row_map = lambda i, c: (i * nchunk + c, 0)
        sh_shape, sh_block, sh_map = (batch, 1, d), (nb, 1, d), (lambda i, c: (i, 0, 0))
    return pl.pallas_call(
        functools.partial(_rwkv_proj_kernel, nb=nb, tl=tl, time_major=time_major),
        out_shape=[jax.ShapeDtypeStruct((batch * seq, d), F32)] * 7 + [jax.ShapeDtypeStruct(sh_shape, F32)],
        grid=grid,
        in_specs=[pl.BlockSpec((rows, d), row_map), pl.BlockSpec(sh_block, sh_map)]
        + [_const_spec(w.shape) for w in consts],
        out_specs=[pl.BlockSpec((rows, d), row_map)] * 7 + [pl.BlockSpec(sh_block, sh_map)],
        scratch_shapes=[pltpu.VMEM(sh_block, F32)],
        compiler_params=_params(("parallel", "arbitrary")),
        name="rwkv_proj",
    )(x, shift0, *consts)


def _rwkv_chunk_kernel(r_ref, k_ref, v_ref, kk_ref, bb_ref, lw_ref, g_ref, x_ref, s0_ref, rk_ref, lnw_ref, lnb_ref,
                       wo_ref, o_ref, so_ref, s_ref, og_ref, *, nb, tl, chunk, zero_state):
    c = pl.program_id(1)
    d = x_ref.shape[1]
    hd = RWKV_HEAD
    heads = d // hd
    cpb = tl // chunk
    n_iter = nb * cpb

    @pl.when(c == 0)
    def _():
        if zero_state:
            s_ref[...] = jnp.zeros(s_ref.shape, F32)
        else:
            s_ref[...] = s0_ref[...]

    t_in = _row_index(chunk, chunk)
    ri = lax.broadcasted_iota(jnp.int32, (2 * chunk, 2 * chunk), 0)
    ci = lax.broadcasted_iota(jnp.int32, (2 * chunk, 2 * chunk), 1) & (chunk - 1)
    blk_mask = jnp.where(ri < chunk, ri - 1, ri - chunk) >= ci

    def body(it, carry):
        seq_i = it if cpb == 1 else 0
        rows = pl.ds(pl.multiple_of(it * chunk, chunk), chunk)
        r = r_ref[rows, :]
        k = k_ref[rows, :]
        v = v_ref[rows, :]
        kk = kk_ref[rows, :]
        bb = bb_ref[rows, :]
        lw = lw_ref[rows, :]
        cum = _cumsum_rows(lw, t_in, chunk)
        c_last = cum[chunk - 1:chunk, :]
        g_last = jnp.exp(c_last)
        e_inv = jnp.exp(-cum)
        e_end = jnp.exp(c_last - cum)
        p_st = jnp.concatenate([kk * jnp.exp(cum - lw), r * jnp.exp(cum)], axis=0).astype(BF16)
        q_st = jnp.concatenate([bb * e_inv, k * e_inv], axis=0).astype(BF16)
        e_st = jnp.concatenate([k * e_end, bb * e_end], axis=0).astype(BF16)
        zv_st = jnp.concatenate([jnp.zeros_like(v), v], axis=0).astype(BF16)
        hs = range(heads)
        sls = [slice(h * hd, (h + 1) * hd) for h in hs]
        s_l = [s_ref[seq_i, sl, :] for sl in sls]
        g_m = [jnp.where(blk_mask, _bdot_nt(p_st[:, sl], q_st[:, sl]), 0.0) for sl in sls]
        ps = [_bdot_nt(p_st[:, sls[h]], s_l[h]) for h in hs]
        t1 = [_bdot(g_m[h], zv_st[:, sls[h]]) for h in hs]
        nmat = [-g_m[h][0:chunk, 0:chunk] for h in hs]
        w_l = [ps[h][0:chunk] + t1[h][0:chunk] for h in hs]
        u_l = [w_l[h] + _bdot(nmat[h], w_l[h]) for h in hs]
        m = 2
        while m < chunk:
            nmat = [_bdot(x, x) for x in nmat]
            u_l = [u_l[h] + _bdot(nmat[h], u_l[h]) for h in hs]
            m *= 2
        o_parts = [ps[h][chunk:] + t1[h][chunk:] - _bdot(g_m[h][chunk:, 0:chunk], u_l[h]) for h in hs]
        for h in hs:
            vu = jnp.concatenate([v[:, sls[h]], -u_l[h]], axis=0)
            s_ref[seq_i, sls[h], :] = s_l[h] * g_last[:, sls[h]] + _bdot_tn(vu, e_st[:, sls[h]])
        o = jnp.concatenate(o_parts, axis=1)
        mean = _head_sum(o, hd) * (1.0 / hd)
        oc = o - mean
        var = _head_sum(oc * oc, hd) * (1.0 / hd)
        on = oc * lax.rsqrt(var + GN_EPS) * lnw_ref[...] + lnb_ref[...]
        bonus = _head_sum(r * k * rk_ref[...], hd) * v
        og_ref[rows, :] = (on + bonus) * g_ref[rows, :]
        return carry

    lax.fori_loop(0, n_iter, body, 0)

    @pl.when(c == pl.num_programs(1) - 1)
    def _():
        so_ref[...] = s_ref[...]

    o_ref[...] = x_ref[...] + _bdot(og_ref[...], wo_ref[...])


def _rwkv_mixer(x, r, k, v, kk, bb, lw, g, s0, rk, lnw, lnb, wo, *, batch, seq, nb, tl, chunk):
    d = x.shape[1]
    nchunk = seq // tl
    assert batch % nb == 0 and seq % tl == 0 and tl % chunk == 0 and (nb == 1 or tl == chunk)
    rows = nb * tl
    zero_state = s0 is None
    if zero_state:
        s0 = jnp.zeros((nb, d, RWKV_HEAD), F32)
    row_map = lambda i, c: (i * nchunk + c, 0)
    seq_map = lambda i, c: (i, 0, 0)
    s0_map = (lambda i, c: (0, 0, 0)) if zero_state else seq_map
    row_spec = pl.BlockSpec((rows, d), row_map)
    return pl.pallas_call(
        functools.partial(_rwkv_chunk_kernel, nb=nb, tl=tl, chunk=chunk, zero_state=zero_state),
        out_shape=[jax.ShapeDtypeStruct((batch * seq, d), F32), jax.ShapeDtypeStruct((batch, d, RWKV_HEAD), F32)],
        grid=(batch // nb, nchunk),
        in_specs=[row_spec] * 8 + [pl.BlockSpec((nb, d, RWKV_HEAD), s0_map),
                                   _const_spec(rk.shape), _const_spec(lnw.shape), _const_spec(lnb.shape),
                                   _const_spec(wo.shape)],
        out_specs=[row_spec, pl.BlockSpec((nb, d, RWKV_HEAD), seq_map)],
        scratch_shapes=[pltpu.VMEM((nb, d, RWKV_HEAD), F32), pltpu.VMEM((rows, d), F32)],
        compiler_params=_params(("parallel", "arbitrary")),
        name="rwkv_chunk",
    )(r, k, v, kk, bb, lw, g, x, s0, rk, lnw, lnb, wo)


def _rwkv_seq_kernel(r_ref, k_ref, v_ref, kk_ref, bb_ref, lw_ref, s0_ref, o_ref, so_ref, xt_ref, ot_ref, *, seq, batch):
    hd = RWKV_HEAD
    for i, ref in enumerate((r_ref, k_ref, v_ref, kk_ref, bb_ref, lw_ref)):
        for t in range(seq):
            tile = ref[t * batch:(t + 1) * batch, :]
            if ref is lw_ref:
                tile = jnp.exp(tile)
            xt_ref[i * seq + t] = tile.T
    sub = V7X_SUBLANES
    for hh in range(2):
        base = hh * hd
        for t in range(seq):
            src = s0_ref if t == 0 else so_ref
            r_t, k_t, kk_t, bb_t, w_t = (xt_ref[i * seq + t, base:base + hd, :] for i in (0, 1, 3, 4, 5))

            def body(vg, carry, src=src, t=t, base=base, hh=hh, r_t=r_t, k_t=k_t, kk_t=kk_t, bb_t=bb_t, w_t=w_t):
                v0 = pl.multiple_of(base + vg * sub, sub)
                v_tile = xt_ref[2 * seq + t, pl.ds(v0, sub), :]
                rows = []
                for i in range(sub):
                    off = pl.multiple_of((vg * sub + i) * hd, hd)
                    sv = src[hh, pl.ds(off, hd), :]
                    sa = jnp.sum(sv * kk_t, axis=0, keepdims=True)
                    sv = sv * w_t - sa * bb_t + v_tile[i:i + 1, :] * k_t
                    so_ref[hh, pl.ds(off, hd), :] = sv
                    rows.append(jnp.sum(sv * r_t, axis=0, keepdims=True))
                ot_ref[t, pl.ds(v0, sub), :] = jnp.concatenate(rows, axis=0)
                return carry

            lax.fori_loop(0, hd // sub, body, 0)
    for t in range(seq):
        o_ref[t * batch:(t + 1) * batch, :] = ot_ref[t].T


def _rwkv_seq(r, k, v, kk, bb, lw, s0, *, batch, seq):
    d = r.shape[1]
    assert batch == V7X_LANES and d % V7X_LANES == 0
    npair = d // V7X_LANES
    col_spec = pl.BlockSpec((seq * batch, V7X_LANES), lambda j: (0, j))
    st_spec = pl.BlockSpec((2, RWKV_HEAD * RWKV_HEAD, batch), lambda j: (j, 0, 0))
    return pl.pallas_call(
        functools.partial(_rwkv_seq_kernel, seq=seq, batch=batch),
        out_shape=[jax.ShapeDtypeStruct((seq * batch, d), F32), jax.ShapeDtypeStruct(s0.shape, F32)],
        grid=(npair,),
        in_specs=[col_spec] * 6 + [st_spec],
        out_specs=[col_spec, st_spec],
        scratch_shapes=[pltpu.VMEM((6 * seq, V7X_LANES, batch), F32), pltpu.VMEM((seq, V7X_LANES, batch), F32)],
        compiler_params=_params(("parallel",)),
        name="rwkv_seq",
    )(r, k, v, kk, bb, lw, s0)


def _rwkv_finish_kernel(o_ref, r_ref, k_ref, v_ref, g_ref, x_ref, rk_ref, lnw_ref, lnb_ref, wo_ref, out_ref):
    hd = RWKV_HEAD
    o = o_ref[...]
    mean = _head_sum(o, hd) * (1.0 / hd)
    oc = o - mean
    var = _head_sum(oc * oc, hd) * (1.0 / hd)
    on = oc * lax.rsqrt(var + GN_EPS) * lnw_ref[...] + lnb_ref[...]
    bonus = _head_sum(r_ref[...] * k_ref[...] * rk_ref[...], hd) * v_ref[...]
    out_ref[...] = x_ref[...] + _bdot((on + bonus) * g_ref[...], wo_ref[...])


def _rwkv_finish(o, r, k, v, g, x, rk, lnw, lnb, wo, *, tm):
    t, d = x.shape
    assert t % tm == 0
    row_spec = pl.BlockSpec((tm, d), lambda i: (i, 0))
    return pl.pallas_call(
        _rwkv_finish_kernel,
        out_shape=jax.ShapeDtypeStruct((t, d), F32),
        grid=(t // tm,),
        in_specs=[row_spec] * 6 + [_const_spec(rk.shape), _const_spec(lnw.shape), _const_spec(lnb.shape),
                                   _const_spec(wo.shape)],
        out_specs=row_spec,
        compiler_params=_params(("parallel",)),
        name="rwkv_finish",
    )(o, r, k, v, g, x, rk, lnw, lnb, wo)


def _pad_conv_state(state):
    return jnp.pad(state, ((0, 0), (CONV_PAD - (CONV_W - 1), 0), (0, 0)))


class _Group:
    def __init__(self, x3, states, start_pos, cfg):
        self.batch, self.seq, d = x3.shape
        self.x = x3.reshape(self.batch * self.seq, d)
        self.states, self.start_pos, self.cfg = states, start_pos, cfg
        self.out = dict(lru_conv=[], lru_h=[], ssm_conv=[], ssm=[], rwkv_shift=[], rwkv_wkv=[])

    def state(self, name, idx, zero_shape):
        return self.states[name][idx] if self.states else jnp.zeros(zero_shape, F32)


def _mixer_layer(grp, layer, idx, p):
    batch, seq, cfg = grp.batch, grp.seq, grp.cfg
    x = grp.x
    d = x.shape[1]
    g_mix = p["norm_mix"][layer][None, :]
    kind = layer % 3
    if kind == 0:
        cs = grp.state("lru_conv", idx, (batch, CONV_W - 1, d))
        h0 = grp.state("lru_h", idx, (batch, d))
        x3, cso, ho = _lru_mixer(
            x.reshape(batch, seq, d), g_mix, p["lru_w_in"][idx], _pad_conv_state(cs), h0,
            p["lru_conv_w"][idx], p["lru_conv_b"][idx][None, :], p["lru_w_r"][idx], p["lru_b_r"][idx][None, :],
            p["lru_w_i"][idx], p["lru_b_i"][idx][None, :], p["lru_lambda"][idx][None, :], p["lru_w_out"][idx],
            nb=cfg["lru_nb"], tl=cfg["lru_tl"], start_pos=grp.start_pos)
        grp.x = x3.reshape(batch * seq, d)
        grp.out["lru_conv"].append(cso[:, CONV_PAD - (CONV_W - 1):, :])
        grp.out["lru_h"].append(ho)
    elif kind == 1:
        cdim = p["ssm_w_xbc"].shape[2]
        cs = grp.state("ssm_conv", idx, (batch, CONV_W - 1, cdim))
        s0 = grp.states["ssm"][idx].reshape(batch, -1, SSM_STATE) if grp.states else None
        grp.x, cso, so = _ssd_mixer(
            x, g_mix, p["ssm_w_z"][idx], p["ssm_w_xbc"][idx], p["ssm_w_dt"][idx], _pad_conv_state(cs), s0,
            p["ssm_conv_w"][idx], p["ssm_conv_b"][idx][None, :], p["ssm_dt_bias"][idx], p["ssm_a_log"][idx],
            p["ssm_d_x"][idx], p["ssm_norm_w"][idx][None, :], p["ssm_expand"], p["ssm_w_out"][idx],
            batch=batch, seq=seq, q=cfg["ssd_q"], fused_proj=cfg["ssd_fused"])
        grp.out["ssm_conv"].append(cso[:, CONV_PAD - (CONV_W - 1):, :])
        grp.out["ssm"].append(so.reshape(batch, -1, SSM_HEADDIM, SSM_STATE))
    elif cfg["rwkv_stepwise"]:
        heads = d // RWKV_HEAD
        proj_w = (g_mix, p["rwkv_mu"][idx], p["rwkv_w_rkv"][idx], p["rwkv_w0"][idx][None, :],
                  p["rwkv_w_w1"][idx], p["rwkv_w_w2"][idx], p["rwkv_a0"][idx][None, :], p["rwkv_w_a1"][idx],
                  p["rwkv_w_a2"][idx], p["rwkv_w_g1"][idx], p["rwkv_w_g2"][idx], p["rwkv_k_k"][idx][None, :],
                  p["rwkv_k_a"][idx][None, :])
        x_tm = x.reshape(batch, seq, d).transpose(1, 0, 2).reshape(seq * batch, d)
        r, k, v, kk, bb, lw, g, sho = _rwkv_proj(x_tm, grp.state("rwkv_shift", idx, (batch, d)), *proj_w,
                                                 batch=batch, seq=seq, nb=1, tl=1, time_major=True)
        s0 = grp.state("rwkv_wkv", idx, (batch, heads, RWKV_HEAD, RWKV_HEAD))
        s0 = s0.transpose(1, 2, 3, 0).reshape(heads, RWKV_HEAD * RWKV_HEAD, batch)
        o, so = _rwkv_seq(r, k, v, kk, bb, lw, s0, batch=batch, seq=seq)
        x_tm = _rwkv_finish(o, r, k, v, g, x_tm, p["rwkv_r_k"][idx].reshape(1, d), p["rwkv_lnx_w"][idx][None, :],
                            p["rwkv_lnx_b"][idx][None, :], p["rwkv_w_out"][idx], tm=min(256, seq * batch))
        grp.x = x_tm.reshape(seq, batch, d).transpose(1, 0, 2).reshape(batch * seq, d)
        grp.out["rwkv_shift"].append(sho)
        grp.out["rwkv_wkv"].append(so.reshape(heads, RWKV_HEAD, RWKV_HEAD, batch).transpose(3, 0, 1, 2))
    else:
        sh0 = grp.state("rwkv_shift", idx, (batch, d))
        s0 = grp.states["rwkv_wkv"][idx].reshape(batch, d, RWKV_HEAD) if grp.states else None
        r, k, v, kk, bb, lw, g, sho = _rwkv_proj(
            x, sh0[:, None, :], g_mix, p["rwkv_mu"][idx], p["rwkv_w_rkv"][idx], p["rwkv_w0"][idx][None, :],
            p["rwkv_w_w1"][idx], p["rwkv_w_w2"][idx], p["rwkv_a0"][idx][None, :], p["rwkv_w_a1"][idx],
            p["rwkv_w_a2"][idx], p["rwkv_w_g1"][idx], p["rwkv_w_g2"][idx], p["rwkv_k_k"][idx][None, :],
            p["rwkv_k_a"][idx][None, :], batch=batch, seq=seq, nb=cfg["rwkv_proj_nb"], tl=cfg["rwkv_tl"],
            time_major=False)
        grp.x, so = _rwkv_mixer(
            x, r, k, v, kk, bb, lw, g, s0, p["rwkv_r_k"][idx].reshape(1, d), p["rwkv_lnx_w"][idx][None, :],
            p["rwkv_lnx_b"][idx][None, :], p["rwkv_w_out"][idx],
            batch=batch, seq=seq, nb=cfg["rwkv_nb"], tl=cfg["rwkv_tl"], chunk=cfg["rwkv_chunk"])
        grp.out["rwkv_shift"].append(sho[:, 0, :])
        grp.out["rwkv_wkv"].append(so.reshape(batch, d // RWKV_HEAD, RWKV_HEAD, RWKV_HEAD))


def _run_trunk(groups, p, tm_ffn):
    depth = p["norm_mix"].shape[0]
    counts = [0, 0, 0]
    for layer in range(depth):
        kind = layer % 3
        for grp in groups:
            _mixer_layer(grp, layer, counts[kind], p)
        counts[kind] += 1
        groups[0].x, groups[1].x = _ffn(
            groups[0].x, groups[1].x, p["norm_ffn"][layer][None, :], p["ffn_w1"], p["ffn_w2"],
            p["norm_final"][None, :], layer=layer, tm=tm_ffn, final_norm=(layer == depth - 1))
    stack = lambda parts: parts[0][None] if len(parts) == 1 else jnp.stack(parts)
    results = []
    for grp in groups:
        d = grp.x.shape[1]
        results.append([grp.x.reshape(grp.batch, grp.seq, d)]
                       + [stack(grp.out[name]) for name in ("lru_conv", "lru_h", "ssm_conv", "ssm", "rwkv_shift", "rwkv_wkv")])
    return results


def _prep_params(p):
    q = dict(p)
    for name in ("lru_w_in", "lru_w_r", "lru_w_i", "lru_w_out", "ssm_w_out", "rwkv_w_rkv", "rwkv_w_w1", "rwkv_w_w2",
                 "rwkv_w_a1", "rwkv_w_a2", "rwkv_w_g1", "rwkv_w_g2", "rwkv_w_out"):
        q[name] = p[name].astype(BF16)
    d_ssm = p["ssm_w_out"].shape[1]
    heads = p["ssm_dt_bias"].shape[1]
    w_in = p["ssm_w_in"]
    cdim = w_in.shape[2] - d_ssm - heads
    pad = V7X_LANES - heads
    q["ssm_w_z"] = w_in[:, :, :d_ssm].astype(BF16)
    q["ssm_w_xbc"] = w_in[:, :, d_ssm:d_ssm + cdim].astype(BF16)
    q["ssm_w_dt"] = jnp.pad(w_in[:, :, d_ssm + cdim:], ((0, 0), (0, 0), (0, pad))).astype(BF16)
    for name in ("ssm_dt_bias", "ssm_a_log"):
        q[name] = jnp.pad(p[name], ((0, 0), (0, pad)))[:, None, :]
    q["ssm_d_x"] = jnp.repeat(p["ssm_d"], SSM_HEADDIM, axis=1)[:, None, :]
    q["ssm_expand"] = (jnp.arange(V7X_LANES)[:, None] == jnp.arange(d_ssm)[None, :] // SSM_HEADDIM).astype(BF16)
    return q


def kernel(x_prompt, x_sample, state_lru_conv, state_lru_h, state_ssm_conv, state_ssm, state_rwkv_shift, state_rwkv_wkv, norm_mix, norm_ffn, norm_final, lru_w_in, lru_conv_w, lru_conv_b, lru_w_r, lru_b_r, lru_w_i, lru_b_i, lru_lambda, lru_w_out, ssm_w_in, ssm_conv_w, ssm_conv_b, ssm_dt_bias, ssm_a_log, ssm_d, ssm_norm_w, ssm_w_out, rwkv_mu, rwkv_w_rkv, rwkv_w0, rwkv_w_w1, rwkv_w_w2, rwkv_a0, rwkv_w_a1, rwkv_w_a2, rwkv_w_g1, rwkv_w_g2, rwkv_k_k, rwkv_k_a, rwkv_r_k, rwkv_lnx_w, rwkv_lnx_b, rwkv_w_out, ffn_w1, ffn_w2):
    p = _prep_params(dict(
        norm_mix=norm_mix, norm_ffn=norm_ffn, norm_final=norm_final,
        lru_w_in=lru_w_in, lru_conv_w=lru_conv_w, lru_conv_b=lru_conv_b, lru_w_r=lru_w_r, lru_b_r=lru_b_r,
        lru_w_i=lru_w_i, lru_b_i=lru_b_i, lru_lambda=lru_lambda, lru_w_out=lru_w_out,
        ssm_w_in=ssm_w_in, ssm_conv_w=ssm_conv_w, ssm_conv_b=ssm_conv_b, ssm_dt_bias=ssm_dt_bias,
        ssm_a_log=ssm_a_log, ssm_d=ssm_d, ssm_norm_w=ssm_norm_w, ssm_w_out=ssm_w_out,
        rwkv_mu=rwkv_mu, rwkv_w_rkv=rwkv_w_rkv, rwkv_w0=rwkv_w0, rwkv_w_w1=rwkv_w_w1, rwkv_w_w2=rwkv_w_w2,
        rwkv_a0=rwkv_a0, rwkv_w_a1=rwkv_w_a1, rwkv_w_a2=rwkv_w_a2, rwkv_w_g1=rwkv_w_g1, rwkv_w_g2=rwkv_w_g2,
        rwkv_k_k=rwkv_k_k, rwkv_k_a=rwkv_k_a, rwkv_r_k=rwkv_r_k, rwkv_lnx_w=rwkv_lnx_w, rwkv_lnx_b=rwkv_lnx_b,
        rwkv_w_out=rwkv_w_out, ffn_w1=ffn_w1, ffn_w2=ffn_w2))
    seq_p = x_prompt.shape[1]
    seq_s = x_sample.shape[1]
    cfg_p = dict(lru_nb=8, lru_tl=64, ssd_q=min(SSM_CHUNK, seq_p), ssd_fused=True,
                 rwkv_stepwise=False, rwkv_proj_nb=1, rwkv_nb=1, rwkv_tl=256, rwkv_chunk=RWKV_CHUNK)
    cfg_s = dict(lru_nb=32, lru_tl=seq_s, ssd_q=seq_s, ssd_fused=False,
                 rwkv_stepwise=True)
    states = dict(lru_conv=state_lru_conv, lru_h=state_lru_h, ssm_conv=state_ssm_conv, ssm=state_ssm,
                  rwkv_shift=state_rwkv_shift, rwkv_wkv=state_rwkv_wkv)
    groups = [_Group(x_prompt, None, 0, cfg_p), _Group(x_sample, states, PAST_LEN, cfg_s)]
    (y_p, lc_p, lh_p, sc_p, ss_p, rs_p, rw_p), (y_s, lc_s, lh_s, sc_s, ss_s, rs_s, rw_s) = _run_trunk(groups, p, 512)
    return (y_p, y_s, lc_p, lc_s, lh_p, lh_s, sc_p, sc_s, ss_p, ss_s, rs_p, rs_s, rw_p, rw_s)
```

```python
import functools
import math

import jax
import jax.numpy as jnp
from jax import lax
from jax.experimental import pallas as pl
from jax.experimental.pallas import tpu as pltpu

F32 = jnp.float32
BF16 = jnp.bfloat16

NORM_EPS = 1e-6
CONV_W = 4
LRU_BLOCKS = 8
LRU_C = 8.0
SSM_HEADDIM = 64
SSM_GROUPS = 8
SSM_STATE = 128
SSM_CHUNK = 128
SSM_NORM_EPS = 1e-5
RWKV_HEAD = 64
GN_EPS = 64e-5
PAST_LEN = 16384

V7X_LANES = 128
V7X_SUBLANES = 8
V7X_VMEM_BYTES = 64 * 1024 * 1024
VMEM_LIMIT = V7X_VMEM_BYTES - 8 * 1024 * 1024

RWKV_CHUNK = 64
CONV_PAD = V7X_SUBLANES


def _bdot(a, b):
    return jnp.dot(a.astype(BF16), b.astype(BF16), preferred_element_type=F32)


def _bdot_nt(a, b):
    return lax.dot_general(a.astype(BF16), b.astype(BF16), (((1,), (1,)), ((), ())),
                           preferred_element_type=F32)


def _bdot_tn(a, b):
    return lax.dot_general(a.astype(BF16), b.astype(BF16), (((0,), (0,)), ((), ())),
                           preferred_element_type=F32)


def _rms(x, g):
    return x * lax.rsqrt(jnp.mean(x * x, axis=-1, keepdims=True) + NORM_EPS) * g


def _sigmoid(x):
    return 0.5 * jnp.tanh(0.5 * x) + 0.5


def _softplus(x):
    return jnp.maximum(x, 0.0) + jnp.log1p(jnp.exp(-jnp.abs(x)))


def _silu(x):
    return x * _sigmoid(x)


def _gelu_tanh(x):
    c = math.sqrt(2.0 / math.pi)
    return 0.5 * x * (1.0 + jnp.tanh(c * (x + 0.044715 * (x * x * x))))


def _row_index(rows, period):
    return lax.broadcasted_iota(jnp.int32, (rows, 1), 0) & (period - 1)


def _cumsum_rows(x, t_in, period):
    d = 1
    while d < period:
        x = x + jnp.where(t_in >= d, pltpu.roll(x, d, axis=0), 0.0)
        d *= 2
    return x


def _head_sum(x, head):
    rows, width = x.shape
    lo = lax.broadcasted_iota(jnp.int32, (1, V7X_LANES), 1) < head
    parts = []
    for j in range(width // V7X_LANES):
        seg = x[:, j * V7X_LANES:(j + 1) * V7X_LANES]
        s_lo = jnp.sum(jnp.where(lo, seg, 0.0), axis=-1, keepdims=True)
        s_hi = jnp.sum(jnp.where(lo, 0.0, seg), axis=-1, keepdims=True)
        parts.append(jnp.where(lo, s_lo, s_hi))
    return jnp.concatenate(parts, axis=1)


def _const_spec(shape):
    nd = len(shape)
    return pl.BlockSpec(shape, lambda *_: (0,) * nd, pipeline_mode=pl.Buffered(1))


def _params(sem):
    return pltpu.CompilerParams(dimension_semantics=sem, vmem_limit_bytes=VMEM_LIMIT)


def _proj_kernel(x_ref, g_ref, *refs, n_out, col_chunk):
    w_refs, o_refs = refs[:n_out], refs[n_out:]
    u = _rms(x_ref[...], g_ref[...]).astype(BF16)
    for w_ref, o_ref in zip(w_refs, o_refs):
        n = w_ref.shape[1]
        step = min(col_chunk, n)
        for c0 in range(0, n, step):
            o_ref[:, c0:c0 + step] = jnp.dot(u, w_ref[:, c0:c0 + step], preferred_element_type=F32)


def _norm_proj(x, g, weights, *, tm):
    t, d = x.shape
    assert t % tm == 0
    n_out = len(weights)
    return pl.pallas_call(
        functools.partial(_proj_kernel, n_out=n_out, col_chunk=1024),
        out_shape=[jax.ShapeDtypeStruct((t, w.shape[1]), F32) for w in weights],
        grid=(t // tm,),
        in_specs=[pl.BlockSpec((tm, d), lambda i: (i, 0)), _const_spec((1, d))]
        + [_const_spec(w.shape) for w in weights],
        out_specs=[pl.BlockSpec((tm, w.shape[1]), lambda i: (i, 0)) for w in weights],
        compiler_params=_params(("parallel",)),
        name="norm_proj",
    )(x, g, *weights)


def _ffn_kernel(xa_ref, xb_ref, g_ref, w1_ref, w2_ref, gf_ref, oa_ref, ob_ref, *, n_a, f_chunk, final_norm):
    def run(x_ref, o_ref):
        x = x_ref[...]
        u = _rms(x, g_ref[...]).astype(BF16)
        acc = x
        for c0 in range(0, w1_ref.shape[1], f_chunk):
            h = jnp.dot(u, w1_ref[:, c0:c0 + f_chunk].astype(BF16), preferred_element_type=F32)
            h = jnp.square(jnp.maximum(h, 0.0)).astype(BF16)
            acc = acc + jnp.dot(h, w2_ref[c0:c0 + f_chunk, :].astype(BF16), preferred_element_type=F32)
        if final_norm:
            acc = _rms(acc, gf_ref[...])
        o_ref[...] = acc

    i = pl.program_id(0)

    @pl.when(i < n_a)
    def _():
        run(xa_ref, oa_ref)

    @pl.when(i >= n_a)
    def _():
        run(xb_ref, ob_ref)


def _ffn(xa, xb, g, w1, w2, g_final, *, layer, tm, final_norm):
    d = xa.shape[1]
    f = w1.shape[2]
    assert xa.shape[0] % tm == 0 and xb.shape[0] % tm == 0
    n_a, n_b = xa.shape[0] // tm, xb.shape[0] // tm
    a_map = lambda i: (jnp.minimum(i, n_a - 1), 0)
    b_map = lambda i: (jnp.maximum(i - n_a, 0), 0)
    layer_spec = lambda shape: pl.BlockSpec((None,) + shape, lambda i: (layer, 0, 0), pipeline_mode=pl.Buffered(1))
    return pl.pallas_call(
        functools.partial(_ffn_kernel, n_a=n_a, f_chunk=1024, final_norm=final_norm),
        out_shape=[jax.ShapeDtypeStruct(xa.shape, F32), jax.ShapeDtypeStruct(xb.shape, F32)],
        grid=(n_a + n_b,),
        in_specs=[pl.BlockSpec((tm, d), a_map), pl.BlockSpec((tm, d), b_map), _const_spec((1, d)),
                  layer_spec((d, f)), layer_spec((f, d)), _const_spec((1, d))],
        out_specs=[pl.BlockSpec((tm, d), a_map), pl.BlockSpec((tm, d), b_map)],
        compiler_params=_params(("arbitrary",)),
        name="ffn",
    )(xa, xb, g, w1, w2, g_final)


def _conv_taps(xe_ref, cw, cb, tl):
    nd = len(xe_ref.shape)

    def win(off):
        if nd == 3:
            return xe_ref[:, off:off + tl, :]
        return xe_ref[off:off + tl, :]

    base = CONV_PAD - (CONV_W - 1)
    y = cb + win(base) * cw[0:1, :]
    for k in range(1, CONV_W):
        y = y + win(base + k) * cw[k:k + 1, :]
    return y


def _lru_kernel(x_ref, g_ref, win_ref, cs_ref, h0_ref, cw_ref, cb_ref, wr_ref, br_ref, wi_ref, bi_ref, lam_ref, wo_ref,
                o_ref, cso_ref, ho_ref, xe_ref, hs_ref, hc_ref, *, nb, tl, start_pos):
    c = pl.program_id(1)
    rows = nb * tl
    d = x_ref.shape[2]
    blk = d // LRU_BLOCKS
    ncol = d // V7X_LANES
    pitch = CONV_PAD + tl
    cols = [slice(j * V7X_LANES, (j + 1) * V7X_LANES) for j in range(ncol)]

    @pl.when(c == 0)
    def _():
        for j in range(ncol):
            for s in range(nb):
                xe_ref[j, s * pitch:s * pitch + CONV_PAD, :] = cs_ref[s, :, cols[j]]
        hc_ref[...] = h0_ref[...]

    x = x_ref[...].reshape(rows, d)
    u = _rms(x, g_ref[...]).astype(BF16)
    xb = jnp.dot(u, win_ref[:, 0:d], preferred_element_type=F32)
    for j in range(ncol):
        for s in range(nb):
            xe_ref[j, s * pitch + CONV_PAD:(s + 1) * pitch, :] = xb[s * tl:(s + 1) * tl, cols[j]]

    cw = cw_ref[...]
    cb = cb_ref[...]
    base = CONV_PAD - (CONV_W - 1)
    slabs = []
    for t in range(tl):
        parts = []
        for j in range(ncol):
            acc = cb[:, cols[j]] + xe_ref[j, pl.ds(base + t, nb, stride=pitch), :] * cw[0:1, cols[j]]
            for k in range(1, CONV_W):
                acc = acc + xe_ref[j, pl.ds(base + t + k, nb, stride=pitch), :] * cw[k:k + 1, cols[j]]
            parts.append(acc)
        slabs.append(jnp.concatenate(parts, axis=1))
    xc = jnp.concatenate(slabs, axis=0)
    for j in range(ncol):
        for s in range(nb):
            tail = xe_ref[j, s * pitch + tl:(s + 1) * pitch, :]
            xe_ref[j, s * pitch:s * pitch + CONV_PAD, :] = tail
            cso_ref[s, :, cols[j]] = tail

    xcb = xc.astype(BF16)
    r_pre = jnp.concatenate(
        [jnp.dot(xcb[:, j * blk:(j + 1) * blk], wr_ref[j], preferred_element_type=F32) for j in range(LRU_BLOCKS)],
        axis=1) + br_ref[...]
    i_pre = jnp.concatenate(
        [jnp.dot(xcb[:, j * blk:(j + 1) * blk], wi_ref[j], preferred_element_type=F32) for j in range(LRU_BLOCKS)],
        axis=1) + bi_ref[...]
    r = _sigmoid(r_pre)
    gi = _sigmoid(i_pre)
    log_a = (-LRU_C) * r * _softplus(-lam_ref[...])
    a = jnp.exp(log_a)
    m2 = -jnp.tanh(log_a) * (a * a + 1.0)
    mult = jnp.where(m2 > 0.0, m2 * lax.rsqrt(m2), 0.0)
    if start_pos == 0:
        reset = (lax.broadcasted_iota(jnp.int32, (rows, 1), 0) < nb) & (c == 0)
        a = jnp.where(reset, 0.0, a)
        mult = jnp.where(reset, 1.0, mult)
    b = mult * (gi * xc)

    h = hc_ref[...]
    for t in range(tl):
        h = a[t * nb:(t + 1) * nb] * h + b[t * nb:(t + 1) * nb]
        for j in range(ncol):
            hs_ref[j, pl.ds(t, nb, stride=pitch), :] = h[:, cols[j]]
    hc_ref[...] = h
    ho_ref[...] = h
    h_bm = jnp.concatenate(
        [jnp.concatenate([hs_ref[j, s * pitch:s * pitch + tl, :] for s in range(nb)], axis=0) for j in range(ncol)],
        axis=1)

    gate = jnp.dot(u, win_ref[:, d:2 * d], preferred_element_type=F32)
    y = (h_bm * _gelu_tanh(gate)).astype(BF16)
    o_ref[...] = (x + jnp.dot(y, wo_ref[...], preferred_element_type=F32)).reshape(nb, tl, d)


def _lru_mixer(x3, g, w_in, conv_state, h0, cw, cb, wr, br, wi, bi, lam, wo, *, nb, tl, start_pos):
    batch, seq, d = x3.shape
    nchunk = seq // tl
    assert batch % nb == 0 and seq % tl == 0 and nb % V7X_SUBLANES == 0 and tl % V7X_SUBLANES == 0
    ncol = d // V7X_LANES
    blk_map = lambda i, c: (i, c, 0)
    seq_map = lambda i, c: (i, 0, 0)
    row_map = lambda i, c: (i, 0)
    return pl.pallas_call(
        functools.partial(_lru_kernel, nb=nb, tl=tl, start_pos=start_pos),
        out_shape=[jax.ShapeDtypeStruct((batch, seq, d), F32),
                   jax.ShapeDtypeStruct((batch, CONV_PAD, d), F32),
                   jax.ShapeDtypeStruct((batch, d), F32)],
        grid=(batch // nb, nchunk),
        in_specs=[pl.BlockSpec((nb, tl, d), blk_map), _const_spec(g.shape), _const_spec(w_in.shape),
                  pl.BlockSpec((nb, CONV_PAD, d), seq_map), pl.BlockSpec((nb, d), row_map),
                  _const_spec(cw.shape), _const_spec(cb.shape), _const_spec(wr.shape), _const_spec(br.shape),
                  _const_spec(wi.shape), _const_spec(bi.shape), _const_spec(lam.shape), _const_spec(wo.shape)],
        out_specs=[pl.BlockSpec((nb, tl, d), blk_map), pl.BlockSpec((nb, CONV_PAD, d), seq_map),
                   pl.BlockSpec((nb, d), row_map)],
        scratch_shapes=[pltpu.VMEM((ncol, nb * (CONV_PAD + tl), V7X_LANES), F32),
                        pltpu.VMEM((ncol, nb * (CONV_PAD + tl), V7X_LANES), F32),
                        pltpu.VMEM((nb, d), F32)],
        compiler_params=_params(("parallel", "arbitrary")),
        name="lru_mixer",
    )(x3, g, w_in, conv_state, h0, cw, cb, wr, br, wi, bi, lam, wo)


def _split_bf16(x, terms):
    parts = []
    for _ in range(terms - 1):
        hi = x.astype(BF16).astype(F32)
        parts.append(hi)
        x = x - hi
    parts.append(x)
    return parts


def _ssd_kernel(x_ref, g_ref, pz_ref, pxbc_ref, pdt_ref, cs_ref, s0_ref, cw_ref, cb_ref, dtb_ref, alog_ref, dskx_ref, nw_ref,
                ex_ref, wo_ref, o_ref, cso_ref, so_ref, xe_ref, st_ref, *, q, zero_state, fused_proj):
    c = pl.program_id(1)
    d_ssm = pz_ref.shape[1]
    n = SSM_STATE
    hpg = d_ssm // SSM_HEADDIM // SSM_GROUPS
    gw = hpg * SSM_HEADDIM
    nblk = d_ssm // n

    @pl.when(c == 0)
    def _():
        xe_ref[0:CONV_PAD, :] = cs_ref[0]
        if zero_state:
            st_ref[...] = jnp.zeros(st_ref.shape, F32)
        else:
            for j in range(nblk):
                st_ref[:, j * n:(j + 1) * n] = s0_ref[0, j * n:(j + 1) * n, :].T

    x = x_ref[...]
    if fused_proj:
        u = _rms(x, g_ref[...]).astype(BF16)
        for c0 in range(0, pxbc_ref.shape[1], 1024):
            xe_ref[CONV_PAD:CONV_PAD + q, c0:c0 + 1024] = jnp.dot(u, pxbc_ref[:, c0:c0 + 1024],
                                                                  preferred_element_type=F32)
        dt_pre = jnp.dot(u, pdt_ref[...], preferred_element_type=F32)
    else:
        xe_ref[CONV_PAD:CONV_PAD + q, :] = pxbc_ref[...]
        dt_pre = pdt_ref[...]
    xbc = _silu(_conv_taps(xe_ref, cw_ref[...], cb_ref[...], q))
    tail = xe_ref[q:q + CONV_PAD, :]
    xe_ref[0:CONV_PAD, :] = tail
    cso_ref[0] = tail

    xs = xbc[:, 0:d_ssm]
    bm = xbc[:, d_ssm:d_ssm + SSM_GROUPS * n]
    cm = xbc[:, d_ssm + SSM_GROUPS * n:]
    dt = _softplus(dt_pre + dtb_ref[...])
    da = dt * (-jnp.exp(alog_ref[...]))
    t_in = _row_index(q, q)
    acs = _cumsum_rows(da, t_in, q)
    if q < V7X_LANES:
        acs_sq = jnp.concatenate([acs, jnp.zeros((V7X_LANES - q, V7X_LANES), F32)], axis=0)
    else:
        acs_sq = acs
    acs_t = acs_sq.T

    lhs = jnp.concatenate(_split_bf16(dt, 2) + _split_bf16(acs, 3), axis=0).astype(BF16)
    ex = jnp.dot(lhs, ex_ref[...], preferred_element_type=F32)
    dt_x = ex[0:q] + ex[q:2 * q]
    acs_x = ex[2 * q:3 * q] + ex[3 * q:4 * q] + ex[4 * q:5 * q]
    last_x = acs_x[q - 1:q, :]
    xdt = (xs * dt_x).astype(BF16)
    xdtd = (xs * (dt_x * jnp.exp(last_x - acs_x))).astype(BF16)

    tri = lax.broadcasted_iota(jnp.int32, (q, q), 0) >= lax.broadcasted_iota(jnp.int32, (q, q), 1)
    lane_head = lax.broadcasted_iota(jnp.int32, (1, gw), 1) // SSM_HEADDIM
    groups = range(SSM_GROUPS)
    bgs = [bm[:, g * n:(g + 1) * n].astype(BF16) for g in groups]
    cgs = [cm[:, g * n:(g + 1) * n].astype(BF16) for g in groups]
    cb_mats = [_bdot_nt(cgs[g], bgs[g]) for g in groups]
    y_off = [_bdot(cgs[g], st_ref[:, g * gw:(g + 1) * gw]) for g in groups]
    y_diag = []
    for g in groups:
        xdt_g = xdt[:, g * gw:(g + 1) * gw]
        acc = None
        for e in range(hpg):
            h = g * hpg + e
            seg = acs[:, h:h + 1] - acs_t[h:h + 1, 0:q]
            l_mat = jnp.exp(jnp.where(tri, seg, -jnp.inf))
            part = jnp.dot((cb_mats[g] * l_mat).astype(BF16), jnp.where(lane_head == e, xdt_g, jnp.zeros_like(xdt_g)),
                           preferred_element_type=F32)
            acc = part if acc is None else acc + part
        y_diag.append(acc)
    st_new = [_bdot_tn(bgs[g], xdtd[:, g * gw:(g + 1) * gw]) for g in groups]
    e_last_x = jnp.exp(last_x)
    for g in groups:
        cols = slice(g * gw, (g + 1) * gw)
        st_ref[:, cols] = st_ref[:, cols] * e_last_x[:, cols] + st_new[g]

    @pl.when(c == pl.num_programs(1) - 1)
    def _():
        for j in range(nblk):
            so_ref[0, j * n:(j + 1) * n, :] = st_ref[:, j * n:(j + 1) * n].T

    y = jnp.concatenate(y_diag, axis=1) + jnp.concatenate(y_off, axis=1) * jnp.exp(acs_x) + dskx_ref[...] * xs
    y = y * _silu(jnp.dot(u, pz_ref[...], preferred_element_type=F32) if fused_proj else pz_ref[...])
    norm_parts = []
    for g in groups:
        yg = y[:, g * gw:(g + 1) * gw]
        norm_parts.append(yg * lax.rsqrt(jnp.mean(yg * yg, axis=-1, keepdims=True) + SSM_NORM_EPS))
    yn = (jnp.concatenate(norm_parts, axis=1) * nw_ref[...]).astype(BF16)
    o_ref[...] = x + jnp.dot(yn, wo_ref[...], preferred_element_type=F32)


def _ssd_mixer(x, g, w_z, w_xbc, w_dt, conv_state, s0, cw, cb, dtb, alog, dskx, nw, expand, wo, *, batch, seq, q,
               fused_proj):
    d = x.shape[1]
    d_ssm = w_z.shape[1]
    cdim = w_xbc.shape[1]
    nchunk = seq // q
    assert seq % q == 0
    zero_state = s0 is None
    if zero_state:
        s0 = jnp.zeros((1, d_ssm, SSM_STATE), F32)
    row_map = lambda b, c: (b * nchunk + c, 0)
    seq_map = lambda b, c: (b, 0, 0)
    s0_map = (lambda b, c: (0, 0, 0)) if zero_state else seq_map
    if fused_proj:
        proj = (w_z, w_xbc, w_dt)
        proj_specs = [_const_spec(w.shape) for w in proj]
    else:
        proj = _norm_proj(x, g, [w_z, w_xbc, w_dt], tm=min(256, batch * seq))
        proj_specs = [pl.BlockSpec((q, a.shape[1]), row_map) for a in proj]
    return pl.pallas_call(
        functools.partial(_ssd_kernel, q=q, zero_state=zero_state, fused_proj=fused_proj),
        out_shape=[jax.ShapeDtypeStruct((batch * seq, d), F32),
                   jax.ShapeDtypeStruct((batch, CONV_PAD, cdim), F32),
                   jax.ShapeDtypeStruct((batch, d_ssm, SSM_STATE), F32)],
        grid=(batch, nchunk),
        in_specs=[pl.BlockSpec((q, d), row_map), _const_spec(g.shape)] + proj_specs
        + [pl.BlockSpec((1, CONV_PAD, cdim), seq_map), pl.BlockSpec((1, d_ssm, SSM_STATE), s0_map),
                  _const_spec(cw.shape), _const_spec(cb.shape), _const_spec(dtb.shape), _const_spec(alog.shape),
                  _const_spec(dskx.shape), _const_spec(nw.shape), _const_spec(expand.shape), _const_spec(wo.shape)],
        out_specs=[pl.BlockSpec((q, d), row_map), pl.BlockSpec((1, CONV_PAD, cdim), seq_map),
                   pl.BlockSpec((1, d_ssm, SSM_STATE), seq_map)],
        scratch_shapes=[pltpu.VMEM((CONV_PAD + q, cdim), F32), pltpu.VMEM((SSM_STATE, d_ssm), F32)],
        compiler_params=_params(("parallel", "arbitrary")),
        name="ssd_mixer",
    )(x, g, *proj, conv_state, s0, cw, cb, dtb, alog, dskx, nw, expand, wo)


def _rwkv_proj_kernel(x_ref, sh_ref, g_ref, mu_ref, wrkv_ref, w0_ref, w1_ref, w2_ref, a0_ref, a1_ref, a2_ref,
                      g1_ref, g2_ref, kkw_ref, kaw_ref,
                      r_o, k_o, v_o, kk_o, bb_o, lw_o, g_o, sho_ref, carry_ref, *, nb, tl, time_major):
    c = pl.program_id(1)
    rows = x_ref.shape[0]
    d = x_ref.shape[1]

    @pl.when(c == 0)
    def _():
        carry_ref[...] = sh_ref[...]

    u = _rms(x_ref[...], g_ref[...])
    if time_major:
        prev = carry_ref[...]
        last = u
    else:
        t_in = _row_index(rows, tl)
        prev = jnp.where(t_in == 0, jnp.broadcast_to(carry_ref[...], (nb, tl, d)).reshape(rows, d),
                         pltpu.roll(u, 1, axis=0))
        last = u.reshape(nb, tl, d)[:, tl - 1:tl, :]
    carry_ref[...] = last
    sho_ref[...] = last

    diff = prev - u
    mu = mu_ref[...]

    def mixed(s):
        return (u + diff * mu[s:s + 1, :]).astype(BF16)

    r = jnp.dot(mixed(0), wrkv_ref[0], preferred_element_type=F32)
    k = jnp.dot(mixed(1), wrkv_ref[1], preferred_element_type=F32)
    v_o[...] = jnp.dot(mixed(2), wrkv_ref[2], preferred_element_type=F32)
    w_pre = w0_ref[...] + _bdot(jnp.tanh(jnp.dot(mixed(3), w1_ref[...], preferred_element_type=F32)), w2_ref[...])
    sp = jnp.maximum(-w_pre, 0.0) + jnp.log(1.0 + jnp.exp(-jnp.abs(w_pre)))
    lw_o[...] = -jnp.exp(-sp - 0.5)
    a = _sigmoid(a0_ref[...] + _bdot(jnp.dot(mixed(4), a1_ref[...], preferred_element_type=F32), a2_ref[...]))
    g_o[...] = _bdot(_sigmoid(jnp.dot(mixed(5), g1_ref[...], preferred_element_type=F32)), g2_ref[...])
    kk = k * kkw_ref[...]
    kk = kk * jnp.minimum(lax.rsqrt(_head_sum(kk * kk, RWKV_HEAD)), 1e12)
    r_o[...] = r
    k_o[...] = k * (1.0 + (a - 1.0) * kaw_ref[...])
    kk_o[...] = kk
    bb_o[...] = kk * a


def _rwkv_proj(x, shift0, g, mu, wrkv, w0, w1, w2, a0, a1, a2, g1, g2, kkw, kaw, *, batch, seq, nb, tl, time_major):
    d = x.shape[1]
    consts = (g, mu, wrkv, w0, w1, w2, a0, a1, a2, g1, g2, kkw, kaw)
    if time_major:
        rows, grid = batch, (1, seq)
        row_map = lambda i, c: (c, 0)
        sh_shape, sh_block, sh_map = (batch, d), (batch, d), (lambda i, c: (0, 0))
    else:
        nchunk = seq // tl
        assert batch % nb == 0 and seq % tl == 0 and (nb == 1 or nchunk == 1)
        rows, grid = nb * tl, (batch // nb, nchunk)
        row_map = lambda i, c: (i * nchunk + c, 0)
        sh_shape, sh_block, sh_map = (batch, 1, d), (nb, 1, d), (lambda i, c: (i, 0, 0))
    return pl.pallas_call(
        functools.partial(_rwkv_proj_kernel, nb=nb, tl=tl, time_major=time_major),
        out_shape=[jax.ShapeDtypeStruct((batch * seq, d), F32)] * 7 + [jax.ShapeDtypeStruct(sh_shape, F32)],
        grid=grid,
        in_specs=[pl.BlockSpec((rows, d), row_map), pl.BlockSpec(sh_block, sh_map)]
        + [_const_spec(w.shape) for w in consts],
        out_specs=[pl.BlockSpec((rows, d), row_map)] * 7 + [pl.BlockSpec(sh_block, sh_map)],
        scratch_shapes=[pltpu.VMEM(sh_block, F32)],
        compiler_params=_params(("parallel", "arbitrary")),
        name="rwkv_proj",
    )(x, shift0, *consts)


def _rwkv_chunk_kernel(r_ref, k_ref, v_ref, kk_ref, bb_ref, lw_ref, g_ref, x_ref, s0_ref, rk_ref, lnw_ref, lnb_ref,
                       wo_ref, o_ref, so_ref, s_ref, og_ref, *, nb, tl, chunk, zero_state):
    c = pl.program_id(1)
    d = x_ref.shape[1]
    hd = RWKV_HEAD
    heads = d // hd
    cpb = tl // chunk
    n_iter = nb * cpb

    @pl.when(c == 0)
    def _():
        if zero_state:
            s_ref[...] = jnp.zeros(s_ref.shape, F32)
        else:
            s_ref[...] = s0_ref[...]

    t_in = _row_index(chunk, chunk)
    ri = lax.broadcasted_iota(jnp.int32, (2 * chunk, 2 * chunk), 0)
    ci = lax.broadcasted_iota(jnp.int32, (2 * chunk, 2 * chunk), 1) & (chunk - 1)
    blk_mask = jnp.where(ri < chunk, ri - 1, ri - chunk) >= ci

    def body(it, carry):
        seq_i = it if cpb == 1 else 0
        rows = pl.ds(pl.multiple_of(it * chunk, chunk), chunk)
        r = r_ref[rows, :]
        k = k_ref[rows, :]
        v = v_ref[rows, :]
        kk = kk_ref[rows, :]
        bb = bb_ref[rows, :]
        lw = lw_ref[rows, :]
        cum = _cumsum_rows(lw, t_in, chunk)
        c_last = cum[chunk - 1:chunk, :]
        g_last = jnp.exp(c_last)
        e_inv = jnp.exp(-cum)
        e_end = jnp.exp(c_last - cum)
        p_st = jnp.concatenate([kk * jnp.exp(cum - lw), r * jnp.exp(cum)], axis=0).astype(BF16)
        q_st = jnp.concatenate([bb * e_inv, k * e_inv], axis=0).astype(BF16)
        e_st = jnp.concatenate([k * e_end, bb * e_end], axis=0).astype(BF16)
        zv_st = jnp.concatenate([jnp.zeros_like(v), v], axis=0).astype(BF16)
        hs = range(heads)
        sls = [slice(h * hd, (h + 1) * hd) for h in hs]
        s_l = [s_ref[seq_i, sl, :] for sl in sls]
        g_m = [jnp.where(blk_mask, _bdot_nt(p_st[:, sl], q_st[:, sl]), 0.0) for sl in sls]
        ps = [_bdot_nt(p_st[:, sls[h]], s_l[h]) for h in hs]
        t1 = [_bdot(g_m[h], zv_st[:, sls[h]]) for h in hs]
        nmat = [-g_m[h][0:chunk, 0:chunk] for h in hs]
        w_l = [ps[h][0:chunk] + t1[h][0:chunk] for h in hs]
        u_l = [w_l[h] + _bdot(nmat[h], w_l[h]) for h in hs]
        m = 2
        while m < chunk:
            nmat = [_bdot(x, x) for x in nmat]
            u_l = [u_l[h] + _bdot(nmat[h], u_l[h]) for h in hs]
            m *= 2
        o_parts = [ps[h][chunk:] + t1[h][chunk:] - _bdot(g_m[h][chunk:, 0:chunk], u_l[h]) for h in hs]
        for h in hs:
            vu = jnp.concatenate([v[:, sls[h]], -u_l[h]], axis=0)
            s_ref[seq_i, sls[h], :] = s_l[h] * g_last[:, sls[h]] + _bdot_tn(vu, e_st[:, sls[h]])
        o = jnp.concatenate(o_parts, axis=1)
        mean = _head_sum(o, hd) * (1.0 / hd)
        oc = o - mean
        var = _head_sum(oc * oc, hd) * (1.0 / hd)
        on = oc * lax.rsqrt(var + GN_EPS) * lnw_ref[...] + lnb_ref[...]
        bonus = _head_sum(r * k * rk_ref[...], hd) * v
        og_ref[rows, :] = (on + bonus) * g_ref[rows, :]
        return carry

    lax.fori_loop(0, n_iter, body, 0)

    @pl.when(c == pl.num_programs(1) - 1)
    def _():
        so_ref[...] = s_ref[...]

    o_ref[...] = x_ref[...] + _bdot(og_ref[...], wo_ref[...])


def _rwkv_mixer(x, r, k, v, kk, bb, lw, g, s0, rk, lnw, lnb, wo, *, batch, seq, nb, tl, chunk):
    d = x.shape[1]
    nchunk = seq // tl
    assert batch % nb == 0 and seq % tl == 0 and tl % chunk == 0 and (nb == 1 or tl == chunk)
    rows = nb * tl
    zero_state = s0 is None
    if zero_state:
        s0 = jnp.zeros((nb, d, RWKV_HEAD), F32)
    row_map = lambda i, c: (i * nchunk + c, 0)
    seq_map = lambda i, c: (i, 0, 0)
    s0_map = (lambda i, c: (0, 0, 0)) if zero_state else seq_map
    row_spec = pl.BlockSpec((rows, d), row_map)
    return pl.pallas_call(
        functools.partial(_rwkv_chunk_kernel, nb=nb, tl=tl, chunk=chunk, zero_state=zero_state),
        out_shape=[jax.ShapeDtypeStruct((batch * seq, d), F32), jax.ShapeDtypeStruct((batch, d, RWKV_HEAD), F32)],
        grid=(batch // nb, nchunk),
        in_specs=[row_spec] * 8 + [pl.BlockSpec((nb, d, RWKV_HEAD), s0_map),
                                   _const_spec(rk.shape), _const_spec(lnw.shape), _const_spec(lnb.shape),
                                   _const_spec(wo.shape)],
        out_specs=[row_spec, pl.BlockSpec((nb, d, RWKV_HEAD), seq_map)],
        scratch_shapes=[pltpu.VMEM((nb, d, RWKV_HEAD), F32), pltpu.VMEM((rows, d), F32)],
        compiler_params=_params(("parallel", "arbitrary")),
        name="rwkv_chunk",
    )(r, k, v, kk, bb, lw, g, x, s0, rk, lnw, lnb, wo)


def _rwkv_seq_kernel(r_ref, k_ref, v_ref, kk_ref, bb_ref, lw_ref, s0_ref, o_ref, so_ref, xt_ref, ot_ref, *, seq, batch):
    hd = RWKV_HEAD
    for i, ref in enumerate((r_ref, k_ref, v_ref, kk_ref, bb_ref, lw_ref)):
        for t in range(seq):
            tile = ref[t * batch:(t + 1) * batch, :]
            if ref is lw_ref:
                tile = jnp.exp(tile)
            xt_ref[i * seq + t] = tile.T
    sub = V7X_SUBLANES
    for hh in range(2):
        base = hh * hd
        for t in range(seq):
            src = s0_ref if t == 0 else so_ref
            r_t, k_t, kk_t, bb_t, w_t = (xt_ref[i * seq + t, base:base + hd, :] for i in (0, 1, 3, 4, 5))

            def body(vg, carry, src=src, t=t, base=base, hh=hh, r_t=r_t, k_t=k_t, kk_t=kk_t, bb_t=bb_t, w_t=w_t):
                v0 = pl.multiple_of(base + vg * sub, sub)
                v_tile = xt_ref[2 * seq + t, pl.ds(v0, sub), :]
                rows = []
                for i in range(sub):
                    off = pl.multiple_of((vg * sub + i) * hd, hd)
                    sv = src[hh, pl.ds(off, hd), :]
                    sa = jnp.sum(sv * kk_t, axis=0, keepdims=True)
                    sv = sv * w_t - sa * bb_t + v_tile[i:i + 1, :] * k_t
                    so_ref[hh, pl.ds(off, hd), :] = sv
                    rows.append(jnp.sum(sv * r_t, axis=0, keepdims=True))
                ot_ref[t, pl.ds(v0, sub), :] = jnp.concatenate(rows, axis=0)
                return carry

            lax.fori_loop(0, hd // sub, body, 0)
    for t in range(seq):
        o_ref[t * batch:(t + 1) * batch, :] = ot_ref[t].T


def _rwkv_seq(r, k, v, kk, bb, lw, s0, *, batch, seq):
    d = r.shape[1]
    assert batch == V7X_LANES and d % V7X_LANES == 0
    npair = d // V7X_LANES
    col_spec = pl.BlockSpec((seq * batch, V7X_LANES), lambda j: (0, j))
    st_spec = pl.BlockSpec((2, RWKV_HEAD * RWKV_HEAD, batch), lambda j: (j, 0, 0))
    return pl.pallas_call(
        functools.partial(_rwkv_seq_kernel, seq=seq, batch=batch),
        out_shape=[jax.ShapeDtypeStruct((seq * batch, d), F32), jax.ShapeDtypeStruct(s0.shape, F32)],
        grid=(npair,),
        in_specs=[col_spec] * 6 + [st_spec],
        out_specs=[col_spec, st_spec],
        scratch_shapes=[pltpu.VMEM((6 * seq, V7X_LANES, batch), F32), pltpu.VMEM((seq, V7X_LANES, batch), F32)],
        compiler_params=_params(("parallel",)),
        name="rwkv_seq",
    )(r, k, v, kk, bb, lw, s0)


def _rwkv_finish_kernel(o_ref, r_ref, k_ref, v_ref, g_ref, x_ref, rk_ref, lnw_ref, lnb_ref, wo_ref, out_ref):
    hd = RWKV_HEAD
    o = o_ref[...]
    mean = _head_sum(o, hd) * (1.0 / hd)
    oc = o - mean
    var = _head_sum(oc * oc, hd) * (1.0 / hd)
    on = oc * lax.rsqrt(var + GN_EPS) * lnw_ref[...] + lnb_ref[...]
    bonus = _head_sum(r_ref[...] * k_ref[...] * rk_ref[...], hd) * v_ref[...]
    out_ref[...] = x_ref[...] + _bdot((on + bonus) * g_ref[...], wo_ref[...])


def _rwkv_finish(o, r, k, v, g, x, rk, lnw, lnb, wo, *, tm):
    t, d = x.shape
    assert t % tm == 0
    row_spec = pl.BlockSpec((tm, d), lambda i: (i, 0))
    return pl.pallas_call(
        _rwkv_finish_kernel,
        out_shape=jax.ShapeDtypeStruct((t, d), F32),
        grid=(t // tm,),
        in_specs=[row_spec] * 6 + [_const_spec(rk.shape), _const_spec(lnw.shape), _const_spec(lnb.shape),
                                   _const_spec(wo.shape)],
        out_specs=row_spec,
        compiler_params=_params(("parallel",)),
        name="rwkv_finish",
    )(o, r, k, v, g, x, rk, lnw, lnb, wo)


def _pad_conv_state(state):
    return jnp.pad(state, ((0, 0), (CONV_PAD - (CONV_W - 1), 0), (0, 0)))


class _Group:
    def __init__(self, x3, states, start_pos, cfg):
        self.batch, self.seq, d = x3.shape
        self.x = x3.reshape(self.batch * self.seq, d)
        self.states, self.start_pos, self.cfg = states, start_pos, cfg
        self.out = dict(lru_conv=[], lru_h=[], ssm_conv=[], ssm=[], rwkv_shift=[], rwkv_wkv=[])

    def state(self, name, idx, zero_shape):
        return self.states[name][idx] if self.states else jnp.zeros(zero_shape, F32)


def _mixer_layer(grp, layer, idx, p):
    batch, seq, cfg = grp.batch, grp.seq, grp.cfg
    x = grp.x
    d = x.shape[1]
    g_mix = p["norm_mix"][layer][None, :]
    kind = layer % 3
    if kind == 0:
        cs = grp.state("lru_conv", idx, (batch, CONV_W - 1, d))
        h0 = grp.state("lru_h", idx, (batch, d))
        x3, cso, ho = _lru_mixer(
            x.reshape(batch, seq, d), g_mix, p["lru_w_in"][idx], _pad_conv_state(cs), h0,
            p["lru_conv_w"][idx], p["lru_conv_b"][idx][None, :], p["lru_w_r"][idx], p["lru_b_r"][idx][None, :],
            p["lru_w_i"][idx], p["lru_b_i"][idx][None, :], p["lru_lambda"][idx][None, :], p["lru_w_out"][idx],
            nb=cfg["lru_nb"], tl=cfg["lru_tl"], start_pos=grp.start_pos)
        grp.x = x3.reshape(batch * seq, d)
        grp.out["lru_conv"].append(cso[:, CONV_PAD - (CONV_W - 1):, :])
        grp.out["lru_h"].append(ho)
    elif kind == 1:
        cdim = p["ssm_w_xbc"].shape[2]
        cs = grp.state("ssm_conv", idx, (batch, CONV_W - 1, cdim))
        s0 = grp.states["ssm"][idx].reshape(batch, -1, SSM_STATE) if grp.states else None
        grp.x, cso, so = _ssd_mixer(
            x, g_mix, p["ssm_w_z"][idx], p["ssm_w_xbc"][idx], p["ssm_w_dt"][idx], _pad_conv_state(cs), s0,
            p["ssm_conv_w"][idx], p["ssm_conv_b"][idx][None, :], p["ssm_dt_bias"][idx], p["ssm_a_log"][idx],
            p["ssm_d_x"][idx], p["ssm_norm_w"][idx][None, :], p["ssm_expand"], p["ssm_w_out"][idx],
            batch=batch, seq=seq, q=cfg["ssd_q"], fused_proj=cfg["ssd_fused"])
        grp.out["ssm_conv"].append(cso[:, CONV_PAD - (CONV_W - 1):, :])
        grp.out["ssm"].append(so.reshape(batch, -1, SSM_HEADDIM, SSM_STATE))
    elif cfg["rwkv_stepwise"]:
        heads = d // RWKV_HEAD
        proj_w = (g_mix, p["rwkv_mu"][idx], p["rwkv_w_rkv"][idx], p["rwkv_w0"][idx][None, :],
                  p["rwkv_w_w1"][idx], p["rwkv_w_w2"][idx], p["rwkv_a0"][idx][None, :], p["rwkv_w_a1"][idx],
                  p["rwkv_w_a2"][idx], p["rwkv_w_g1"][idx], p["rwkv_w_g2"][idx], p["rwkv_k_k"][idx][None, :],
                  p["rwkv_k_a"][idx][None, :])
        x_tm = x.reshape(batch, seq, d).transpose(1, 0, 2).reshape(seq * batch, d)
        r, k, v, kk, bb, lw, g, sho = _rwkv_proj(x_tm, grp.state("rwkv_shift", idx, (batch, d)), *proj_w,
                                                 batch=batch, seq=seq, nb=1, tl=1, time_major=True)
        s0 = grp.state("rwkv_wkv", idx, (batch, heads, RWKV_HEAD, RWKV_HEAD))
        s0 = s0.transpose(1, 2, 3, 0).reshape(heads, RWKV_HEAD * RWKV_HEAD, batch)
        o, so = _rwkv_seq(r, k, v, kk, bb, lw, s0, batch=batch, seq=seq)
        x_tm = _rwkv_finish(o, r, k, v, g, x_tm, p["rwkv_r_k"][idx].reshape(1, d), p["rwkv_lnx_w"][idx][None, :],
                            p["rwkv_lnx_b"][idx][None, :], p["rwkv_w_out"][idx], tm=min(256, seq * batch))
        grp.x = x_tm.reshape(seq, batch, d).transpose(1, 0, 2).reshape(batch * seq, d)
        grp.out["rwkv_shift"].append(sho)
        grp.out["rwkv_wkv"].append(so.reshape(heads, RWKV_HEAD, RWKV_HEAD, batch).transpose(3, 0, 1, 2))
    else:
        sh0 = grp.state("rwkv_shift", idx, (batch, d))
        s0 = grp.states["rwkv_wkv"][idx].reshape(batch, d, RWKV_HEAD) if grp.states else None
        r, k, v, kk, bb, lw, g, sho = _rwkv_proj(
            x, sh0[:, None, :], g_mix, p["rwkv_mu"][idx], p["rwkv_w_rkv"][idx], p["rwkv_w0"][idx][None, :],
            p["rwkv_w_w1"][idx], p["rwkv_w_w2"][idx], p["rwkv_a0"][idx][None, :], p["rwkv_w_a1"][idx],
            p["rwkv_w_a2"][idx], p["rwkv_w_g1"][idx], p["rwkv_w_g2"][idx], p["rwkv_k_k"][idx][None, :],
            p["rwkv_k_a"][idx][None, :], batch=batch, seq=seq, nb=cfg["rwkv_proj_nb"], tl=cfg["rwkv_tl"],
            time_major=False)
        grp.x, so = _rwkv_mixer(
            x, r, k, v, kk, bb, lw, g, s0, p["rwkv_r_k"][idx].reshape(1, d), p["rwkv_lnx_w"][idx][None, :],
            p["rwkv_lnx_b"][idx][None, :], p["rwkv_w_out"][idx],
            batch=batch, seq=seq, nb=cfg["rwkv_nb"], tl=cfg["rwkv_tl"], chunk=cfg["rwkv_chunk"])
        grp.out["rwkv_shift"].append(sho[:, 0, :])
        grp.out["rwkv_wkv"].append(so.reshape(batch, d // RWKV_HEAD, RWKV_HEAD, RWKV_HEAD))


def _run_trunk(groups, p, tm_ffn):
    depth = p["norm_mix"].shape[0]
    counts = [0, 0, 0]
    for layer in range(depth):
        kind = layer % 3
        for grp in groups:
            _mixer_layer(grp, layer, counts[kind], p)
        counts[kind] += 1
        groups[0].x, groups[1].x = _ffn(
            groups[0].x, groups[1].x, p["norm_ffn"][layer][None, :], p["ffn_w1"], p["ffn_w2"],
            p["norm_final"][None, :], layer=layer, tm=tm_ffn, final_norm=(layer == depth - 1))
    stack = lambda parts: parts[0][None] if len(parts) == 1 else jnp.stack(parts)
    results = []
    for grp in groups:
        d = grp.x.shape[1]
        results.append([grp.x.reshape(grp.batch, grp.seq, d)]
                       + [stack(grp.out[name]) for name in ("lru_conv", "lru_h", "ssm_conv", "ssm", "rwkv_shift", "rwkv_wkv")])
    return results


def _prep_params(p):
    q = dict(p)
    for name in ("lru_w_in", "lru_w_r", "lru_w_i", "lru_w_out", "ssm_w_out", "rwkv_w_rkv", "rwkv_w_w1", "rwkv_w_w2",
                 "rwkv_w_a1", "rwkv_w_a2", "rwkv_w_g1", "rwkv_w_g2", "rwkv_w_out"):
        q[name] = p[name].astype(BF16)
    d_ssm = p["ssm_w_out"].shape[1]
    heads = p["ssm_dt_bias"].shape[1]
    w_in = p["ssm_w_in"]
    cdim = w_in.shape[2] - d_ssm - heads
    pad = V7X_LANES - heads
    q["ssm_w_z"] = w_in[:, :, :d_ssm].astype(BF16)
    q["ssm_w_xbc"] = w_in[:, :, d_ssm:d_ssm + cdim].astype(BF16)
    q["ssm_w_dt"] = jnp.pad(w_in[:, :, d_ssm + cdim:], ((0, 0), (0, 0), (0, pad))).astype(BF16)
    for name in ("ssm_dt_bias", "ssm_a_log"):
        q[name] = jnp.pad(p[name], ((0, 0), (0, pad)))[:, None, :]
    q["ssm_d_x"] = jnp.repeat(p["ssm_d"], SSM_HEADDIM, axis=1)[:, None, :]
    q["ssm_expand"] = (jnp.arange(V7X_LANES)[:, None] == jnp.arange(d_ssm)[None, :] // SSM_HEADDIM).astype(BF16)
    return q


def kernel(x_prompt, x_sample, state_lru_conv, state_lru_h, state_ssm_conv, state_ssm, state_rwkv_shift, state_rwkv_wkv, norm_mix, norm_ffn, norm_final, lru_w_in, lru_conv_w, lru_conv_b, lru_w_r, lru_b_r, lru_w_i, lru_b_i, lru_lambda, lru_w_out, ssm_w_in, ssm_conv_w, ssm_conv_b, ssm_dt_bias, ssm_a_log, ssm_d, ssm_norm_w, ssm_w_out, rwkv_mu, rwkv_w_rkv, rwkv_w0, rwkv_w_w1, rwkv_w_w2, rwkv_a0, rwkv_w_a1, rwkv_w_a2, rwkv_w_g1, rwkv_w_g2, rwkv_k_k, rwkv_k_a, rwkv_r_k, rwkv_lnx_w, rwkv_lnx_b, rwkv_w_out, ffn_w1, ffn_w2):
    p = _prep_params(dict(
        norm_mix=norm_mix, norm_ffn=norm_ffn, norm_final=norm_final,
        lru_w_in=lru_w_in, lru_conv_w=lru_conv_w, lru_conv_b=lru_conv_b, lru_w_r=lru_w_r, lru_b_r=lru_b_r,
        lru_w_i=lru_w_i, lru_b_i=lru_b_i, lru_lambda=lru_lambda, lru_w_out=lru_w_out,
        ssm_w_in=ssm_w_in, ssm_conv_w=ssm_conv_w, ssm_conv_b=ssm_conv_b, ssm_dt_bias=ssm_dt_bias,
        ssm_a_log=ssm_a_log, ssm_d=ssm_d, ssm_norm_w=ssm_norm_w, ssm_w_out=ssm_w_out,
        rwkv_mu=rwkv_mu, rwkv_w_rkv=rwkv_w_rkv, rwkv_w0=rwkv_w0, rwkv_w_w1=rwkv_w_w1, rwkv_w_w2=rwkv_w_w2,
        rwkv_a0=rwkv_a0, rwkv_w_a1=rwkv_w_a1, rwkv_w_a2=rwkv_w_a2, rwkv_w_g1=rwkv_w_g1, rwkv_w_g2=rwkv_w_g2,
        rwkv_k_k=rwkv_k_k, rwkv_k_a=rwkv_k_a, rwkv_r_k=rwkv_r_k, rwkv_lnx_w=rwkv_lnx_w, rwkv_lnx_b=rwkv_lnx_b,
        rwkv_w_out=rwkv_w_out, ffn_w1=ffn_w1, ffn_w2=ffn_w2))
    seq_p = x_prompt.shape[1]
    seq_s = x_sample.shape[1]
    cfg_p = dict(lru_nb=8, lru_tl=128, ssd_q=min(SSM_CHUNK, seq_p), ssd_fused=True,
                 rwkv_stepwise=False, rwkv_proj_nb=1, rwkv_nb=1, rwkv_tl=512, rwkv_chunk=RWKV_CHUNK)
    cfg_s = dict(lru_nb=32, lru_tl=seq_s, ssd_q=seq_s, ssd_fused=False,
                 rwkv_stepwise=True)
    states = dict(lru_conv=state_lru_conv, lru_h=state_lru_h, ssm_conv=state_ssm_conv, ssm=state_ssm,
                  rwkv_shift=state_rwkv_shift, rwkv_wkv=state_rwkv_wkv)
    groups = [_Group(x_prompt, None, 0, cfg_p), _Group(x_sample, states, PAST_LEN, cfg_s)]
    (y_p, lc_p, lh_p, sc_p, ss_p, rs_p, rw_p), (y_s, lc_s, lh_s, sc_s, ss_s, rs_s, rw_s) = _run_trunk(groups, p, 512)
    return (y_p, y_s, lc_p, lc_s, lh_p, lh_s, sc_p, sc_s, ss_p, ss_s, rs_p, rs_s, rw_p, rw_s)
```

```python
import functools
import math

import jax
import jax.numpy as jnp
from jax import lax
from jax.experimental import pallas as pl
from jax.experimental.pallas import tpu as pltpu

F32 = jnp.float32
BF16 = jnp.bfloat16

NORM_EPS = 1e-6
CONV_W = 4
LRU_BLOCKS = 8
LRU_C = 8.0
SSM_HEADDIM = 64
SSM_GROUPS = 8
SSM_STATE = 128
SSM_CHUNK = 128
SSM_NORM_EPS = 1e-5
RWKV_HEAD = 64
GN_EPS = 64e-5
PAST_LEN = 16384

V7X_LANES = 128
V7X_SUBLANES = 8
V7X_VMEM_BYTES = 64 * 1024 * 1024
VMEM_LIMIT = V7X_VMEM_BYTES - 8 * 1024 * 1024

RWKV_CHUNK = 64
CONV_PAD = V7X_SUBLANES


def _bdot(a, b):
    return jnp.dot(a.astype(BF16), b.astype(BF16), preferred_element_type=F32)


def _bdot_nt(a, b):
    return lax.dot_general(a.astype(BF16), b.astype(BF16), (((1,), (1,)), ((), ())),
                           preferred_element_type=F32)


def _bdot_tn(a, b):
    return lax.dot_general(a.astype(BF16), b.astype(BF16), (((0,), (0,)), ((), ())),
                           preferred_element_type=F32)


def _rms(x, g):
    return x * lax.rsqrt(jnp.mean(x * x, axis=-1, keepdims=True) + NORM_EPS) * g


def _sigmoid(x):
    return 0.5 * jnp.tanh(0.5 * x) + 0.5


def _softplus(x):
    return jnp.maximum(x, 0.0) + jnp.log1p(jnp.exp(-jnp.abs(x)))


def _silu(x):
    return x * _sigmoid(x)


def _gelu_tanh(x):
    c = math.sqrt(2.0 / math.pi)
    return 0.5 * x * (1.0 + jnp.tanh(c * (x + 0.044715 * (x * x * x))))


def _row_index(rows, period):
    return lax.broadcasted_iota(jnp.int32, (rows, 1), 0) & (period - 1)


def _cumsum_rows(x, t_in, period):
    d = 1
    while d < period:
        x = x + jnp.where(t_in >= d, pltpu.roll(x, d, axis=0), 0.0)
        d *= 2
    return x


def _head_sum(x, head):
    rows, width = x.shape
    lo = lax.broadcasted_iota(jnp.int32, (1, V7X_LANES), 1) < head
    parts = []
    for j in range(width // V7X_LANES):
        seg = x[:, j * V7X_LANES:(j + 1) * V7X_LANES]
        s_lo = jnp.sum(jnp.where(lo, seg, 0.0), axis=-1, keepdims=True)
        s_hi = jnp.sum(jnp.where(lo, 0.0, seg), axis=-1, keepdims=True)
        parts.append(jnp.where(lo, s_lo, s_hi))
    return jnp.concatenate(parts, axis=1)


def _const_spec(shape):
    nd = len(shape)
    return pl.BlockSpec(shape, lambda *_: (0,) * nd, pipeline_mode=pl.Buffered(1))


def _params(sem):
    return pltpu.CompilerParams(dimension_semantics=sem, vmem_limit_bytes=VMEM_LIMIT)


def _proj_kernel(x_ref, g_ref, *refs, n_out, col_chunk):
    w_refs, o_refs = refs[:n_out], refs[n_out:]
    u = _rms(x_ref[...], g_ref[...]).astype(BF16)
    for w_ref, o_ref in zip(w_refs, o_refs):
        n = w_ref.shape[1]
        step = min(col_chunk, n)
        for c0 in range(0, n, step):
            o_ref[:, c0:c0 + step] = jnp.dot(u, w_ref[:, c0:c0 + step], preferred_element_type=F32)


def _norm_proj(x, g, weights, *, tm):
    t, d = x.shape
    assert t % tm == 0
    n_out = len(weights)
    return pl.pallas_call(
        functools.partial(_proj_kernel, n_out=n_out, col_chunk=1024),
        out_shape=[jax.ShapeDtypeStruct((t, w.shape[1]), F32) for w in weights],
        grid=(t // tm,),
        in_specs=[pl.BlockSpec((tm, d), lambda i: (i, 0)), _const_spec((1, d))]
        + [_const_spec(w.shape) for w in weights],
        out_specs=[pl.BlockSpec((tm, w.shape[1]), lambda i: (i, 0)) for w in weights],
        compiler_params=_params(("parallel",)),
        name="norm_proj",
    )(x, g, *weights)


def _ffn_kernel(xa_ref, xb_ref, g_ref, w1_ref, w2_ref, gf_ref, oa_ref, ob_ref, *, n_a, f_chunk, final_norm):
    def run(x_ref, o_ref):
        x = x_ref[...]
        u = _rms(x, g_ref[...]).astype(BF16)
        acc = x
        for c0 in range(0, w1_ref.shape[1], f_chunk):
            h = jnp.dot(u, w1_ref[:, c0:c0 + f_chunk].astype(BF16), preferred_element_type=F32)
            h = jnp.square(jnp.maximum(h, 0.0)).astype(BF16)
            acc = acc + jnp.dot(h, w2_ref[c0:c0 + f_chunk, :].astype(BF16), preferred_element_type=F32)
        if final_norm:
            acc = _rms(acc, gf_ref[...])
        o_ref[...] = acc

    i = pl.program_id(0)

    @pl.when(i < n_a)
    def _():
        run(xa_ref, oa_ref)

    @pl.when(i >= n_a)
    def _():
        run(xb_ref, ob_ref)


def _ffn(xa, xb, g, w1, w2, g_final, *, layer, tm, final_norm):
    d = xa.shape[1]
    f = w1.shape[2]
    assert xa.shape[0] % tm == 0 and xb.shape[0] % tm == 0
    n_a, n_b = xa.shape[0] // tm, xb.shape[0] // tm
    a_map = lambda i: (jnp.minimum(i, n_a - 1), 0)
    b_map = lambda i: (jnp.maximum(i - n_a, 0), 0)
    layer_spec = lambda shape: pl.BlockSpec((None,) + shape, lambda i: (layer, 0, 0), pipeline_mode=pl.Buffered(1))
    return pl.pallas_call(
        functools.partial(_ffn_kernel, n_a=n_a, f_chunk=1024, final_norm=final_norm),
        out_shape=[jax.ShapeDtypeStruct(xa.shape, F32), jax.ShapeDtypeStruct(xb.shape, F32)],
        grid=(n_a + n_b,),
        in_specs=[pl.BlockSpec((tm, d), a_map), pl.BlockSpec((tm, d), b_map), _const_spec((1, d)),
                  layer_spec((d, f)), layer_spec((f, d)), _const_spec((1, d))],
        out_specs=[pl.BlockSpec((tm, d), a_map), pl.BlockSpec((tm, d), b_map)],
        compiler_params=_params(("arbitrary",)),
        name="ffn",
    )(xa, xb, g, w1, w2, g_final)


def _conv_taps(xe_ref, cw, cb, tl):
    nd = len(xe_ref.shape)

    def win(off):
        if nd == 3:
            return xe_ref[:, off:off + tl, :]
        return xe_ref[off:off + tl, :]

    base = CONV_PAD - (CONV_W - 1)
    y = cb + win(base) * cw[0:1, :]
    for k in range(1, CONV_W):
        y = y + win(base + k) * cw[k:k + 1, :]
    return y


def _lru_kernel(x_ref, g_ref, win_ref, cs_ref, h0_ref, cw_ref, cb_ref, wr_ref, br_ref, wi_ref, bi_ref, lam_ref, wo_ref,
                o_ref, cso_ref, ho_ref, xe_ref, hs_ref, hc_ref, *, nb, tl, start_pos):
    c = pl.program_id(1)
    rows = nb * tl
    d = x_ref.shape[2]
    blk = d // LRU_BLOCKS
    ncol = d // V7X_LANES
    pitch = CONV_PAD + tl
    cols = [slice(j * V7X_LANES, (j + 1) * V7X_LANES) for j in range(ncol)]

    @pl.when(c == 0)
    def _():
        for j in range(ncol):
            for s in range(nb):
                xe_ref[j, s * pitch:s * pitch + CONV_PAD, :] = cs_ref[s, :, cols[j]]
        hc_ref[...] = h0_ref[...]

    x = x_ref[...].reshape(rows, d)
    u = _rms(x, g_ref[...]).astype(BF16)
    xb = jnp.dot(u, win_ref[:, 0:d], preferred_element_type=F32)
    for j in range(ncol):
        for s in range(nb):
            xe_ref[j, s * pitch + CONV_PAD:(s + 1) * pitch, :] = xb[s * tl:(s + 1) * tl, cols[j]]

    cw = cw_ref[...]
    cb = cb_ref[...]
    base = CONV_PAD - (CONV_W - 1)
    slabs = []
    for t in range(tl):
        parts = []
        for j in range(ncol):
            acc = cb[:, cols[j]] + xe_ref[j, pl.ds(base + t, nb, stride=pitch), :] * cw[0:1, cols[j]]
            for k in range(1, CONV_W):
                acc = acc + xe_ref[j, pl.ds(base + t + k, nb, stride=pitch), :] * cw[k:k + 1, cols[j]]
            parts.append(acc)
        slabs.append(jnp.concatenate(parts, axis=1))
    xc = jnp.concatenate(slabs, axis=0)
    for j in range(ncol):
        for s in range(nb):
            tail = xe_ref[j, s * pitch + tl:(s + 1) * pitch, :]
            xe_ref[j, s * pitch:s * pitch + CONV_PAD, :] = tail
            cso_ref[s, :, cols[j]] = tail

    xcb = xc.astype(BF16)
    r_pre = jnp.concatenate(
        [jnp.dot(xcb[:, j * blk:(j + 1) * blk], wr_ref[j], preferred_element_type=F32) for j in range(LRU_BLOCKS)],
        axis=1) + br_ref[...]
    i_pre = jnp.concatenate(
        [jnp.dot(xcb[:, j * blk:(j + 1) * blk], wi_ref[j], preferred_element_type=F32) for j in range(LRU_BLOCKS)],
        axis=1) + bi_ref[...]
    r = _sigmoid(r_pre)
    gi = _sigmoid(i_pre)
    log_a = (-LRU_C) * r * _softplus(-lam_ref[...])
    a = jnp.exp(log_a)
    m2 = -jnp.tanh(log_a) * (a * a + 1.0)
    mult = jnp.where(m2 > 0.0, m2 * lax.rsqrt(m2), 0.0)
    if start_pos == 0:
        reset = (lax.broadcasted_iota(jnp.int32, (rows, 1), 0) < nb) & (c == 0)
        a = jnp.where(reset, 0.0, a)
        mult = jnp.where(reset, 1.0, mult)
    b = mult * (gi * xc)

    h = hc_ref[...]
    for t in range(tl):
        h = a[t * nb:(t + 1) * nb] * h + b[t * nb:(t + 1) * nb]
        for j in range(ncol):
            hs_ref[j, pl.ds(t, nb, stride=pitch), :] = h[:, cols[j]]
    hc_ref[...] = h
    ho_ref[...] = h
    h_bm = jnp.concatenate(
        [jnp.concatenate([hs_ref[j, s * pitch:s * pitch + tl, :] for s in range(nb)], axis=0) for j in range(ncol)],
        axis=1)

    gate = jnp.dot(u, win_ref[:, d:2 * d], preferred_element_type=F32)
    y = (h_bm * _gelu_tanh(gate)).astype(BF16)
    o_ref[...] = (x + jnp.dot(y, wo_ref[...], preferred_element_type=F32)).reshape(nb, tl, d)


def _lru_mixer(x3, g, w_in, conv_state, h0, cw, cb, wr, br, wi, bi, lam, wo, *, nb, tl, start_pos):
    batch, seq, d = x3.shape
    nchunk = seq // tl
    assert batch % nb == 0 and seq % tl == 0 and nb % V7X_SUBLANES == 0 and tl % V7X_SUBLANES == 0
    ncol = d // V7X_LANES
    blk_map = lambda i, c: (i, c, 0)
    seq_map = lambda i, c: (i, 0, 0)
    row_map = lambda i, c: (i, 0)
    return pl.pallas_call(
        functools.partial(_lru_kernel, nb=nb, tl=tl, start_pos=start_pos),
        out_shape=[jax.ShapeDtypeStruct((batch, seq, d), F32),
                   jax.ShapeDtypeStruct((batch, CONV_PAD, d), F32),
                   jax.ShapeDtypeStruct((batch, d), F32)],
        grid=(batch // nb, nchunk),
        in_specs=[pl.BlockSpec((nb, tl, d), blk_map), _const_spec(g.shape), _const_spec(w_in.shape),
                  pl.BlockSpec((nb, CONV_PAD, d), seq_map), pl.BlockSpec((nb, d), row_map),
                  _const_spec(cw.shape), _const_spec(cb.shape), _const_spec(wr.shape), _const_spec(br.shape),
                  _const_spec(wi.shape), _const_spec(bi.shape), _const_spec(lam.shape), _const_spec(wo.shape)],
        out_specs=[pl.BlockSpec((nb, tl, d), blk_map), pl.BlockSpec((nb, CONV_PAD, d), seq_map),
                   pl.BlockSpec((nb, d), row_map)],
        scratch_shapes=[pltpu.VMEM((ncol, nb * (CONV_PAD + tl), V7X_LANES), F32),
                        pltpu.VMEM((ncol, nb * (CONV_PAD + tl), V7X_LANES), F32),
                        pltpu.VMEM((nb, d), F32)],
        compiler_params=_params(("parallel", "arbitrary")),
        name="lru_mixer",
    )(x3, g, w_in, conv_state, h0, cw, cb, wr, br, wi, bi, lam, wo)


def _split_bf16(x, terms):
    parts = []
    for _ in range(terms - 1):
        hi = x.astype(BF16).astype(F32)
        parts.append(hi)
        x = x - hi
    parts.append(x)
    return parts


def _ssd_kernel(x_ref, g_ref, pz_ref, pxbc_ref, pdt_ref, cs_ref, s0_ref, cw_ref, cb_ref, dtb_ref, alog_ref, dskx_ref, nw_ref,
                ex_ref, wo_ref, o_ref, cso_ref, so_ref, xe_ref, st_ref, *, q, zero_state, fused_proj):
    c = pl.program_id(1)
    d_ssm = pz_ref.shape[1]
    n = SSM_STATE
    hpg = d_ssm // SSM_HEADDIM // SSM_GROUPS
    gw = hpg * SSM_HEADDIM
    nblk = d_ssm // n

    @pl.when(c == 0)
    def _():
        xe_ref[0:CONV_PAD, :] = cs_ref[0]
        if zero_state:
            st_ref[...] = jnp.zeros(st_ref.shape, F32)
        else:
            for j in range(nblk):
                st_ref[:, j * n:(j + 1) * n] = s0_ref[0, j * n:(j + 1) * n, :].T

    x = x_ref[...]
    if fused_proj:
        u = _rms(x, g_ref[...]).astype(BF16)
        for c0 in range(0, pxbc_ref.shape[1], 1024):
            xe_ref[CONV_PAD:CONV_PAD + q, c0:c0 + 1024] = jnp.dot(u, pxbc_ref[:, c0:c0 + 1024],
                                                                  preferred_element_type=F32)
        dt_pre = jnp.dot(u, pdt_ref[...], preferred_element_type=F32)
    else:
        xe_ref[CONV_PAD:CONV_PAD + q, :] = pxbc_ref[...]
        dt_pre = pdt_ref[...]
    xbc = _silu(_conv_taps(xe_ref, cw_ref[...], cb_ref[...], q))
    tail = xe_ref[q:q + CONV_PAD, :]
    xe_ref[0:CONV_PAD, :] = tail
    cso_ref[0] = tail

    xs = xbc[:, 0:d_ssm]
    bm = xbc[:, d_ssm:d_ssm + SSM_GROUPS * n]
    cm = xbc[:, d_ssm + SSM_GROUPS * n:]
    dt = _softplus(dt_pre + dtb_ref[...])
    da = dt * (-jnp.exp(alog_ref[...]))
    t_in = _row_index(q, q)
    acs = _cumsum_rows(da, t_in, q)
    if q < V7X_LANES:
        acs_sq = jnp.concatenate([acs, jnp.zeros((V7X_LANES - q, V7X_LANES), F32)], axis=0)
    else:
        acs_sq = acs
    acs_t = acs_sq.T

    lhs = jnp.concatenate(_split_bf16(dt, 2) + _split_bf16(acs, 3), axis=0).astype(BF16)
    ex = jnp.dot(lhs, ex_ref[...], preferred_element_type=F32)
    dt_x = ex[0:q] + ex[q:2 * q]
    acs_x = ex[2 * q:3 * q] + ex[3 * q:4 * q] + ex[4 * q:5 * q]
    last_x = acs_x[q - 1:q, :]
    xdt = (xs * dt_x).astype(BF16)
    xdtd = (xs * (dt_x * jnp.exp(last_x - acs_x))).astype(BF16)

    tri = lax.broadcasted_iota(jnp.int32, (q, q), 0) >= lax.broadcasted_iota(jnp.int32, (q, q), 1)
    lane_head = lax.broadcasted_iota(jnp.int32, (1, gw), 1) // SSM_HEADDIM
    groups = range(SSM_GROUPS)
    bgs = [bm[:, g * n:(g + 1) * n].astype(BF16) for g in groups]
    cgs = [cm[:, g * n:(g + 1) * n].astype(BF16) for g in groups]
    cb_mats = [_bdot_nt(cgs[g], bgs[g]) for g in groups]
    y_off = [_bdot(cgs[g], st_ref[:, g * gw:(g + 1) * gw]) for g in groups]
    y_diag = []
    for g in groups:
        xdt_g = xdt[:, g * gw:(g + 1) * gw]
        acc = None
        for e in range(hpg):
            h = g * hpg + e
            seg = acs[:, h:h + 1] - acs_t[h:h + 1, 0:q]
            l_mat = jnp.exp(jnp.where(tri, seg, -jnp.inf))
            part = jnp.dot((cb_mats[g] * l_mat).astype(BF16), jnp.where(lane_head == e, xdt_g, jnp.zeros_like(xdt_g)),
                           preferred_element_type=F32)
            acc = part if acc is None else acc + part
        y_diag.append(acc)
    st_new = [_bdot_tn(bgs[g], xdtd[:, g * gw:(g + 1) * gw]) for g in groups]
    e_last_x = jnp.exp(last_x)
    for g in groups:
        cols = slice(g * gw, (g + 1) * gw)
        st_ref[:, cols] = st_ref[:, cols] * e_last_x[:, cols] + st_new[g]

    @pl.when(c == pl.num_programs(1) - 1)
    def _():
        for j in range(nblk):
            so_ref[0, j * n:(j + 1) * n, :] = st_ref[:, j * n:(j + 1) * n].T

    y = jnp.concatenate(y_diag, axis=1) + jnp.concatenate(y_off, axis=1) * jnp.exp(acs_x) + dskx_ref[...] * xs
    y = y * _silu(jnp.dot(u, pz_ref[...], preferred_element_type=F32) if fused_proj else pz_ref[...])
    norm_parts = []
    for g in groups:
        yg = y[:, g * gw:(g + 1) * gw]
        norm_parts.append(yg * lax.rsqrt(jnp.mean(yg * yg, axis=-1, keepdims=True) + SSM_NORM_EPS))
    yn = (jnp.concatenate(norm_parts, axis=1) * nw_ref[...]).astype(BF16)
    o_ref[...] = x + jnp.dot(yn, wo_ref[...], preferred_element_type=F32)


def _ssd_mixer(x, g, w_z, w_xbc, w_dt, conv_state, s0, cw, cb, dtb, alog, dskx, nw, expand, wo, *, batch, seq, q,
               fused_proj):
    d = x.shape[1]
    d_ssm = w_z.shape[1]
    cdim = w_xbc.shape[1]
    nchunk = seq // q
    assert seq % q == 0
    zero_state = s0 is None
    if zero_state:
        s0 = jnp.zeros((1, d_ssm, SSM_STATE), F32)
    row_map = lambda b, c: (b * nchunk + c, 0)
    seq_map = lambda b, c: (b, 0, 0)
    s0_map = (lambda b, c: (0, 0, 0)) if zero_state else seq_map
    if fused_proj:
        proj = (w_z, w_xbc, w_dt)
        proj_specs = [_const_spec(w.shape) for w in proj]
    else:
        proj = _norm_proj(x, g, [w_z, w_xbc, w_dt], tm=min(256, batch * seq))
        proj_specs = [pl.BlockSpec((q, a.shape[1]), row_map) for a in proj]
    return pl.pallas_call(
        functools.partial(_ssd_kernel, q=q, zero_state=zero_state, fused_proj=fused_proj),
        out_shape=[jax.ShapeDtypeStruct((batch * seq, d), F32),
                   jax.ShapeDtypeStruct((batch, CONV_PAD, cdim), F32),
                   jax.ShapeDtypeStruct((batch, d_ssm, SSM_STATE), F32)],
        grid=(batch, nchunk),
        in_specs=[pl.BlockSpec((q, d), row_map), _const_spec(g.shape)] + proj_specs
        + [pl.BlockSpec((1, CONV_PAD, cdim), seq_map), pl.BlockSpec((1, d_ssm, SSM_STATE), s0_map),
                  _const_spec(cw.shape), _const_spec(cb.shape), _const_spec(dtb.shape), _const_spec(alog.shape),
                  _const_spec(dskx.shape), _const_spec(nw.shape), _const_spec(expand.shape), _const_spec(wo.shape)],
        out_specs=[pl.BlockSpec((q, d), row_map), pl.BlockSpec((1, CONV_PAD, cdim), seq_map),
                   pl.BlockSpec((1, d_ssm, SSM_STATE), seq_map)],
        scratch_shapes=[pltpu.VMEM((CONV_PAD + q, cdim), F32), pltpu.VMEM((SSM_STATE, d_ssm), F32)],
        compiler_params=_params(("parallel", "arbitrary")),
        name="ssd_mixer",
    )(x, g, *proj, conv_state, s0, cw, cb, dtb, alog, dskx, nw, expand, wo)


def _rwkv_proj_kernel(x_ref, sh_ref, g_ref, mu_ref, wrkv_ref, w0_ref, w1_ref, w2_ref, a0_ref, a1_ref, a2_ref,
                      g1_ref, g2_ref, kkw_ref, kaw_ref,
                      r_o, k_o, v_o, kk_o, bb_o, lw_o, g_o, sho_ref, carry_ref, *, nb, tl, time_major):
    c = pl.program_id(1)
    rows = x_ref.shape[0]
    d = x_ref.shape[1]

    @pl.when(c == 0)
    def _():
        carry_ref[...] = sh_ref[...]

    u = _rms(x_ref[...], g_ref[...])
    if time_major:
        prev = carry_ref[...]
        last = u
    else:
        t_in = _row_index(rows, tl)
        prev = jnp.where(t_in == 0, jnp.broadcast_to(carry_ref[...], (nb, tl, d)).reshape(rows, d),
                         pltpu.roll(u, 1, axis=0))
        last = u.reshape(nb, tl, d)[:, tl - 1:tl, :]
    carry_ref[...] = last
    sho_ref[...] = last

    diff = prev - u
    mu = mu_ref[...]

    def mixed(s):
        return (u + diff * mu[s:s + 1, :]).astype(BF16)

    r = jnp.dot(mixed(0), wrkv_ref[0], preferred_element_type=F32)
    k = jnp.dot(mixed(1), wrkv_ref[1], preferred_element_type=F32)
    v_o[...] = jnp.dot(mixed(2), wrkv_ref[2], preferred_element_type=F32).astype(v_o.dtype)
    w_pre = w0_ref[...] + _bdot(jnp.tanh(jnp.dot(mixed(3), w1_ref[...], preferred_element_type=F32)), w2_ref[...])
    sp = jnp.maximum(-w_pre, 0.0) + jnp.log(1.0 + jnp.exp(-jnp.abs(w_pre)))
    lw_o[...] = -jnp.exp(-sp - 0.5)
    a = _sigmoid(a0_ref[...] + _bdot(jnp.dot(mixed(4), a1_ref[...], preferred_element_type=F32), a2_ref[...]))
    g_o[...] = _bdot(_sigmoid(jnp.dot(mixed(5), g1_ref[...], preferred_element_type=F32)),
                     g2_ref[...]).astype(g_o.dtype)
    kk = k * kkw_ref[...]
    kk = kk * jnp.minimum(lax.rsqrt(_head_sum(kk * kk, RWKV_HEAD)), 1e12)
    r_o[...] = r
    k_o[...] = k * (1.0 + (a - 1.0) * kaw_ref[...])
    kk_o[...] = kk
    bb_o[...] = kk * a


def _rwkv_proj(x, shift0, g, mu, wrkv, w0, w1, w2, a0, a1, a2, g1, g2, kkw, kaw, *, batch, seq, nb, tl, time_major):
    d = x.shape[1]
    consts = (g, mu, wrkv, w0, w1, w2, a0, a1, a2, g1, g2, kkw, kaw)
    if time_major:
        rows, grid = batch, (1, seq)
        row_map = lambda i, c: (c, 0)
        sh_shape, sh_block, sh_map = (batch, d), (batch, d), (lambda i, c: (0, 0))
    else:
        nchunk = seq // tl
        assert batch % nb == 0 and seq % tl == 0 and (nb == 1 or nchunk == 1)
        rows, grid = nb * tl, (batch // nb, nchunk)
        row_map = lambda i, c: (i * nchunk + c, 0)
        sh_shape, sh_block, sh_map = (batch, 1, d), (nb, 1, d), (lambda i, c: (i, 0, 0))
    return pl.pallas_call(
        functools.partial(_rwkv_proj_kernel, nb=nb, tl=tl, time_major=time_major),
        out_shape=[jax.ShapeDtypeStruct((batch * seq, d), F32 if time_major or i not in (2, 6) else BF16)
                   for i in range(7)] + [jax.ShapeDtypeStruct(sh_shape, F32)],
        grid=grid,
        in_specs=[pl.BlockSpec((rows, d), row_map), pl.BlockSpec(sh_block, sh_map)]
        + [_const_spec(w.shape) for w in consts],
        out_specs=[pl.BlockSpec((rows, d), row_map)] * 7 + [pl.BlockSpec(sh_block, sh_map)],
        scratch_shapes=[pltpu.VMEM(sh_block, F32)],
        compiler_params=_params(("parallel", "arbitrary")),
        name="rwkv_proj",
    )(x, shift0, *consts)


def _rwkv_chunk_kernel(r_ref, k_ref, v_ref, kk_ref, bb_ref, lw_ref, g_ref, x_ref, s0_ref, rk_ref, lnw_ref, lnb_ref,
                       wo_ref, o_ref, so_ref, s_ref, og_ref, *, nb, tl, chunk, zero_state):
    c = pl.program_id(1)
    d = x_ref.shape[1]
    hd = RWKV_HEAD
    heads = d // hd
    cpb = tl // chunk
    n_iter = nb * cpb

    @pl.when(c == 0)
    def _():
        if zero_state:
            s_ref[...] = jnp.zeros(s_ref.shape, F32)
        else:
            s_ref[...] = s0_ref[...]

    t_in = _row_index(chunk, chunk)
    ri = lax.broadcasted_iota(jnp.int32, (2 * chunk, 2 * chunk), 0)
    ci = lax.broadcasted_iota(jnp.int32, (2 * chunk, 2 * chunk), 1) & (chunk - 1)
    blk_mask = jnp.where(ri < chunk, ri - 1, ri - chunk) >= ci

    def body(it, carry):
        seq_i = it if cpb == 1 else 0
        rows = pl.ds(pl.multiple_of(it * chunk, chunk), chunk)
        r = r_ref[rows, :]
        k = k_ref[rows, :]
        v = v_ref[rows, :].astype(F32)
        kk = kk_ref[rows, :]
        bb = bb_ref[rows, :]
        lw = lw_ref[rows, :]
        cum = _cumsum_rows(lw, t_in, chunk)
        c_last = cum[chunk - 1:chunk, :]
        g_last = jnp.exp(c_last)
        e_inv = jnp.exp(-cum)
        e_end = jnp.exp(c_last - cum)
        p_st = jnp.concatenate([kk * jnp.exp(cum - lw), r * jnp.exp(cum)], axis=0).astype(BF16)
        q_st = jnp.concatenate([bb * e_inv, k * e_inv], axis=0).astype(BF16)
        e_st = jnp.concatenate([k * e_end, bb * e_end], axis=0).astype(BF16)
        zv_st = jnp.concatenate([jnp.zeros_like(v), v], axis=0).astype(BF16)
        hs = range(heads)
        sls = [slice(h * hd, (h + 1) * hd) for h in hs]
        s_l = [s_ref[seq_i, sl, :] for sl in sls]
        g_m = [jnp.where(blk_mask, _bdot_nt(p_st[:, sl], q_st[:, sl]), 0.0) for sl in sls]
        ps = [_bdot_nt(p_st[:, sls[h]], s_l[h]) for h in hs]
        t1 = [_bdot(g_m[h], zv_st[:, sls[h]]) for h in hs]
        nmat = [-g_m[h][0:chunk, 0:chunk] for h in hs]
        w_l = [ps[h][0:chunk] + t1[h][0:chunk] for h in hs]
        u_l = [w_l[h] + _bdot(nmat[h], w_l[h]) for h in hs]
        m = 2
        while m < chunk:
            nmat = [_bdot(x, x) for x in nmat]
            u_l = [u_l[h] + _bdot(nmat[h], u_l[h]) for h in hs]
            m *= 2
        o_parts = [ps[h][chunk:] + t1[h][chunk:] - _bdot(g_m[h][chunk:, 0:chunk], u_l[h]) for h in hs]
        for h in hs:
            vu = jnp.concatenate([v[:, sls[h]], -u_l[h]], axis=0)
            s_ref[seq_i, sls[h], :] = s_l[h] * g_last[:, sls[h]] + _bdot_tn(vu, e_st[:, sls[h]])
        o = jnp.concatenate(o_parts, axis=1)
        mean = _head_sum(o, hd) * (1.0 / hd)
        oc = o - mean
        var = _head_sum(oc * oc, hd) * (1.0 / hd)
        on = oc * lax.rsqrt(var + GN_EPS) * lnw_ref[...] + lnb_ref[...]
        bonus = _head_sum(r * k * rk_ref[...], hd) * v
        og_ref[rows, :] = (on + bonus) * g_ref[rows, :].astype(F32)
        return carry

    lax.fori_loop(0, n_iter, body, 0)

    @pl.when(c == pl.num_programs(1) - 1)
    def _():
        so_ref[...] = s_ref[...]

    o_ref[...] = x_ref[...] + _bdot(og_ref[...], wo_ref[...])


def _rwkv_mixer(x, r, k, v, kk, bb, lw, g, s0, rk, lnw, lnb, wo, *, batch, seq, nb, tl, chunk):
    d = x.shape[1]
    nchunk = seq // tl
    assert batch % nb == 0 and seq % tl == 0 and tl % chunk == 0 and (nb == 1 or tl == chunk)
    rows = nb * tl
    zero_state = s0 is None
    if zero_state:
        s0 = jnp.zeros((nb, d, RWKV_HEAD), F32)
    row_map = lambda i, c: (i * nchunk + c, 0)
    seq_map = lambda i, c: (i, 0, 0)
    s0_map = (lambda i, c: (0, 0, 0)) if zero_state else seq_map
    row_spec = pl.BlockSpec((rows, d), row_map)
    return pl.pallas_call(
        functools.partial(_rwkv_chunk_kernel, nb=nb, tl=tl, chunk=chunk, zero_state=zero_state),
        out_shape=[jax.ShapeDtypeStruct((batch * seq, d), F32), jax.ShapeDtypeStruct((batch, d, RWKV_HEAD), F32)],
        grid=(batch // nb, nchunk),
        in_specs=[row_spec] * 8 + [pl.BlockSpec((nb, d, RWKV_HEAD), s0_map),
                                   _const_spec(rk.shape), _const_spec(lnw.shape), _const_spec(lnb.shape),
                                   _const_spec(wo.shape)],
        out_specs=[row_spec, pl.BlockSpec((nb, d, RWKV_HEAD), seq_map)],
        scratch_shapes=[pltpu.VMEM((nb, d, RWKV_HEAD), F32), pltpu.VMEM((rows, d), F32)],
        compiler_params=_params(("parallel", "arbitrary")),
        name="rwkv_chunk",
    )(r, k, v, kk, bb, lw, g, x, s0, rk, lnw, lnb, wo)


def _rwkv_seq_kernel(r_ref, k_ref, v_ref, kk_ref, bb_ref, lw_ref, s0_ref, o_ref, so_ref, xt_ref, ot_ref, *, seq, batch):
    hd = RWKV_HEAD
    for i, ref in enumerate((r_ref, k_ref, v_ref, kk_ref, bb_ref, lw_ref)):
        for t in range(seq):
            tile = ref[t * batch:(t + 1) * batch, :]
            if ref is lw_ref:
                tile = jnp.exp(tile)
            xt_ref[i * seq + t] = tile.T
    sub = V7X_SUBLANES
    for hh in range(2):
        base = hh * hd
        for t in range(seq):
            src = s0_ref if t == 0 else so_ref
            r_t, k_t, kk_t, bb_t, w_t = (xt_ref[i * seq + t, base:base + hd, :] for i in (0, 1, 3, 4, 5))

            def body(vg, carry, src=src, t=t, base=base, hh=hh, r_t=r_t, k_t=k_t, kk_t=kk_t, bb_t=bb_t, w_t=w_t):
                v0 = pl.multiple_of(base + vg * sub, sub)
                v_tile = xt_ref[2 * seq + t, pl.ds(v0, sub), :]
                rows = []
                for i in range(sub):
                    off = pl.multiple_of((vg * sub + i) * hd, hd)
                    sv = src[hh, pl.ds(off, hd), :]
                    sa = jnp.sum(sv * kk_t, axis=0, keepdims=True)
                    sv = sv * w_t - sa * bb_t + v_tile[i:i + 1, :] * k_t
                    so_ref[hh, pl.ds(off, hd), :] = sv
                    rows.append(jnp.sum(sv * r_t, axis=0, keepdims=True))
                ot_ref[t, pl.ds(v0, sub), :] = jnp.concatenate(rows, axis=0)
                return carry

            lax.fori_loop(0, hd // sub, body, 0)
    for t in range(seq):
        o_ref[t * batch:(t + 1) * batch, :] = ot_ref[t].T


def _rwkv_seq(r, k, v, kk, bb, lw, s0, *, batch, seq):
    d = r.shape[1]
    assert batch == V7X_LANES and d % V7X_LANES == 0
    npair = d // V7X_LANES
    col_spec = pl.BlockSpec((seq * batch, V7X_LANES), lambda j: (0, j))
    st_spec = pl.BlockSpec((2, RWKV_HEAD * RWKV_HEAD, batch), lambda j: (j, 0, 0))
    return pl.pallas_call(
        functools.partial(_rwkv_seq_kernel, seq=seq, batch=batch),
        out_shape=[jax.ShapeDtypeStruct((seq * batch, d), F32), jax.ShapeDtypeStruct(s0.shape, F32)],
        grid=(npair,),
        in_specs=[col_spec] * 6 + [st_spec],
        out_specs=[col_spec, st_spec],
        scratch_shapes=[pltpu.VMEM((6 * seq, V7X_LANES, batch), F32), pltpu.VMEM((seq, V7X_LANES, batch), F32)],
        compiler_params=_params(("parallel",)),
        name="rwkv_seq",
    )(r, k, v, kk, bb, lw, s0)


def _rwkv_finish_kernel(o_ref, r_ref, k_ref, v_ref, g_ref, x_ref, rk_ref, lnw_ref, lnb_ref, wo_ref, out_ref):
    hd = RWKV_HEAD
    o = o_ref[...]
    mean = _head_sum(o, hd) * (1.0 / hd)
    oc = o - mean
    var = _head_sum(oc * oc, hd) * (1.0 / hd)
    on = oc * lax.rsqrt(var + GN_EPS) * lnw_ref[...] + lnb_ref[...]
    bonus = _head_sum(r_ref[...] * k_ref[...] * rk_ref[...], hd) * v_ref[...]
    out_ref[...] = x_ref[...] + _bdot((on + bonus) * g_ref[...], wo_ref[...])


def _rwkv_finish(o, r, k, v, g, x, rk, lnw, lnb, wo, *, tm):
    t, d = x.shape
    assert t % tm == 0
    row_spec = pl.BlockSpec((tm, d), lambda i: (i, 0))
    return pl.pallas_call(
        _rwkv_finish_kernel,
        out_shape=jax.ShapeDtypeStruct((t, d), F32),
        grid=(t // tm,),
        in_specs=[row_spec] * 6 + [_const_spec(rk.shape), _const_spec(lnw.shape), _const_spec(lnb.shape),
                                   _const_spec(wo.shape)],
        out_specs=row_spec,
        compiler_params=_params(("parallel",)),
        name="rwkv_finish",
    )(o, r, k, v, g, x, rk, lnw, lnb, wo)


def _pad_conv_state(state):
    return jnp.pad(state, ((0, 0), (CONV_PAD - (CONV_W - 1), 0), (0, 0)))


class _Group:
    def __init__(self, x3, states, start_pos, cfg):
        self.batch, self.seq, d = x3.shape
        self.x = x3.reshape(self.batch * self.seq, d)
        self.states, self.start_pos, self.cfg = states, start_pos, cfg
        self.out = dict(lru_conv=[], lru_h=[], ssm_conv=[], ssm=[], rwkv_shift=[], rwkv_wkv=[])

    def state(self, name, idx, zero_shape):
        return self.states[name][idx] if self.states else jnp.zeros(zero_shape, F32)


def _mixer_layer(grp, layer, idx, p):
    batch, seq, cfg = grp.batch, grp.seq, grp.cfg
    x = grp.x
    d = x.shape[1]
    g_mix = p["norm_mix"][layer][None, :]
    kind = layer % 3
    if kind == 0:
        cs = grp.state("lru_conv", idx, (batch, CONV_W - 1, d))
        h0 = grp.state("lru_h", idx, (batch, d))
        x3, cso, ho = _lru_mixer(
            x.reshape(batch, seq, d), g_mix, p["lru_w_in"][idx], _pad_conv_state(cs), h0,
            p["lru_conv_w"][idx], p["lru_conv_b"][idx][None, :], p["lru_w_r"][idx], p["lru_b_r"][idx][None, :],
            p["lru_w_i"][idx], p["lru_b_i"][idx][None, :], p["lru_lambda"][idx][None, :], p["lru_w_out"][idx],
            nb=cfg["lru_nb"], tl=cfg["lru_tl"], start_pos=grp.start_pos)
        grp.x = x3.reshape(batch * seq, d)
        grp.out["lru_conv"].append(cso[:, CONV_PAD - (CONV_W - 1):, :])
        grp.out["lru_h"].append(ho)
    elif kind == 1:
        cdim = p["ssm_w_xbc"].shape[2]
        cs = grp.state("ssm_conv", idx, (batch, CONV_W - 1, cdim))
        s0 = grp.states["ssm"][idx].reshape(batch, -1, SSM_STATE) if grp.states else None
        grp.x, cso, so = _ssd_mixer(
            x, g_mix, p["ssm_w_z"][idx], p["ssm_w_xbc"][idx], p["ssm_w_dt"][idx], _pad_conv_state(cs), s0,
            p["ssm_conv_w"][idx], p["ssm_conv_b"][idx][None, :], p["ssm_dt_bias"][idx], p["ssm_a_log"][idx],
            p["ssm_d_x"][idx], p["ssm_norm_w"][idx][None, :], p["ssm_expand"], p["ssm_w_out"][idx],
            batch=batch, seq=seq, q=cfg["ssd_q"], fused_proj=cfg["ssd_fused"])
        grp.out["ssm_conv"].append(cso[:, CONV_PAD - (CONV_W - 1):, :])
        grp.out["ssm"].append(so.reshape(batch, -1, SSM_HEADDIM, SSM_STATE))
    elif cfg["rwkv_stepwise"]:
        heads = d // RWKV_HEAD
        proj_w = (g_mix, p["rwkv_mu"][idx], p["rwkv_w_rkv"][idx], p["rwkv_w0"][idx][None, :],
                  p["rwkv_w_w1"][idx], p["rwkv_w_w2"][idx], p["rwkv_a0"][idx][None, :], p["rwkv_w_a1"][idx],
                  p["rwkv_w_a2"][idx], p["rwkv_w_g1"][idx], p["rwkv_w_g2"][idx], p["rwkv_k_k"][idx][None, :],
                  p["rwkv_k_a"][idx][None, :])
        x_tm = x.reshape(batch, seq, d).transpose(1, 0, 2).reshape(seq * batch, d)
        r, k, v, kk, bb, lw, g, sho = _rwkv_proj(x_tm, grp.state("rwkv_shift", idx, (batch, d)), *proj_w,
                                                 batch=batch, seq=seq, nb=1, tl=1, time_major=True)
        s0 = grp.state("rwkv_wkv", idx, (batch, heads, RWKV_HEAD, RWKV_HEAD))
        s0 = s0.transpose(1, 2, 3, 0).reshape(heads, RWKV_HEAD * RWKV_HEAD, batch)
        o, so = _rwkv_seq(r, k, v, kk, bb, lw, s0, batch=batch, seq=seq)
        x_tm = _rwkv_finish(o, r, k, v, g, x_tm, p["rwkv_r_k"][idx].reshape(1, d), p["rwkv_lnx_w"][idx][None, :],
                            p["rwkv_lnx_b"][idx][None, :], p["rwkv_w_out"][idx], tm=min(256, seq * batch))
        grp.x = x_tm.reshape(seq, batch, d).transpose(1, 0, 2).reshape(batch * seq, d)
        grp.out["rwkv_shift"].append(sho)
        grp.out["rwkv_wkv"].append(so.reshape(heads, RWKV_HEAD, RWKV_HEAD, batch).transpose(3, 0, 1, 2))
    else:
        sh0 = grp.state("rwkv_shift", idx, (batch, d))
        s0 = grp.states["rwkv_wkv"][idx].reshape(batch, d, RWKV_HEAD) if grp.states else None
        r, k, v, kk, bb, lw, g, sho = _rwkv_proj(
            x, sh0[:, None, :], g_mix, p["rwkv_mu"][idx], p["rwkv_w_rkv"][idx], p["rwkv_w0"][idx][None, :],
            p["rwkv_w_w1"][idx], p["rwkv_w_w2"][idx], p["rwkv_a0"][idx][None, :], p["rwkv_w_a1"][idx],
            p["rwkv_w_a2"][idx], p["rwkv_w_g1"][idx], p["rwkv_w_g2"][idx], p["rwkv_k_k"][idx][None, :],
            p["rwkv_k_a"][idx][None, :], batch=batch, seq=seq, nb=cfg["rwkv_proj_nb"], tl=cfg["rwkv_tl"],
            time_major=False)
        grp.x, so = _rwkv_mixer(
            x, r, k, v, kk, bb, lw, g, s0, p["rwkv_r_k"][idx].reshape(1, d), p["rwkv_lnx_w"][idx][None, :],
            p["rwkv_lnx_b"][idx][None, :], p["rwkv_w_out"][idx],
            batch=batch, seq=seq, nb=cfg["rwkv_nb"], tl=cfg["rwkv_tl"], chunk=cfg["rwkv_chunk"])
        grp.out["rwkv_shift"].append(sho[:, 0, :])
        grp.out["rwkv_wkv"].append(so.reshape(batch, d // RWKV_HEAD, RWKV_HEAD, RWKV_HEAD))


def _run_trunk(groups, p, tm_ffn):
    depth = p["norm_mix"].shape[0]
    counts = [0, 0, 0]
    for layer in range(depth):
        kind = layer % 3
        for grp in groups:
            _mixer_layer(grp, layer, counts[kind], p)
        counts[kind] += 1
        groups[0].x, groups[1].x = _ffn(
            groups[0].x, groups[1].x, p["norm_ffn"][layer][None, :], p["ffn_w1"], p["ffn_w2"],
            p["norm_final"][None, :], layer=layer, tm=tm_ffn, final_norm=(layer == depth - 1))
    stack = lambda parts: parts[0][None] if len(parts) == 1 else jnp.stack(parts)
    results = []
    for grp in groups:
        d = grp.x.shape[1]
        results.append([grp.x.reshape(grp.batch, grp.seq, d)]
                       + [stack(grp.out[name]) for name in ("lru_conv", "lru_h", "ssm_conv", "ssm", "rwkv_shift", "rwkv_wkv")])
    return results


def _prep_params(p):
    q = dict(p)
    for name in ("lru_w_in", "lru_w_r", "lru_w_i", "lru_w_out", "ssm_w_out", "rwkv_w_rkv", "rwkv_w_w1", "rwkv_w_w2",
                 "rwkv_w_a1", "rwkv_w_a2", "rwkv_w_g1", "rwkv_w_g2", "rwkv_w_out"):
        q[name] = p[name].astype(BF16)
    d_ssm = p["ssm_w_out"].shape[1]
    heads = p["ssm_dt_bias"].shape[1]
    w_in = p["ssm_w_in"]
    cdim = w_in.shape[2] - d_ssm - heads
    pad = V7X_LANES - heads
    q["ssm_w_z"] = w_in[:, :, :d_ssm].astype(BF16)
    q["ssm_w_xbc"] = w_in[:, :, d_ssm:d_ssm + cdim].astype(BF16)
    q["ssm_w_dt"] = jnp.pad(w_in[:, :, d_ssm + cdim:], ((0, 0), (0, 0), (0, pad))).astype(BF16)
    for name in ("ssm_dt_bias", "ssm_a_log"):
        q[name] = jnp.pad(p[name], ((0, 0), (0, pad)))[:, None, :]
    q["ssm_d_x"] = jnp.repeat(p["ssm_d"], SSM_HEADDIM, axis=1)[:, None, :]
    q["ssm_expand"] = (jnp.arange(V7X_LANES)[:, None] == jnp.arange(d_ssm)[None, :] // SSM_HEADDIM).astype(BF16)
    return q


def kernel(x_prompt, x_sample, state_lru_conv, state_lru_h, state_ssm_conv, state_ssm, state_rwkv_shift, state_rwkv_wkv, norm_mix, norm_ffn, norm_final, lru_w_in, lru_conv_w, lru_conv_b, lru_w_r, lru_b_r, lru_w_i, lru_b_i, lru_lambda, lru_w_out, ssm_w_in, ssm_conv_w, ssm_conv_b, ssm_dt_bias, ssm_a_log, ssm_d, ssm_norm_w, ssm_w_out, rwkv_mu, rwkv_w_rkv, rwkv_w0, rwkv_w_w1, rwkv_w_w2, rwkv_a0, rwkv_w_a1, rwkv_w_a2, rwkv_w_g1, rwkv_w_g2, rwkv_k_k, rwkv_k_a, rwkv_r_k, rwkv_lnx_w, rwkv_lnx_b, rwkv_w_out, ffn_w1, ffn_w2):
    p = _prep_params(dict(
        norm_mix=norm_mix, norm_ffn=norm_ffn, norm_final=norm_final,
        lru_w_in=lru_w_in, lru_conv_w=lru_conv_w, lru_conv_b=lru_conv_b, lru_w_r=lru_w_r, lru_b_r=lru_b_r,
        lru_w_i=lru_w_i, lru_b_i=lru_b_i, lru_lambda=lru_lambda, lru_w_out=lru_w_out,
        ssm_w_in=ssm_w_in, ssm_conv_w=ssm_conv_w, ssm_conv_b=ssm_conv_b, ssm_dt_bias=ssm_dt_bias,
        ssm_a_log=ssm_a_log, ssm_d=ssm_d, ssm_norm_w=ssm_norm_w, ssm_w_out=ssm_w_out,
        rwkv_mu=rwkv_mu, rwkv_w_rkv=rwkv_w_rkv, rwkv_w0=rwkv_w0, rwkv_w_w1=rwkv_w_w1, rwkv_w_w2=rwkv_w_w2,
        rwkv_a0=rwkv_a0, rwkv_w_a1=rwkv_w_a1, rwkv_w_a2=rwkv_w_a2, rwkv_w_g1=rwkv_w_g1, rwkv_w_g2=rwkv_w_g2,
        rwkv_k_k=rwkv_k_k, rwkv_k_a=rwkv_k_a, rwkv_r_k=rwkv_r_k, rwkv_lnx_w=rwkv_lnx_w, rwkv_lnx_b=rwkv_lnx_b,
        rwkv_w_out=rwkv_w_out, ffn_w1=ffn_w1, ffn_w2=ffn_w2))
    seq_p = x_prompt.shape[1]
    seq_s = x_sample.shape[1]
    cfg_p = dict(lru_nb=8, lru_tl=128, ssd_q=min(SSM_CHUNK, seq_p), ssd_fused=True,
                 rwkv_stepwise=False, rwkv_proj_nb=1, rwkv_nb=1, rwkv_tl=512, rwkv_chunk=RWKV_CHUNK)
    cfg_s = dict(lru_nb=32, lru_tl=seq_s, ssd_q=seq_s, ssd_fused=False,
                 rwkv_stepwise=True)
    states = dict(lru_conv=state_lru_conv, lru_h=state_lru_h, ssm_conv=state_ssm_conv, ssm=state_ssm,
                  rwkv_shift=state_rwkv_shift, rwkv_wkv=state_rwkv_wkv)
    groups = [_Group(x_prompt, None, 0, cfg_p), _Group(x_sample, states, PAST_LEN, cfg_s)]
    (y_p, lc_p, lh_p, sc_p, ss_p, rs_p, rw_p), (y_s, lc_s, lh_s, sc_s, ss_s, rs_s, rw_s) = _run_trunk(groups, p, 512)
    return (y_p, y_s, lc_p, lc_s, lh_p, lh_s, sc_p, sc_s, ss_p, ss_s, rs_p, rs_s, rw_p, rw_s)
```
